```python
import jax, jax.numpy as jnp
from jax import lax
import numpy as np

D_MODEL = 1024
BATCH = 8
SEQ = 2048
DEPTH = 1

MIX_WIDTH = D_MODEL
CONV_WIDTH = MIX_WIDTH // 2
ATTN_WIDTH = MIX_WIDTH - CONV_WIDTH
HEAD_DIM = 64
N_HEADS = ATTN_WIDTH // HEAD_DIM
CONV_K = 3
DILATED_PATTERNS = ((128, 1), (512, 4), (2048, 16))
ROPE_THETA = 500000.0
ROT_DIM = HEAD_DIM // 4
N_EXPERTS = 16
CAPACITY_FACTOR = 2
D_FF_EXPERT = 2 * D_MODEL
PROJ_WIDTH = 3 * CONV_WIDTH + 3 * ATTN_WIDTH
RMS_EPS = 1e-6
NEG_INF = -1e30

kernel_name = "hybrid_conv_dilated_attn_ec_moe_encoder"


def rmsnorm(x, g):
    xf = x.astype(jnp.float32)
    y = xf * lax.rsqrt(jnp.mean(xf * xf, axis=-1, keepdims=True) + RMS_EPS)
    return (y * g.astype(jnp.float32)).astype(x.dtype)


def partial_rope(t, positions):
    half = ROT_DIM // 2
    inv_freq = ROPE_THETA ** (-jnp.arange(half, dtype=jnp.float32) / half)
    ang = positions.astype(jnp.float32)[..., None] * inv_freq
    cos = jnp.cos(ang)[:, :, None, :]
    sin = jnp.sin(ang)[:, :, None, :]
    tf = t.astype(jnp.float32)
    t1, t2, rest = tf[..., :half], tf[..., half:ROT_DIM], tf[..., ROT_DIM:]
    out = jnp.concatenate([t1 * cos - t2 * sin, t2 * cos + t1 * sin, rest], axis=-1)
    return out.astype(t.dtype)


def centred_depthwise_conv(u, w):
    s = u.shape[1]
    up = jnp.pad(u, ((0, 0), (1, 1), (0, 0)))
    return w[0] * up[:, 0:s] + w[1] * up[:, 1:s + 1] + w[2] * up[:, 2:s + 2]


def dilated_band_attention(q, k, v, window, dil):
    b, s, h, dh = q.shape
    half = (window // 2) // dil
    blk = half
    L = s // dil
    nb = -(-L // blk)
    lp = nb * blk

    def to_sub(t):
        return t.reshape(b, L, dil, h, dh).transpose(0, 2, 3, 1, 4)

    qs = jnp.pad(to_sub(q), ((0, 0), (0, 0), (0, 0), (0, lp - L), (0, 0)))
    qs = qs.reshape(b, dil, h, nb, blk, dh)

    def key_blocks(t):
        tp = jnp.pad(to_sub(t), ((0, 0), (0, 0), (0, 0), (blk, lp - L + blk), (0, 0)))
        tp = tp.reshape(b, dil, h, nb + 2, blk, dh)
        return jnp.concatenate([tp[:, :, :, :nb], tp[:, :, :, 1:nb + 1], tp[:, :, :, 2:]], axis=-2)

    kb, vb = key_blocks(k), key_blocks(v)
    qpos = np.arange(nb)[:, None, None] * blk + np.arange(blk)[None, :, None]
    kpos = (np.arange(nb)[:, None, None] - 1) * blk + np.arange(3 * blk)[None, None, :]
    mask = (np.abs(qpos - kpos) <= half) & (kpos >= 0) & (kpos < L)

    sc = jnp.einsum('brhnqc,brhnkc->brhnqk', qs.astype(jnp.float32), kb.astype(jnp.float32))
    sc = jnp.where(mask, sc, NEG_INF)
    m = jnp.max(sc, axis=-1)
    p = jnp.where(mask, jnp.exp(sc - m[..., None]), 0.0)
    l = jnp.sum(p, axis=-1)
    num = jnp.einsum('brhnqk,brhnkc->brhnqc', p, vb.astype(jnp.float32))

    def from_sub(t):
        t = t.reshape((b, dil, h, lp) + t.shape[5:])[:, :, :, :L]
        t = jnp.moveaxis(t, 3, 1)
        return t.reshape((b, s, h) + t.shape[4:])

    return from_sub(num), from_sub(m), from_sub(l)


def mixed_dilated_attention(q, k, v):
    nums, ms, ls = [], [], []
    for window, dil in DILATED_PATTERNS:
        n_, m_, l_ = dilated_band_attention(q, k, v, window, dil)
        nums.append(n_); ms.append(m_); ls.append(l_)
    m_all = jnp.stack(ms)
    a = jnp.exp(m_all - jnp.max(m_all, axis=0, keepdims=True))
    den = jnp.sum(a * jnp.stack(ls), axis=0)
    num = jnp.sum(a[..., None] * jnp.stack(nums), axis=0)
    return (num / den[..., None]).astype(q.dtype)


def expert_choice_ffn(h, w_router, w_gate, w_up, w_down):
    b, s, d = h.shape
    cap = CAPACITY_FACTOR * s // N_EXPERTS
    logits = jnp.einsum('bsd,de->bse', h.astype(jnp.float32), w_router.astype(jnp.float32))
    aff = jax.nn.softmax(logits, axis=-1)
    gates, idx = lax.top_k(jnp.swapaxes(aff, 1, 2), cap)
    bidx = jnp.arange(b)[:, None, None]
    xe = h[bidx, idx]
    a = jnp.einsum('becd,edf->becf', xe, w_gate)
    u = jnp.einsum('becd,edf->becf', xe, w_up)
    y = jnp.einsum('becf,efd->becd', jax.nn.silu(a) * u, w_down)
    y = y * gates[..., None].astype(y.dtype)
    return jnp.zeros_like(h).at[bidx, idx].add(y)


def setup_inputs(seed: int = 0) -> dict:
    key = jax.random.key(seed)
    ks = jax.random.split(key, 16)
    f32 = jnp.float32
    nrm = lambda k, shape, fan: jax.random.normal(k, shape, f32) * (fan ** -0.5)
    gain = lambda k, shape: 1.0 + 0.02 * jax.random.normal(k, shape, f32)
    x = jax.random.normal(ks[0], (BATCH, SEQ, D_MODEL), f32)
    offsets = jax.random.randint(ks[1], (BATCH, 1), 0, 4096, dtype=jnp.int32)
    positions = offsets + jnp.arange(SEQ, dtype=jnp.int32)[None, :]
    return {
        "x": x,
        "positions": positions,
        "g_mix": gain(ks[2], (DEPTH, D_MODEL)),
        "w_in": nrm(ks[3], (DEPTH, D_MODEL, PROJ_WIDTH), D_MODEL),
        "conv_w": nrm(ks[4], (DEPTH, CONV_K, CONV_WIDTH), CONV_K),
        "g_conv_out": gain(ks[5], (DEPTH, CONV_WIDTH)),
        "g_attn_out": gain(ks[6], (DEPTH, ATTN_WIDTH)),
        "w_out": nrm(ks[7], (DEPTH, MIX_WIDTH, D_MODEL), MIX_WIDTH),
        "g_ffn": gain(ks[8], (DEPTH, D_MODEL)),
        "w_router": nrm(ks[9], (DEPTH, D_MODEL, N_EXPERTS), D_MODEL),
        "w_gate": nrm(ks[10], (DEPTH, N_EXPERTS, D_MODEL, D_FF_EXPERT), D_MODEL),
        "w_up": nrm(ks[11], (DEPTH, N_EXPERTS, D_MODEL, D_FF_EXPERT), D_MODEL),
        "w_down": nrm(ks[12], (DEPTH, N_EXPERTS, D_FF_EXPERT, D_MODEL), D_FF_EXPERT),
        "g_final": gain(ks[13], (D_MODEL,)),
    }


def reference(x, positions, g_mix, w_in, conv_w, g_conv_out, g_attn_out, w_out,
              g_ffn, w_router, w_gate, w_up, w_down, g_final):
    b, s, _ = x.shape
    splits = np.cumsum([CONV_WIDTH] * 3 + [ATTN_WIDTH] * 2).tolist()
    for layer in range(DEPTH):
        h = rmsnorm(x, g_mix[layer])
        p = jnp.einsum('bsd,dp->bsp', h, w_in[layer])
        c_b, c_c, c_u, q, k, v = jnp.split(p, splits, axis=-1)
        y_conv = c_b * centred_depthwise_conv(c_c * c_u, conv_w[layer])
        q = partial_rope(q.reshape(b, s, N_HEADS, HEAD_DIM), positions) * (HEAD_DIM ** -0.5)
        k = partial_rope(k.reshape(b, s, N_HEADS, HEAD_DIM), positions)
        v = v.reshape(b, s, N_HEADS, HEAD_DIM)
        y_attn = mixed_dilated_attention(q, k, v).reshape(b, s, ATTN_WIDTH)
        y = jnp.concatenate([rmsnorm(y_conv, g_conv_out[layer]),
                             rmsnorm(y_attn, g_attn_out[layer])], axis=-1)
        x = x + jnp.einsum('bsm,md->bsd', y, w_out[layer])
        h2 = rmsnorm(x, g_ffn[layer])
        x = x + expert_choice_ffn(h2, w_router[layer], w_gate[layer], w_up[layer], w_down[layer])
    return rmsnorm(x, g_final)
```

```python
import functools

import jax
import jax.numpy as jnp
from jax import lax
from jax.experimental import pallas as pl
from jax.experimental.pallas import tpu as pltpu

D_MODEL = 1024
BATCH = 8
SEQ = 2048
TOKENS = BATCH * SEQ
CONV_WIDTH = 512
ATTN_WIDTH = 512
HEAD_DIM = 64
N_HEADS = 8
DILATED_PATTERNS = ((128, 1), (512, 4), (2048, 16))
ROPE_THETA = 500000.0
ROT_DIM = 16
N_EXPERTS = 16
CAPACITY = 2 * SEQ // N_EXPERTS
D_FF = 2 * D_MODEL
PROJ_WIDTH = 3 * CONV_WIDTH + 3 * ATTN_WIDTH
RMS_EPS = 1e-6
NEG_INF = -1e30

LANES = 128
ROW_TILE = 512
FF_TILE = 512
BAND_HALF = 64
Q_CHUNK = 2 * BAND_HALF
K_WIN = 4 * BAND_HALF
HEADS_PER_STEP = LANES // HEAD_DIM
MIB = 1024 * 1024

f32 = jnp.float32
bf16 = jnp.bfloat16


def _params(semantics, vmem_mib):
    return pltpu.CompilerParams(dimension_semantics=semantics, vmem_limit_bytes=vmem_mib * MIB)


def _rms(x, g):
    return x * lax.rsqrt(jnp.mean(x * x, axis=-1, keepdims=True) + RMS_EPS) * g


def _in_proj_kernel(x_ref, g_ref, w_ref, pc_ref, qkv_ref):
    h = _rms(x_ref[...], g_ref[...]).astype(bf16)
    p = jnp.dot(h, w_ref[...], preferred_element_type=f32)
    pc_ref[...] = p[:, :3 * CONV_WIDTH]
    qkv_ref[...] = p[:, 3 * CONV_WIDTH:]


def _in_proj(x2, g_mix, w_in_bf):
    n = TOKENS // ROW_TILE
    return pl.pallas_call(
        _in_proj_kernel,
        grid=(n,),
        in_specs=[
            pl.BlockSpec((ROW_TILE, D_MODEL), lambda i: (i, 0)),
            pl.BlockSpec((1, D_MODEL), lambda i: (0, 0)),
            pl.BlockSpec((D_MODEL, PROJ_WIDTH), lambda i: (0, 0)),
        ],
        out_specs=[
            pl.BlockSpec((ROW_TILE, 3 * CONV_WIDTH), lambda i: (i, 0)),
            pl.BlockSpec((ROW_TILE, 3 * ATTN_WIDTH), lambda i: (i, 0)),
        ],
        out_shape=[
            jax.ShapeDtypeStruct((TOKENS, 3 * CONV_WIDTH), f32),
            jax.ShapeDtypeStruct((TOKENS, 3 * ATTN_WIDTH), f32),
        ],
        compiler_params=_params(("arbitrary",), 48),
        name="in_proj",
    )(x2, g_mix, w_in_bf)


CONV_PAD = 8
CONV_ROWS = 512


def _conv_kernel(pc_ref, w_ref, g_ref, o_ref, u_ref):
    zeros = jnp.zeros((CONV_PAD, CONV_WIDTH), f32)
    u_ref[pl.ds(0, CONV_PAD), :] = zeros
    u_ref[pl.ds(CONV_PAD + SEQ, CONV_PAD), :] = zeros
    for i in range(SEQ // CONV_ROWS):
        rows = pl.ds(i * CONV_ROWS, CONV_ROWS)
        u_ref[pl.ds(CONV_PAD + i * CONV_ROWS, CONV_ROWS), :] = (
            pc_ref[rows, CONV_WIDTH:2 * CONV_WIDTH] * pc_ref[rows, 2 * CONV_WIDTH:])
    w = w_ref[...]
    g = g_ref[...]
    for i in range(SEQ // CONV_ROWS):
        base = CONV_PAD + i * CONV_ROWS
        prev = u_ref[pl.ds(base - 1, CONV_ROWS), :]
        cur = u_ref[pl.ds(base, CONV_ROWS), :]
        nxt = u_ref[pl.ds(base + 1, CONV_ROWS), :]
        y = pc_ref[pl.ds(i * CONV_ROWS, CONV_ROWS), :CONV_WIDTH] * (
            w[0:1] * prev + w[1:2] * cur + w[2:3] * nxt)
        o_ref[pl.ds(i * CONV_ROWS, CONV_ROWS), :] = _rms(y, g).astype(bf16)


def _conv(pc, conv_w, g_conv):
    return pl.pallas_call(
        _conv_kernel,
        grid=(BATCH,),
        in_specs=[
            pl.BlockSpec((SEQ, 3 * CONV_WIDTH), lambda b: (b, 0)),
            pl.BlockSpec((3, CONV_WIDTH), lambda b: (0, 0)),
            pl.BlockSpec((1, CONV_WIDTH), lambda b: (0, 0)),
        ],
        out_specs=pl.BlockSpec((SEQ, CONV_WIDTH), lambda b: (b, 0)),
        out_shape=jax.ShapeDtypeStruct((TOKENS, CONV_WIDTH), bf16),
        scratch_shapes=[pltpu.VMEM((SEQ + 2 * CONV_PAD, CONV_WIDTH), f32)],
        compiler_params=_params(("arbitrary",), 48),
        name="conv",
    )(pc, conv_w, g_conv)


ROPE_ROWS = 256


def _attn_kernel(pos_ref, invf_ref, q_ref, k_ref, v_ref, o_ref,
                 cos_ref, s1_ref, s2_ref, qs_ref, ks_ref,
                 qlo_ref, qhi_ref, kpad_ref, vlo_ref, vhi_ref,
                 num_ref, m_ref, l_ref):
    hp = pl.program_id(1)

    @pl.when(hp == 0)
    def _():
        def body(i, c):
            rows = pl.ds(pl.multiple_of(i * ROPE_ROWS, ROPE_ROWS), ROPE_ROWS)
            ang = pos_ref[rows, :].astype(f32) * invf_ref[...]
            lane = lax.broadcasted_iota(jnp.int32, (ROPE_ROWS, LANES), 1) % HEAD_DIM
            sn = jnp.sin(ang)
            cos_ref[rows, :] = jnp.cos(ang)
            s1_ref[rows, :] = jnp.where((lane >= ROT_DIM // 2) & (lane < ROT_DIM), sn, 0.0)
            s2_ref[rows, :] = jnp.where(lane < ROT_DIM // 2, -sn, 0.0)
            return c
        lax.fori_loop(0, SEQ // ROPE_ROWS, body, 0)

    def rope_body(i, c):
        rows = pl.ds(pl.multiple_of(i * ROPE_ROWS, ROPE_ROWS), ROPE_ROWS)
        cs, s1, s2 = cos_ref[rows, :], s1_ref[rows, :], s2_ref[rows, :]
        for src, dst, scale in ((q_ref, qs_ref, HEAD_DIM ** -0.5), (k_ref, ks_ref, None)):
            x = src[rows, :]
            y = (x * cs + pltpu.roll(x, ROT_DIM // 2, 1) * s1
                 + pltpu.roll(x, LANES - ROT_DIM // 2, 1) * s2)
            dst[rows, :] = y if scale is None else y * scale
        return c
    lax.fori_loop(0, SEQ // ROPE_ROWS, rope_body, 0)

    lane_lo = lax.broadcasted_iota(jnp.int32, (Q_CHUNK, LANES), 1) < HEAD_DIM
    col = lax.broadcasted_iota(jnp.int32, (Q_CHUNK, K_WIN), 1)
    rel = col - lax.broadcasted_iota(jnp.int32, (Q_CHUNK, K_WIN), 0)
    band = (rel >= 0) & (rel <= 2 * BAND_HALF)
    zero_pad = jnp.zeros((BAND_HALF, LANES), bf16)
    for ref in (kpad_ref, vlo_ref, vhi_ref):
        ref[pl.ds(0, BAND_HALF), :] = zero_pad

    for pat, (_, dil) in enumerate(DILATED_PATTERNS):
        length = SEQ // dil
        n_chunks = length // Q_CHUNK
        for ref in (kpad_ref, vlo_ref, vhi_ref):
            ref[pl.ds(BAND_HALF + length, BAND_HALF), :] = zero_pad

        def residue(r, c, dil=dil, length=length, n_chunks=n_chunks, pat=pat):
            def src_rows(i):
                if dil == 1:
                    return pl.ds(pl.multiple_of(i * Q_CHUNK, Q_CHUNK), Q_CHUNK)
                return pl.ds(r + dil * Q_CHUNK * i, Q_CHUNK, stride=dil)

            def stage(i, c2):
                src = src_rows(i)
                dst = pl.ds(pl.multiple_of(i * Q_CHUNK, Q_CHUNK), Q_CHUNK)
                dstp = pl.ds(pl.multiple_of(i * Q_CHUNK + BAND_HALF, BAND_HALF), Q_CHUNK)
                qf, kf, vf = qs_ref[src, :], ks_ref[src, :], v_ref[src, :]
                qlo_ref[dst, :] = jnp.where(lane_lo, qf, 0.0).astype(bf16)
                qhi_ref[dst, :] = jnp.where(lane_lo, 0.0, qf).astype(bf16)
                kpad_ref[dstp, :] = kf.astype(bf16)
                vlo_ref[dstp, :] = jnp.where(lane_lo, vf, 0.0).astype(bf16)
                vhi_ref[dstp, :] = jnp.where(lane_lo, 0.0, vf).astype(bf16)
                return c2
            lax.fori_loop(0, n_chunks, stage, 0)

            def chunk(j, c2):
                row0 = pl.multiple_of(j * Q_CHUNK, Q_CHUNK)
                kw = kpad_ref[pl.ds(row0, K_WIN), :]
                lo = BAND_HALF - j * Q_CHUNK
                valid = band & (col >= lo) & (col < lo + length)
                nums, ms, ls = [], [], []
                for q_half, v_half in ((qlo_ref, vlo_ref), (qhi_ref, vhi_ref)):
                    s = lax.dot_general(q_half[pl.ds(row0, Q_CHUNK), :], kw,
                                        (((1,), (1,)), ((), ())), preferred_element_type=f32)
                    s = jnp.where(valid, s, NEG_INF)
                    m = jnp.max(s, axis=1, keepdims=True)
                    p = jnp.where(valid, jnp.exp(s - m), 0.0)
                    ls.append(jnp.sum(p, axis=1, keepdims=True))
                    ms.append(m)
                    nums.append(jnp.dot(p.astype(bf16), v_half[pl.ds(row0, K_WIN), :],
                                        preferred_element_type=f32))
                out_rows = src_rows(j)
                num_ref[pat, out_rows, :] = nums[0] + nums[1]
                m_ref[pat, out_rows, :] = jnp.where(lane_lo, ms[0], ms[1])
                l_ref[pat, out_rows, :] = jnp.where(lane_lo, ls[0], ls[1])
                return c2
            lax.fori_loop(0, n_chunks, chunk, 0)
            return c

        if dil == 1:
            residue(0, 0)
        else:
            lax.fori_loop(0, dil, residue, 0)

    def mix(i, c):
        rows = pl.ds(pl.multiple_of(i * ROPE_ROWS, ROPE_ROWS), ROPE_ROWS)
        ms = [m_ref[p, rows, :] for p in range(len(DILATED_PATTERNS))]
        m_all = functools.reduce(jnp.maximum, ms)
        num = jnp.zeros((ROPE_ROWS, LANES), f32)
        den = jnp.zeros((ROPE_ROWS, LANES), f32)
        for p in range(len(DILATED_PATTERNS)):
            a = jnp.exp(ms[p] - m_all)
            den = den + a * l_ref[p, rows, :]
            num = num + a * num_ref[p, rows, :]
        o_ref[rows, :] = num / den
        return c
    lax.fori_loop(0, SEQ // ROPE_ROWS, mix, 0)


def _attention(qkv3, pos3, invf):
    n_hp = N_HEADS // HEADS_PER_STEP
    blk = lambda off: pl.BlockSpec((None, SEQ, LANES), lambda b, h, off=off: (b, 0, off + h))
    pad_rows = SEQ + 2 * BAND_HALF
    return pl.pallas_call(
        _attn_kernel,
        grid=(BATCH, n_hp),
        in_specs=[
            pl.BlockSpec((None, SEQ, 1), lambda b, h: (b, 0, 0)),
            pl.BlockSpec((1, LANES), lambda b, h: (0, 0)),
            blk(0), blk(n_hp), blk(2 * n_hp),
        ],
        out_specs=pl.BlockSpec((None, SEQ, LANES), lambda b, h: (b, 0, h)),
        out_shape=jax.ShapeDtypeStruct((BATCH, SEQ, ATTN_WIDTH), f32),
        scratch_shapes=[
            pltpu.VMEM((SEQ, LANES), f32), pltpu.VMEM((SEQ, LANES), f32), pltpu.VMEM((SEQ, LANES), f32),
            pltpu.VMEM((SEQ, LANES), f32), pltpu.VMEM((SEQ, LANES), f32),
            pltpu.VMEM((SEQ, LANES), bf16), pltpu.VMEM((SEQ, LANES), bf16),
            pltpu.VMEM((pad_rows, LANES), bf16), pltpu.VMEM((pad_rows, LANES), bf16),
            pltpu.VMEM((pad_rows, LANES), bf16),
            pltpu.VMEM((len(DILATED_PATTERNS), SEQ, LANES), f32),
            pltpu.VMEM((len(DILATED_PATTERNS), SEQ, LANES), f32),
            pltpu.VMEM((len(DILATED_PATTERNS), SEQ, LANES), f32),
        ],
        compiler_params=_params(("arbitrary", "arbitrary"), 48),
        name="attention",
    )(pos3, invf, qkv3, qkv3, qkv3)


def _out_proj_kernel(yc_ref, ya_ref, x_ref, w_ref, ga_ref, gf_ref, wr_ref,
                     x1_ref, h2_ref, lg_ref):
    ya = _rms(ya_ref[...], ga_ref[...]).astype(bf16)
    mix = (jnp.dot(yc_ref[...], w_ref[pl.ds(0, CONV_WIDTH), :], preferred_element_type=f32)
           + jnp.dot(ya, w_ref[pl.ds(CONV_WIDTH, ATTN_WIDTH), :], preferred_element_type=f32))
    x1 = x_ref[...] + mix
    x1_ref[...] = x1
    h2 = _rms(x1, gf_ref[...])
    h2_ref[...] = h2.astype(bf16)
    lg_ref[...] = lax.dot_general(wr_ref[...], h2, (((1,), (1,)), ((), ())),
                                  precision=lax.Precision.HIGHEST, preferred_element_type=f32)


def _out_proj(yc, ya, x2, w_out_bf, g_attn, g_ffn, w_router_t):
    n = TOKENS // ROW_TILE
    return pl.pallas_call(
        _out_proj_kernel,
        grid=(n,),
        in_specs=[
            pl.BlockSpec((ROW_TILE, CONV_WIDTH), lambda i: (i, 0)),
            pl.BlockSpec((ROW_TILE, ATTN_WIDTH), lambda i: (i, 0)),
            pl.BlockSpec((ROW_TILE, D_MODEL), lambda i: (i, 0)),
            pl.BlockSpec((D_MODEL, D_MODEL), lambda i: (0, 0)),
            pl.BlockSpec((1, ATTN_WIDTH), lambda i: (0, 0)),
            pl.BlockSpec((1, D_MODEL), lambda i: (0, 0)),
            pl.BlockSpec((N_EXPERTS, D_MODEL), lambda i: (0, 0)),
        ],
        out_specs=[
            pl.BlockSpec((ROW_TILE, D_MODEL), lambda i: (i, 0)),
            pl.BlockSpec((ROW_TILE, D_MODEL), lambda i: (i, 0)),
            pl.BlockSpec((N_EXPERTS, ROW_TILE), lambda i: (0, i)),
        ],
        out_shape=[
            jax.ShapeDtypeStruct((TOKENS, D_MODEL), f32),
            jax.ShapeDtypeStruct((TOKENS, D_MODEL), bf16),
            jax.ShapeDtypeStruct((N_EXPERTS, TOKENS), f32),
        ],
        compiler_params=_params(("arbitrary",), 48),
        name="out_proj",
    )(yc, ya, x2, w_out_bf, g_attn, g_ffn, w_router_t)


N_LANE_BLOCKS = SEQ // LANES


def _excl_cumsum_tokens(x, tri):
    stacked = jnp.concatenate([x[:, j * LANES:(j + 1) * LANES] for j in range(N_LANE_BLOCKS)], axis=0)
    within = jnp.dot(stacked.astype(bf16), tri, preferred_element_type=f32)
    totals = jnp.sum(stacked, axis=1, keepdims=True)
    out, offset = [], jnp.zeros((N_EXPERTS, 1), f32)
    for j in range(N_LANE_BLOCKS):
        rows = slice(j * N_EXPERTS, (j + 1) * N_EXPERTS)
        out.append(within[rows] + offset)
        offset = offset + totals[rows]
    return jnp.concatenate(out, axis=1)


def _route_kernel(lg_ref, rank_ref, aff_ref):
    x = lg_ref[...]
    e = jnp.exp(x - jnp.max(x, axis=0, keepdims=True))
    aff = e / jnp.sum(e, axis=0, keepdims=True)
    aff_ref[...] = aff
    keys = pltpu.bitcast(aff, jnp.int32)
    thr = jnp.zeros((N_EXPERTS, 1), jnp.int32)
    for bit in range(30, -1, -1):
        cand = thr | (1 << bit)
        cnt = jnp.sum(jnp.where(keys >= cand, 1.0, 0.0), axis=1, keepdims=True)
        thr = jnp.where(cnt >= float(CAPACITY), cand, thr)
    above = keys > thr
    tie = keys == thr
    need = float(CAPACITY) - jnp.sum(jnp.where(above, 1.0, 0.0), axis=1, keepdims=True)
    tri = jnp.where(lax.broadcasted_iota(jnp.int32, (LANES, LANES), 0)
                    < lax.broadcasted_iota(jnp.int32, (LANES, LANES), 1), 1.0, 0.0).astype(bf16)
    tie_rank = _excl_cumsum_tokens(jnp.where(tie, 1.0, 0.0), tri)
    sel = above | (tie & (tie_rank < need))
    rank = _excl_cumsum_tokens(jnp.where(sel, 1.0, 0.0), tri)
    rank_ref[...] = jnp.where(sel, rank, -1.0)


def _route(logits_t):
    blk = pl.BlockSpec((None, N_EXPERTS, SEQ), lambda b: (b, 0, 0))
    return pl.pallas_call(
        _route_kernel,
        grid=(BATCH,),
        in_specs=[pl.BlockSpec((N_EXPERTS, SEQ), lambda b: (0, b))],
        out_specs=[blk, blk],
        out_shape=[jax.ShapeDtypeStruct((BATCH, N_EXPERTS, SEQ), f32)] * 2,
        compiler_params=_params(("arbitrary",), 32),
        name="route",
    )(logits_t)


def _gather_kernel(rank_ref, h2_ref, xe_ref):
    e = pl.program_id(1)
    r = rank_ref[pl.ds(e, 1), :]
    slot = lax.broadcasted_iota(jnp.int32, (CAPACITY, SEQ), 0).astype(f32)
    onehot = jnp.where(r == slot, 1.0, 0.0).astype(bf16)
    xe_ref[...] = jnp.dot(onehot, h2_ref[...], preferred_element_type=f32).astype(bf16)


def _gather(rank, h2):
    return pl.pallas_call(
        _gather_kernel,
        grid=(BATCH, N_EXPERTS),
        in_specs=[
            pl.BlockSpec((None, N_EXPERTS, SEQ), lambda b, e: (b, 0, 0)),
            pl.BlockSpec((SEQ, D_MODEL), lambda b, e: (b, 0)),
        ],
        out_specs=pl.BlockSpec((None, None, CAPACITY, D_MODEL), lambda b, e: (e, b, 0, 0)),
        out_shape=jax.ShapeDtypeStruct((N_EXPERTS, BATCH, CAPACITY, D_MODEL), bf16),
        compiler_params=_params(("arbitrary", "arbitrary"), 40),
        name="gather",
    )(rank, h2)


def _ffn_kernel(xe_ref, wg_ref, wu_ref, wd_ref, y_ref, acc_ref):
    f = pl.program_id(1)
    xe = xe_ref[...]
    a = jnp.dot(xe, wg_ref[...].astype(bf16), preferred_element_type=f32)
    u = jnp.dot(xe, wu_ref[...].astype(bf16), preferred_element_type=f32)
    hidden = (a * (1.0 / (1.0 + jnp.exp(-a))) * u).astype(bf16)
    part = jnp.dot(hidden, wd_ref[...].astype(bf16), preferred_element_type=f32)

    @pl.when(f == 0)
    def _():
        acc_ref[...] = part

    @pl.when(f != 0)
    def _():
        acc_ref[...] += part

    @pl.when(f == D_FF // FF_TILE - 1)
    def _():
        y_ref[...] = acc_ref[...].astype(bf16)


def _ffn(xe, w_gate, w_up, w_down):
    rows = BATCH * CAPACITY
    return pl.pallas_call(
        _ffn_kernel,
        grid=(N_EXPERTS, D_FF // FF_TILE),
        in_specs=[
            pl.BlockSpec((None, rows, D_MODEL), lambda e, f: (e, 0, 0)),
            pl.BlockSpec((None, D_MODEL, FF_TILE), lambda e, f: (e, 0, f)),
            pl.BlockSpec((None, D_MODEL, FF_TILE), lambda e, f: (e, 0, f)),
            pl.BlockSpec((None, FF_TILE, D_MODEL), lambda e, f: (e, f, 0)),
        ],
        out_specs=pl.BlockSpec((None, rows, D_MODEL), lambda e, f: (e, 0, 0)),
        out_shape=jax.ShapeDtypeStruct((N_EXPERTS, rows, D_MODEL), bf16),
        scratch_shapes=[pltpu.VMEM((rows, D_MODEL), f32)],
        compiler_params=_params(("arbitrary", "arbitrary"), 56),
        name="ffn",
    )(xe, w_gate, w_up, w_down)


def _combine_kernel(rank_ref, aff_ref, y_ref, x1_ref, g_ref, o_ref, acc_ref):
    e = pl.program_id(1)

    @pl.when(e == 0)
    def _():
        acc_ref[...] = x1_ref[...]

    r = rank_ref[pl.ds(e, 1), :]
    gate = aff_ref[pl.ds(e, 1), :]
    slot = lax.broadcasted_iota(jnp.int32, (CAPACITY, SEQ), 0).astype(f32)
    weights = jnp.where(r == slot, gate, 0.0).astype(bf16)
    acc_ref[...] += lax.dot_general(weights, y_ref[...], (((0,), (0,)), ((), ())),
                                    preferred_element_type=f32)

    @pl.when(e == N_EXPERTS - 1)
    def _():
        o_ref[...] = _rms(acc_ref[...], g_ref[...])


def _combine(rank, aff, y, x1, g_final):
    route_blk = pl.BlockSpec((None, N_EXPERTS, SEQ), lambda b, e: (b, 0, 0))
    return pl.pallas_call(
        _combine_kernel,
        grid=(BATCH, N_EXPERTS),
        in_specs=[
            route_blk, route_blk,
            pl.BlockSpec((None, None, CAPACITY, D_MODEL), lambda b, e: (e, b, 0, 0)),
            pl.BlockSpec((SEQ, D_MODEL), lambda b, e: (b, 0)),
            pl.BlockSpec((1, D_MODEL), lambda b, e: (0, 0)),
        ],
        out_specs=pl.BlockSpec((SEQ, D_MODEL), lambda b, e: (b, 0)),
        out_shape=jax.ShapeDtypeStruct((TOKENS, D_MODEL), f32),
        scratch_shapes=[pltpu.VMEM((SEQ, D_MODEL), f32)],
        compiler_params=_params(("arbitrary", "arbitrary"), 56),
        name="combine",
    )(rank, aff, y, x1, g_final)


def _rope_lane_freqs():
    half = ROT_DIM // 2
    inv_freq = ROPE_THETA ** (-jnp.arange(half, dtype=f32) / half)
    per_head = jnp.concatenate([inv_freq, inv_freq, jnp.zeros((HEAD_DIM - ROT_DIM,), f32)])
    return jnp.tile(per_head, HEADS_PER_STEP).reshape(1, LANES)


def kernel(x, positions, g_mix, w_in, conv_w, g_conv_out, g_attn_out, w_out, g_ffn, w_router,
           w_gate, w_up, w_down, g_final):
    x2 = x.reshape(TOKENS, D_MODEL)
    pc, qkv = _in_proj(x2, g_mix[0].reshape(1, D_MODEL), w_in[0].astype(bf16))
    yc = _conv(pc, conv_w[0], g_conv_out[0].reshape(1, CONV_WIDTH))
    ya = _attention(qkv.reshape(BATCH, SEQ, 3 * ATTN_WIDTH), positions.reshape(BATCH, SEQ, 1),
                    _rope_lane_freqs())
    x1, h2, logits_t = _out_proj(yc, ya.reshape(TOKENS, ATTN_WIDTH), x2, w_out[0].astype(bf16),
                                 g_attn_out[0].reshape(1, ATTN_WIDTH), g_ffn[0].reshape(1, D_MODEL),
                                 w_router[0].T)
    rank, aff = _route(logits_t)
    xe = _gather(rank, h2)
    y = _ffn(xe.reshape(N_EXPERTS, BATCH * CAPACITY, D_MODEL), w_gate[0], w_up[0], w_down[0])
    out = _combine(rank, aff, y.reshape(N_EXPERTS, BATCH, CAPACITY, D_MODEL), x1,
                   g_final.reshape(1, D_MODEL))
    return out.reshape(BATCH, SEQ, D_MODEL)
```

```python
import functools

import jax
import jax.numpy as jnp
from jax import lax
from jax.experimental import pallas as pl
from jax.experimental.pallas import tpu as pltpu

D_MODEL = 1024
BATCH = 8
SEQ = 2048
TOKENS = BATCH * SEQ
CONV_WIDTH = 512
ATTN_WIDTH = 512
HEAD_DIM = 64
N_HEADS = 8
DILATED_PATTERNS = ((128, 1), (512, 4), (2048, 16))
N_PATTERNS = len(DILATED_PATTERNS)
ROPE_THETA = 500000.0
ROT_DIM = 16
ROT_HALF = ROT_DIM // 2
N_EXPERTS = 16
CAPACITY = 2 * SEQ // N_EXPERTS
D_FF = 2 * D_MODEL
PROJ_WIDTH = 3 * CONV_WIDTH + 3 * ATTN_WIDTH
RMS_EPS = 1e-6
NEG_INF = -1e30

LANES = 128
ROW_TILE = 512
FF_TILE = 512
BAND_HALF = 64
Q_CHUNK = 2 * BAND_HALF
K_WIN = 4 * BAND_HALF
CHUNK_GROUP = 16
HEADS_PER_STEP = LANES // HEAD_DIM
MIB = 1024 * 1024

f32 = jnp.float32
bf16 = jnp.bfloat16


def _params(semantics, vmem_mib):
    return pltpu.CompilerParams(dimension_semantics=semantics, vmem_limit_bytes=vmem_mib * MIB)


def _rms(x, g):
    return x * lax.rsqrt(jnp.mean(x * x, axis=-1, keepdims=True) + RMS_EPS) * g


def _in_proj_kernel(x_ref, g_ref, w_ref, pc_ref, qkv_ref):
    h = _rms(x_ref[...], g_ref[...]).astype(bf16)
    p = jnp.dot(h, w_ref[...], preferred_element_type=f32)
    pc_ref[...] = p[:, :3 * CONV_WIDTH]
    qkv_ref[...] = p[:, 3 * CONV_WIDTH:]


def _in_proj(x2, g_mix, w_in_bf):
    n = TOKENS // ROW_TILE
    return pl.pallas_call(
        _in_proj_kernel,
        grid=(n,),
        in_specs=[
            pl.BlockSpec((ROW_TILE, D_MODEL), lambda i: (i, 0)),
            pl.BlockSpec((1, D_MODEL), lambda i: (0, 0)),
            pl.BlockSpec((D_MODEL, PROJ_WIDTH), lambda i: (0, 0)),
        ],
        out_specs=[
            pl.BlockSpec((ROW_TILE, 3 * CONV_WIDTH), lambda i: (i, 0)),
            pl.BlockSpec((ROW_TILE, 3 * ATTN_WIDTH), lambda i: (i, 0)),
        ],
        out_shape=[
            jax.ShapeDtypeStruct((TOKENS, 3 * CONV_WIDTH), f32),
            jax.ShapeDtypeStruct((TOKENS, 3 * ATTN_WIDTH), f32),
        ],
        compiler_params=_params(("arbitrary",), 48),
        name="in_proj",
    )(x2, g_mix, w_in_bf)


CONV_PAD = 8
CONV_ROWS = 512


def _conv_kernel(pc_ref, w_ref, g_ref, o_ref, u_ref):
    zeros = jnp.zeros((CONV_PAD, CONV_WIDTH), f32)
    u_ref[pl.ds(0, CONV_PAD), :] = zeros
    u_ref[pl.ds(CONV_PAD + SEQ, CONV_PAD), :] = zeros
    for i in range(SEQ // CONV_ROWS):
        rows = pl.ds(i * CONV_ROWS, CONV_ROWS)
        u_ref[pl.ds(CONV_PAD + i * CONV_ROWS, CONV_ROWS), :] = (
            pc_ref[rows, CONV_WIDTH:2 * CONV_WIDTH] * pc_ref[rows, 2 * CONV_WIDTH:])
    w = w_ref[...]
    g = g_ref[...]
    for i in range(SEQ // CONV_ROWS):
        base = CONV_PAD + i * CONV_ROWS
        prev = u_ref[pl.ds(base - 1, CONV_ROWS), :]
        cur = u_ref[pl.ds(base, CONV_ROWS), :]
        nxt = u_ref[pl.ds(base + 1, CONV_ROWS), :]
        y = pc_ref[pl.ds(i * CONV_ROWS, CONV_ROWS), :CONV_WIDTH] * (
            w[0:1] * prev + w[1:2] * cur + w[2:3] * nxt)
        o_ref[pl.ds(i * CONV_ROWS, CONV_ROWS), :] = _rms(y, g).astype(bf16)


def _conv(pc, conv_w, g_conv):
    return pl.pallas_call(
        _conv_kernel,
        grid=(BATCH,),
        in_specs=[
            pl.BlockSpec((SEQ, 3 * CONV_WIDTH), lambda b: (b, 0)),
            pl.BlockSpec((3, CONV_WIDTH), lambda b: (0, 0)),
            pl.BlockSpec((1, CONV_WIDTH), lambda b: (0, 0)),
        ],
        out_specs=pl.BlockSpec((SEQ, CONV_WIDTH), lambda b: (b, 0)),
        out_shape=jax.ShapeDtypeStruct((TOKENS, CONV_WIDTH), bf16),
        scratch_shapes=[pltpu.VMEM((SEQ + 2 * CONV_PAD, CONV_WIDTH), f32)],
        compiler_params=_params(("arbitrary",), 48),
        name="conv",
    )(pc, conv_w, g_conv)


ROPE_ROWS = 256
CAP_INTERIOR, CAP_FIRST, CAP_LAST, CAP_SINGLE = range(4)
MASK_OPEN = float(jnp.finfo(jnp.float32).max)


def _dot_lhs_t(a, b):
    return lax.dot_general(a, b, (((0,), (0,)), ((), ())),
                           precision=lax.Precision.HIGHEST, preferred_element_type=f32)


def _attn_kernel(pos_ref, invf_ref, q_ref, k_ref, v_ref, o_ref,
                 cos_ref, s1_ref, s2_ref, cap_ref, qs_ref, ks_ref,
                 qlo_ref, qhi_ref, kpad_ref, vlo_ref, vhi_ref,
                 num_ref, m_ref, l_ref):
    hp = pl.program_id(1)

    @pl.when(hp == 0)
    def _():
        ang = invf_ref[...] * pos_ref[...].astype(f32)
        cs, sn = jnp.cos(ang), jnp.sin(ang)
        lane = lax.broadcasted_iota(jnp.int32, (ROT_HALF, LANES), 1) % HEAD_DIM
        freq = lax.broadcasted_iota(jnp.int32, (ROT_HALF, LANES), 0)
        place_lo = jnp.where(lane == freq, 1.0, 0.0)
        place_hi = jnp.where(lane == freq + ROT_HALF, 1.0, 0.0)
        unrotated = jnp.where(
            lax.broadcasted_iota(jnp.int32, (1, LANES), 1) % HEAD_DIM >= ROT_DIM, 1.0, 0.0)
        cos_ref[...] = _dot_lhs_t(cs, place_lo + place_hi) + unrotated
        s1_ref[...] = _dot_lhs_t(sn, place_hi)
        s2_ref[...] = -_dot_lhs_t(sn, place_lo)
        row = lax.broadcasted_iota(jnp.int32, (2 * Q_CHUNK, K_WIN), 0) % Q_CHUNK
        col = lax.broadcasted_iota(jnp.int32, (2 * Q_CHUNK, K_WIN), 1)
        band = (col - row >= 0) & (col - row <= 2 * BAND_HALF)
        not_before = col >= BAND_HALF
        not_after = col < K_WIN - BAND_HALF
        for idx, ok in ((CAP_INTERIOR, band), (CAP_FIRST, band & not_before),
                        (CAP_LAST, band & not_after), (CAP_SINGLE, band & not_before & not_after)):
            cap_ref[idx] = jnp.where(ok, MASK_OPEN, NEG_INF)

    def rope_body(i, c):
        rows = pl.ds(pl.multiple_of(i * ROPE_ROWS, ROPE_ROWS), ROPE_ROWS)
        cs, s1, s2 = cos_ref[rows, :], s1_ref[rows, :], s2_ref[rows, :]
        for src, dst, scale in ((q_ref, qs_ref, HEAD_DIM ** -0.5), (k_ref, ks_ref, None)):
            x = src[rows, :]
            y = (x * cs + pltpu.roll(x, ROT_HALF, 1) * s1
                 + pltpu.roll(x, LANES - ROT_HALF, 1) * s2)
            dst[rows, :] = y if scale is None else y * scale
        return c
    lax.fori_loop(0, SEQ // ROPE_ROWS, rope_body, 0)

    lane_lo = lax.broadcasted_iota(jnp.int32, (Q_CHUNK, LANES), 1) < HEAD_DIM
    zero_pad = jnp.zeros((BAND_HALF, LANES), bf16)

    def zero_rows(start):
        for ref in (kpad_ref, vlo_ref, vhi_ref):
            ref[pl.ds(start, BAND_HALF), :] = zero_pad

    def stage(src, q_row, pad_row):
        qf, kf, vf = qs_ref[src, :], ks_ref[src, :], v_ref[src, :]
        q_dst, p_dst = pl.ds(q_row, Q_CHUNK), pl.ds(pad_row, Q_CHUNK)
        qlo_ref[q_dst, :] = jnp.where(lane_lo, qf, 0.0).astype(bf16)
        qhi_ref[q_dst, :] = jnp.where(lane_lo, 0.0, qf).astype(bf16)
        kpad_ref[p_dst, :] = kf.astype(bf16)
        vlo_ref[p_dst, :] = jnp.where(lane_lo, vf, 0.0).astype(bf16)
        vhi_ref[p_dst, :] = jnp.where(lane_lo, 0.0, vf).astype(bf16)

    def chunk(pat, out_rows, q_row, pad_row, cap):
        q2 = jnp.concatenate([qlo_ref[pl.ds(q_row, Q_CHUNK), :], qhi_ref[pl.ds(q_row, Q_CHUNK), :]],
                             axis=0)
        s = lax.dot_general(q2, kpad_ref[pl.ds(pad_row, K_WIN), :], (((1,), (1,)), ((), ())),
                            preferred_element_type=f32)
        s = jnp.minimum(s, cap_ref[cap])
        m = jnp.max(s, axis=1, keepdims=True)
        p = jnp.exp(s - m)
        l = jnp.sum(p, axis=1, keepdims=True)
        pb = p.astype(bf16)
        num = (jnp.dot(pb[:Q_CHUNK], vlo_ref[pl.ds(pad_row, K_WIN), :], preferred_element_type=f32)
               + jnp.dot(pb[Q_CHUNK:], vhi_ref[pl.ds(pad_row, K_WIN), :], preferred_element_type=f32))
        num_ref[pat, out_rows, :] = num
        m_ref[pat, out_rows, :] = jnp.where(lane_lo, m[:Q_CHUNK], m[Q_CHUNK:])
        l_ref[pat, out_rows, :] = jnp.where(lane_lo, l[:Q_CHUNK], l[Q_CHUNK:])

    for pat, (_, dil) in enumerate(DILATED_PATTERNS):
        length = SEQ // dil
        n_chunks = length // Q_CHUNK
        per_group = max(1, CHUNK_GROUP // n_chunks)
        region = length + 2 * BAND_HALF
        for u in range(per_group):
            zero_rows(u * region)
            zero_rows(u * region + BAND_HALF + length)

        def group(g, c, pat=pat, dil=dil, length=length, n_chunks=n_chunks,
                  per_group=per_group, region=region):
            def rows_of(r, i):
                if dil == 1:
                    return pl.ds(i * Q_CHUNK, Q_CHUNK)
                return pl.ds(r + dil * Q_CHUNK * i, Q_CHUNK, stride=dil)

            for u in range(per_group):
                r = g * per_group + u
                for i in range(n_chunks):
                    stage(rows_of(r, i), u * length + i * Q_CHUNK,
                          u * region + BAND_HALF + i * Q_CHUNK)
            for u in range(per_group):
                r = g * per_group + u
                for j in range(n_chunks):
                    if n_chunks == 1:
                        cap = CAP_SINGLE
                    else:
                        cap = CAP_FIRST if j == 0 else CAP_LAST if j == n_chunks - 1 else CAP_INTERIOR
                    chunk(pat, rows_of(r, j), u * length + j * Q_CHUNK, u * region + j * Q_CHUNK, cap)
            return c

        if dil == per_group:
            group(0, 0)
        else:
            lax.fori_loop(0, dil // per_group, group, 0)

    def mix(i, c):
        rows = pl.ds(pl.multiple_of(i * ROPE_ROWS, ROPE_ROWS), ROPE_ROWS)
        ms = [m_ref[p, rows, :] for p in range(N_PATTERNS)]
        m_all = functools.reduce(jnp.maximum, ms)
        num = jnp.zeros((ROPE_ROWS, LANES), f32)
        den = jnp.zeros((ROPE_ROWS, LANES), f32)
        for p in range(N_PATTERNS):
            a = jnp.exp(ms[p] - m_all)
            den = den + a * l_ref[p, rows, :]
            num = num + a * num_ref[p, rows, :]
        o_ref[rows, :] = num / den
        return c
    lax.fori_loop(0, SEQ // ROPE_ROWS, mix, 0)


def _attention(qkv3, pos_row, invf_col):
    n_hp = N_HEADS // HEADS_PER_STEP
    blk = lambda off: pl.BlockSpec((None, SEQ, LANES), lambda b, h, off=off: (b, 0, off + h))
    pad_rows = max(max(1, CHUNK_GROUP // (SEQ // d // Q_CHUNK)) * (SEQ // d + 2 * BAND_HALF)
                   for _, d in DILATED_PATTERNS)
    table =pltpu.VMEM((SEQ, LANES), f32)
    stats = pltpu.VMEM((N_PATTERNS, SEQ, LANES), f32)
    return pl.pallas_call(
        _attn_kernel,
        grid=(BATCH, n_hp),
        in_specs=[
            pl.BlockSpec((None, 1, SEQ), lambda b, h: (b, 0, 0)),
            pl.BlockSpec((ROT_HALF, 1), lambda b, h: (0, 0)),
            blk(0), blk(n_hp), blk(2 * n_hp),
        ],
        out_specs=pl.BlockSpec((None, SEQ, LANES), lambda b, h: (b, 0, h)),
        out_shape=jax.ShapeDtypeStruct((BATCH, SEQ, ATTN_WIDTH), f32),
        scratch_shapes=[
            table, table, table,
            pltpu.VMEM((4, 2 * Q_CHUNK, K_WIN), f32),
            table, table,
            pltpu.VMEM((SEQ, LANES), bf16), pltpu.VMEM((SEQ, LANES), bf16),
            pltpu.VMEM((pad_rows, LANES), bf16), pltpu.VMEM((pad_rows, LANES), bf16),
            pltpu.VMEM((pad_rows, LANES), bf16),
            stats, stats, stats,
        ],
        compiler_params=_params(("arbitrary", "arbitrary"), 48),
        name="attention",
    )(pos_row, invf_col, qkv3, qkv3, qkv3)


def _out_proj_kernel(yc_ref, ya_ref, x_ref, w_ref, ga_ref, gf_ref, wr_ref,
                     x1_ref, h2_ref, lg_ref):
    ya = _rms(ya_ref[...], ga_ref[...]).astype(bf16)
    mix = (jnp.dot(yc_ref[...], w_ref[pl.ds(0, CONV_WIDTH), :], preferred_element_type=f32)
           + jnp.dot(ya, w_ref[pl.ds(CONV_WIDTH, ATTN_WIDTH), :], preferred_element_type=f32))
    x1 = x_ref[...] + mix
    x1_ref[...] = x1
    h2 = _rms(x1, gf_ref[...])
    h2_ref[...] = h2.astype(bf16)
    lg_ref[...] = lax.dot_general(wr_ref[...], h2, (((1,), (1,)), ((), ())),
                                  precision=lax.Precision.HIGHEST, preferred_element_type=f32)


def _out_proj(yc, ya, x2, w_out_bf, g_attn, g_ffn, w_router_t):
    n = TOKENS // ROW_TILE
    return pl.pallas_call(
        _out_proj_kernel,
        grid=(n,),
        in_specs=[
            pl.BlockSpec((ROW_TILE, CONV_WIDTH), lambda i: (i, 0)),
            pl.BlockSpec((ROW_TILE, ATTN_WIDTH), lambda i: (i, 0)),
            pl.BlockSpec((ROW_TILE, D_MODEL), lambda i: (i, 0)),
            pl.BlockSpec((D_MODEL, D_MODEL), lambda i: (0, 0)),
            pl.BlockSpec((1, ATTN_WIDTH), lambda i: (0, 0)),
            pl.BlockSpec((1, D_MODEL), lambda i: (0, 0)),
            pl.BlockSpec((N_EXPERTS, D_MODEL), lambda i: (0, 0)),
        ],
        out_specs=[
            pl.BlockSpec((ROW_TILE, D_MODEL), lambda i: (i, 0)),
            pl.BlockSpec((ROW_TILE, D_MODEL), lambda i: (i, 0)),
            pl.BlockSpec((N_EXPERTS, ROW_TILE), lambda i: (0, i)),
        ],
        out_shape=[
            jax.ShapeDtypeStruct((TOKENS, D_MODEL), f32),
            jax.ShapeDtypeStruct((TOKENS, D_MODEL), bf16),
            jax.ShapeDtypeStruct((N_EXPERTS, TOKENS), f32),
        ],
        compiler_params=_params(("arbitrary",), 48),
        name="out_proj",
    )(yc, ya, x2, w_out_bf, g_attn, g_ffn, w_router_t)


N_LANE_BLOCKS = SEQ // LANES


def _excl_cumsum_tokens(x, tri):
    stacked = jnp.concatenate([x[:, j * LANES:(j + 1) * LANES] for j in range(N_LANE_BLOCKS)], axis=0)
    within = jnp.dot(stacked.astype(bf16), tri, preferred_element_type=f32)
    totals = jnp.sum(stacked, axis=1, keepdims=True)
    out, offset = [], jnp.zeros((N_EXPERTS, 1), f32)
    for j in range(N_LANE_BLOCKS):
        rows = slice(j * N_EXPERTS, (j + 1) * N_EXPERTS)
        out.append(within[rows] + offset)
        offset = offset + totals[rows]
    return jnp.concatenate(out, axis=1)


def _route_kernel(lg_ref, rank_ref, aff_ref):
    x = lg_ref[...]
    e = jnp.exp(x - jnp.max(x, axis=0, keepdims=True))
    aff = e / jnp.sum(e, axis=0, keepdims=True)
    aff_ref[...] = aff
    thr = jnp.zeros((N_EXPERTS, 1), jnp.int32)
    for bit in range(30, -1, -1):
        cand = thr | (1 << bit)
        hit = aff >= pltpu.bitcast(cand, f32)
        cnt = jnp.sum(jnp.where(hit, 1.0, 0.0), axis=1, keepdims=True)
        thr = jnp.where(cnt >= float(CAPACITY), cand, thr)
    above = aff >= pltpu.bitcast(thr + 1, f32)
    tie = (aff >= pltpu.bitcast(thr, f32)) & jnp.logical_not(above)
    need = float(CAPACITY) - jnp.sum(jnp.where(above, 1.0, 0.0), axis=1, keepdims=True)
    tri = jnp.where(lax.broadcasted_iota(jnp.int32, (LANES, LANES), 0)
                    < lax.broadcasted_iota(jnp.int32, (LANES, LANES), 1), 1.0, 0.0).astype(bf16)
    tie_rank = _excl_cumsum_tokens(jnp.where(tie, 1.0, 0.0), tri)
    sel = above | (tie & (tie_rank < need))
    rank = _excl_cumsum_tokens(jnp.where(sel, 1.0, 0.0), tri)
    rank_ref[...] = jnp.where(sel, rank, -1.0)


def _route(logits_t):
    blk = pl.BlockSpec((None, N_EXPERTS, SEQ), lambda b: (b, 0, 0))
    return pl.pallas_call(
        _route_kernel,
        grid=(BATCH,),
        in_specs=[pl.BlockSpec((N_EXPERTS, SEQ), lambda b: (0, b))],
        out_specs=[blk, blk],
        out_shape=[jax.ShapeDtypeStruct((BATCH, N_EXPERTS, SEQ), f32)] * 2,
        compiler_params=_params(("arbitrary",), 32),
        name="route",
    )(logits_t)


def _gather_kernel(rank_ref, h2_ref, xe_ref):
    e = pl.program_id(1)
    r = rank_ref[pl.ds(e, 1), :]
    slot = lax.broadcasted_iota(jnp.int32, (CAPACITY, SEQ), 0).astype(f32)
    onehot = jnp.where(r == slot, 1.0, 0.0).astype(bf16)
    xe_ref[...] = jnp.dot(onehot, h2_ref[...], preferred_element_type=f32).astype(bf16)


def _gather(rank, h2):
    return pl.pallas_call(
        _gather_kernel,
        grid=(BATCH, N_EXPERTS),
        in_specs=[
            pl.BlockSpec((None, N_EXPERTS, SEQ), lambda b, e: (b, 0, 0)),
            pl.BlockSpec((SEQ, D_MODEL), lambda b, e: (b, 0)),
        ],
        out_specs=pl.BlockSpec((None, None, CAPACITY, D_MODEL), lambda b, e: (e, b, 0, 0)),
        out_shape=jax.ShapeDtypeStruct((N_EXPERTS, BATCH, CAPACITY, D_MODEL), bf16),
        compiler_params=_params(("arbitrary", "arbitrary"), 40),
        name="gather",
    )(rank, h2)


def _ffn_kernel(xe_ref, wg_ref, wu_ref, wd_ref, y_ref, acc_ref):
    f = pl.program_id(1)
    xe = xe_ref[...]
    a = jnp.dot(xe, wg_ref[...].astype(bf16), preferred_element_type=f32)
    u = jnp.dot(xe, wu_ref[...].astype(bf16), preferred_element_type=f32)
    hidden = (a * (1.0 / (1.0 + jnp.exp(-a))) * u).astype(bf16)
    part = jnp.dot(hidden, wd_ref[...].astype(bf16), preferred_element_type=f32)

    @pl.when(f == 0)
    def _():
        acc_ref[...] = part

    @pl.when(f != 0)
    def _():
        acc_ref[...] += part

    @pl.when(f == D_FF // FF_TILE - 1)
    def _():
        y_ref[...] = acc_ref[...].astype(bf16)


def _ffn(xe, w_gate, w_up, w_down):
    rows = BATCH * CAPACITY
    return pl.pallas_call(
        _ffn_kernel,
        grid=(N_EXPERTS, D_FF // FF_TILE),
        in_specs=[
            pl.BlockSpec((None, rows, D_MODEL), lambda e, f: (e, 0, 0)),
            pl.BlockSpec((None, D_MODEL, FF_TILE), lambda e, f: (e, 0, f)),
            pl.BlockSpec((None, D_MODEL, FF_TILE), lambda e, f: (e, 0, f)),
            pl.BlockSpec((None, FF_TILE, D_MODEL), lambda e, f: (e, f, 0)),
        ],
        out_specs=pl.BlockSpec((None, rows, D_MODEL), lambda e, f: (e, 0, 0)),
        out_shape=jax.ShapeDtypeStruct((N_EXPERTS, rows, D_MODEL), bf16),
        scratch_shapes=[pltpu.VMEM((rows, D_MODEL), f32)],
        compiler_params=_params(("arbitrary", "arbitrary"), 56),
        name="ffn",
    )(xe, w_gate, w_up, w_down)


def _combine_kernel(rank_ref, aff_ref, y_ref, x1_ref, g_ref, o_ref, acc_ref):
    e = pl.program_id(1)

    @pl.when(e == 0)
    def _():
        acc_ref[...] = x1_ref[...]

    r = rank_ref[pl.ds(e, 1), :]
    gate = aff_ref[pl.ds(e, 1), :]
    slot = lax.broadcasted_iota(jnp.int32, (CAPACITY, SEQ), 0).astype(f32)
    weights = jnp.where(r == slot, gate, 0.0).astype(bf16)
    acc_ref[...] += lax.dot_general(weights, y_ref[...], (((0,), (0,)), ((), ())),
                                    preferred_element_type=f32)

    @pl.when(e == N_EXPERTS - 1)
    def _():
        o_ref[...] = _rms(acc_ref[...], g_ref[...])


def _combine(rank, aff, y, x1, g_final):
    route_blk = pl.BlockSpec((None, N_EXPERTS, SEQ), lambda b, e: (b, 0, 0))
    return pl.pallas_call(
        _combine_kernel,
        grid=(BATCH, N_EXPERTS),
        in_specs=[
            route_blk, route_blk,
            pl.BlockSpec((None, None, CAPACITY, D_MODEL), lambda b, e: (e, b, 0, 0)),
            pl.BlockSpec((SEQ, D_MODEL), lambda b, e: (b, 0)),
            pl.BlockSpec((1, D_MODEL), lambda b, e: (0, 0)),
        ],
        out_specs=pl.BlockSpec((SEQ, D_MODEL), lambda b, e: (b, 0)),
        out_shape=jax.ShapeDtypeStruct((TOKENS, D_MODEL), f32),
        scratch_shapes=[pltpu.VMEM((SEQ, D_MODEL), f32)],
        compiler_params=_params(("arbitrary", "arbitrary"), 56),
        name="combine",
    )(rank, aff, y, x1, g_final)


def _rope_freqs():
    inv_freq = ROPE_THETA ** (-jnp.arange(ROT_HALF, dtype=f32) / ROT_HALF)
    return inv_freq.reshape(ROT_HALF, 1)


def kernel(x, positions, g_mix, w_in, conv_w, g_conv_out, g_attn_out, w_out, g_ffn, w_router,
           w_gate, w_up, w_down, g_final):
    x2 = x.reshape(TOKENS, D_MODEL)
    pc, qkv = _in_proj(x2, g_mix[0].reshape(1, D_MODEL), w_in[0].astype(bf16))
    yc = _conv(pc, conv_w[0], g_conv_out[0].reshape(1, CONV_WIDTH))
    ya = _attention(qkv.reshape(BATCH, SEQ, 3 * ATTN_WIDTH), positions.reshape(BATCH, 1, SEQ),
                    _rope_freqs())
    x1, h2, logits_t = _out_proj(yc, ya.reshape(TOKENS, ATTN_WIDTH), x2, w_out[0].astype(bf16),
                                 g_attn_out[0].reshape(1, ATTN_WIDTH), g_ffn[0].reshape(1, D_MODEL),
                                 w_router[0].T)
    rank, aff = _route(logits_t)
    xe = _gather(rank, h2)
    y = _ffn(xe.reshape(N_EXPERTS, BATCH * CAPACITY, D_MODEL), w_gate[0], w_up[0], w_down[0])
    out = _combine(rank, aff, y.reshape(N_EXPERTS, BATCH, CAPACITY, D_MODEL), x1,
                   g_final.reshape(1, D_MODEL))
    return out.reshape(BATCH, SEQ, D_MODEL)
```

```python
import functools

import jax
import jax.numpy as jnp
from jax import lax
from jax.experimental import pallas as pl
from jax.experimental.pallas import tpu as pltpu

D_MODEL = 1024
BATCH = 8
SEQ = 2048
TOKENS = BATCH * SEQ
CONV_WIDTH = 512
ATTN_WIDTH = 512
HEAD_DIM = 64
N_HEADS = 8
DILATED_PATTERNS = ((128, 1), (512, 4), (2048, 16))
N_PATTERNS = len(DILATED_PATTERNS)
ROPE_THETA = 500000.0
ROT_DIM = 16
ROT_HALF = ROT_DIM // 2
N_EXPERTS = 16
CAPACITY = 2 * SEQ // N_EXPERTS
D_FF = 2 * D_MODEL
PROJ_WIDTH = 3 * CONV_WIDTH + 3 * ATTN_WIDTH
RMS_EPS = 1e-6
NEG_INF = -1e30

LANES = 128
ROW_TILE = 512
FF_TILE = 512
BAND_HALF = 64
Q_CHUNK = 2 * BAND_HALF
K_WIN = 4 * BAND_HALF
CHUNK_GROUP = 16
HEADS_PER_STEP = LANES // HEAD_DIM
SUBLANES = 8
SLAB_ROWS = D_MODEL // LANES
TILE_PITCH = CAPACITY + SUBLANES
SLOT_SPLIT = 16
EXPERTS_PER_STEP = 4
MIB = 1024 * 1024

f32 = jnp.float32
bf16 = jnp.bfloat16


def _params(semantics, vmem_mib):
    return pltpu.CompilerParams(dimension_semantics=semantics, vmem_limit_bytes=vmem_mib * MIB)


def _rms(x, g):
    return x * lax.rsqrt(jnp.mean(x * x, axis=-1, keepdims=True) + RMS_EPS) * g


def _in_proj_kernel(x_ref, g_ref, w_ref, pc_ref, qkv_ref):
    h = _rms(x_ref[...], g_ref[...]).astype(bf16)
    p = jnp.dot(h, w_ref[...], preferred_element_type=f32)
    pc_ref[...] = p[:, :3 * CONV_WIDTH]
    qkv_ref[...] = p[:, 3 * CONV_WIDTH:]


def _in_proj(x2, g_mix, w_in_bf):
    n = TOKENS // ROW_TILE
    return pl.pallas_call(
        _in_proj_kernel,
        grid=(n,),
        in_specs=[
            pl.BlockSpec((ROW_TILE, D_MODEL), lambda i: (i, 0)),
            pl.BlockSpec((1, D_MODEL), lambda i: (0, 0)),
            pl.BlockSpec((D_MODEL, PROJ_WIDTH), lambda i: (0, 0)),
        ],
        out_specs=[
            pl.BlockSpec((ROW_TILE, 3 * CONV_WIDTH), lambda i: (i, 0)),
            pl.BlockSpec((ROW_TILE, 3 * ATTN_WIDTH), lambda i: (i, 0)),
        ],
        out_shape=[
            jax.ShapeDtypeStruct((TOKENS, 3 * CONV_WIDTH), f32),
            jax.ShapeDtypeStruct((TOKENS, 3 * ATTN_WIDTH), f32),
        ],
        compiler_params=_params(("arbitrary",), 48),
        name="in_proj",
    )(x2, g_mix, w_in_bf)


CONV_PAD = 8
CONV_ROWS = 512


def _conv_kernel(pc_ref, w_ref, g_ref, o_ref, u_ref):
    zeros = jnp.zeros((CONV_PAD, CONV_WIDTH), f32)
    u_ref[pl.ds(0, CONV_PAD), :] = zeros
    u_ref[pl.ds(CONV_PAD + SEQ, CONV_PAD), :] = zeros
    for i in range(SEQ // CONV_ROWS):
        rows = pl.ds(i * CONV_ROWS, CONV_ROWS)
        u_ref[pl.ds(CONV_PAD + i * CONV_ROWS, CONV_ROWS), :] = (
            pc_ref[rows, CONV_WIDTH:2 * CONV_WIDTH] * pc_ref[rows, 2 * CONV_WIDTH:])
    w = w_ref[...]
    g = g_ref[...]
    for i in range(SEQ // CONV_ROWS):
        base = CONV_PAD + i * CONV_ROWS
        prev = u_ref[pl.ds(base - 1, CONV_ROWS), :]
        cur = u_ref[pl.ds(base, CONV_ROWS), :]
        nxt = u_ref[pl.ds(base + 1, CONV_ROWS), :]
        y = pc_ref[pl.ds(i * CONV_ROWS, CONV_ROWS), :CONV_WIDTH] * (
            w[0:1] * prev + w[1:2] * cur + w[2:3] * nxt)
        o_ref[pl.ds(i * CONV_ROWS, CONV_ROWS), :] = _rms(y, g).astype(bf16)


def _conv(pc, conv_w, g_conv):
    return pl.pallas_call(
        _conv_kernel,
        grid=(BATCH,),
        in_specs=[
            pl.BlockSpec((SEQ, 3 * CONV_WIDTH), lambda b: (b, 0)),
            pl.BlockSpec((3, CONV_WIDTH), lambda b: (0, 0)),
            pl.BlockSpec((1, CONV_WIDTH), lambda b: (0, 0)),
        ],
        out_specs=pl.BlockSpec((SEQ, CONV_WIDTH), lambda b: (b, 0)),
        out_shape=jax.ShapeDtypeStruct((TOKENS, CONV_WIDTH), bf16),
        scratch_shapes=[pltpu.VMEM((SEQ + 2 * CONV_PAD, CONV_WIDTH), f32)],
        compiler_params=_params(("arbitrary",), 48),
        name="conv",
    )(pc, conv_w, g_conv)


ROPE_ROWS = 256
CAP_INTERIOR, CAP_FIRST, CAP_LAST, CAP_SINGLE = range(4)
MASK_OPEN = float(jnp.finfo(jnp.float32).max)


def _dot_lhs_t(a, b):
    return lax.dot_general(a, b, (((0,), (0,)), ((), ())),
                           precision=lax.Precision.HIGHEST, preferred_element_type=f32)


def _attn_kernel(pos_ref, invf_ref, q_ref, k_ref, v_ref, o_ref,
                 cos_ref, s1_ref, s2_ref, cap_ref, qs_ref, ks_ref,
                 qlo_ref, qhi_ref, kpad_ref, vlo_ref, vhi_ref,
                 num_ref, m_ref, l_ref):
    hp = pl.program_id(1)

    @pl.when(hp == 0)
    def _():
        ang = invf_ref[...] * pos_ref[...].astype(f32)
        cs, sn = jnp.cos(ang), jnp.sin(ang)
        lane = lax.broadcasted_iota(jnp.int32, (ROT_HALF, LANES), 1) % HEAD_DIM
        freq = lax.broadcasted_iota(jnp.int32, (ROT_HALF, LANES), 0)
        place_lo = jnp.where(lane == freq, 1.0, 0.0)
        place_hi = jnp.where(lane == freq + ROT_HALF, 1.0, 0.0)
        unrotated = jnp.where(
            lax.broadcasted_iota(jnp.int32, (1, LANES), 1) % HEAD_DIM >= ROT_DIM, 1.0, 0.0)
        cos_ref[...] = _dot_lhs_t(cs, place_lo + place_hi) + unrotated
        s1_ref[...] = _dot_lhs_t(sn, place_hi)
        s2_ref[...] = -_dot_lhs_t(sn, place_lo)
        row = lax.broadcasted_iota(jnp.int32, (2 * Q_CHUNK, K_WIN), 0) % Q_CHUNK
        col = lax.broadcasted_iota(jnp.int32, (2 * Q_CHUNK, K_WIN), 1)
        band = (col - row >= 0) & (col - row <= 2 * BAND_HALF)
        not_before = col >= BAND_HALF
        not_after = col < K_WIN - BAND_HALF
        for idx, ok in ((CAP_INTERIOR, band), (CAP_FIRST, band & not_before),
                        (CAP_LAST, band & not_after), (CAP_SINGLE, band & not_before & not_after)):
            cap_ref[idx] = jnp.where(ok, MASK_OPEN, NEG_INF)

    def rope_body(i, c):
        rows = pl.ds(pl.multiple_of(i * ROPE_ROWS, ROPE_ROWS), ROPE_ROWS)
        cs, s1, s2 = cos_ref[rows, :], s1_ref[rows, :], s2_ref[rows, :]
        for src, dst, scale in ((q_ref, qs_ref, HEAD_DIM ** -0.5), (k_ref, ks_ref, None)):
            x = src[rows, :]
            y = (x * cs + pltpu.roll(x, ROT_HALF, 1) * s1
                 + pltpu.roll(x, LANES - ROT_HALF, 1) * s2)
            dst[rows, :] = y if scale is None else y * scale
        return c
    lax.fori_loop(0, SEQ // ROPE_ROWS, rope_body, 0)

    lane_lo = lax.broadcasted_iota(jnp.int32, (Q_CHUNK, LANES), 1) < HEAD_DIM
    zero_pad = jnp.zeros((BAND_HALF, LANES), bf16)

    def zero_rows(start):
        for ref in (kpad_ref, vlo_ref, vhi_ref):
            ref[pl.ds(start, BAND_HALF), :] = zero_pad

    def stage(src, q_row, pad_row):
        qf, kf, vf = qs_ref[src, :], ks_ref[src, :], v_ref[src, :]
        q_dst, p_dst = pl.ds(q_row, Q_CHUNK), pl.ds(pad_row, Q_CHUNK)
        qlo_ref[q_dst, :] = jnp.where(lane_lo, qf, 0.0).astype(bf16)
        qhi_ref[q_dst, :] = jnp.where(lane_lo, 0.0, qf).astype(bf16)
        kpad_ref[p_dst, :] = kf.astype(bf16)
        vlo_ref[p_dst, :] = jnp.where(lane_lo, vf, 0.0).astype(bf16)
        vhi_ref[p_dst, :] = jnp.where(lane_lo, 0.0, vf).astype(bf16)

    def chunk(pat, out_rows, q_row, pad_row, cap):
        q2 = jnp.concatenate([qlo_ref[pl.ds(q_row, Q_CHUNK), :], qhi_ref[pl.ds(q_row, Q_CHUNK), :]],
                             axis=0)
        s = lax.dot_general(q2, kpad_ref[pl.ds(pad_row, K_WIN), :], (((1,), (1,)), ((), ())),
                            preferred_element_type=f32)
        s = jnp.minimum(s, cap_ref[cap])
        m = jnp.max(s, axis=1, keepdims=True)
        p = jnp.exp(s - m)
        l = jnp.sum(p, axis=1, keepdims=True)
        pb = p.astype(bf16)
        num = (jnp.dot(pb[:Q_CHUNK], vlo_ref[pl.ds(pad_row, K_WIN), :], preferred_element_type=f32)
               + jnp.dot(pb[Q_CHUNK:], vhi_ref[pl.ds(pad_row, K_WIN), :], preferred_element_type=f32))
        num_ref[pat, out_rows, :] = num
        m_ref[pat, out_rows, :] = jnp.where(lane_lo, m[:Q_CHUNK], m[Q_CHUNK:])
        l_ref[pat, out_rows, :] = jnp.where(lane_lo, l[:Q_CHUNK], l[Q_CHUNK:])

    for pat, (_, dil) in enumerate(DILATED_PATTERNS):
        length = SEQ // dil
        n_chunks = length // Q_CHUNK
        per_group = max(1, CHUNK_GROUP // n_chunks)
        region = length + 2 * BAND_HALF
        for u in range(per_group):
            zero_rows(u * region)
            zero_rows(u * region + BAND_HALF + length)

        def group(g, c, pat=pat, dil=dil, length=length, n_chunks=n_chunks,
                  per_group=per_group, region=region):
            def rows_of(r, i):
                if dil == 1:
                    return pl.ds(i * Q_CHUNK, Q_CHUNK)
                return pl.ds(r + dil * Q_CHUNK * i, Q_CHUNK, stride=dil)

            for u in range(per_group):
                r = g * per_group + u
                for i in range(n_chunks):
                    stage(rows_of(r, i), u * length + i * Q_CHUNK,
                          u * region + BAND_HALF + i * Q_CHUNK)
            for u in range(per_group):
                r = g * per_group + u
                for j in range(n_chunks):
                    if n_chunks == 1:
                        cap = CAP_SINGLE
                    else:
                        cap = CAP_FIRST if j == 0 else CAP_LAST if j == n_chunks - 1 else CAP_INTERIOR
                    chunk(pat, rows_of(r, j), u * length + j * Q_CHUNK, u * region + j * Q_CHUNK, cap)
            return c

        if dil == per_group:
            group(0, 0)
        else:
            lax.fori_loop(0, dil // per_group, group, 0)

    def mix(i, c):
        rows = pl.ds(pl.multiple_of(i * ROPE_ROWS, ROPE_ROWS), ROPE_ROWS)
        ms = [m_ref[p, rows, :] for p in range(N_PATTERNS)]
        m_all = functools.reduce(jnp.maximum, ms)
        num = jnp.zeros((ROPE_ROWS, LANES), f32)
        den = jnp.zeros((ROPE_ROWS, LANES), f32)
        for p in range(N_PATTERNS):
            a = jnp.exp(ms[p] - m_all)
            den = den + a * l_ref[p, rows, :]
            num = num + a * num_ref[p, rows, :]
        o_ref[rows, :] = num / den
        return c
    lax.fori_loop(0, SEQ // ROPE_ROWS, mix, 0)


def _attention(qkv3, pos_row, invf_col):
    n_hp = N_HEADS // HEADS_PER_STEP
    blk = lambda off: pl.BlockSpec((None, SEQ, LANES), lambda b, h, off=off: (b, 0, off + h))
    pad_rows = max(max(1, CHUNK_GROUP // (SEQ // d // Q_CHUNK)) * (SEQ // d + 2 * BAND_HALF)
                   for _, d in DILATED_PATTERNS)
    table =pltpu.VMEM((SEQ, LANES), f32)
    stats = pltpu.VMEM((N_PATTERNS, SEQ, LANES), f32)
    return pl.pallas_call(
        _attn_kernel,
        grid=(BATCH, n_hp),
        in_specs=[
            pl.BlockSpec((None, 1, SEQ), lambda b, h: (b, 0, 0)),
            pl.BlockSpec((ROT_HALF, 1), lambda b, h: (0, 0)),
            blk(0), blk(n_hp), blk(2 * n_hp),
        ],
        out_specs=pl.BlockSpec((None, SEQ, LANES), lambda b, h: (b, 0, h)),
        out_shape=jax.ShapeDtypeStruct((BATCH, SEQ, ATTN_WIDTH), f32),
        scratch_shapes=[
            table, table, table,
            pltpu.VMEM((4, 2 * Q_CHUNK, K_WIN), f32),
            table, table,
            pltpu.VMEM((SEQ, LANES), bf16), pltpu.VMEM((SEQ, LANES), bf16),
            pltpu.VMEM((pad_rows, LANES), bf16), pltpu.VMEM((pad_rows, LANES), bf16),
            pltpu.VMEM((pad_rows, LANES), bf16),
            stats, stats, stats,
        ],
        compiler_params=_params(("arbitrary", "arbitrary"), 48),
        name="attention",
    )(pos_row, invf_col, qkv3, qkv3, qkv3)


def _store_token_major(ref, x):
    for j in range(SLAB_ROWS):
        ref[pl.ds(j, x.shape[0], stride=SLAB_ROWS), :] = x[:, j * LANES:(j + 1) * LANES]


def _dot_nt(a, b):
    return lax.dot_general(a, b, (((1,), (1,)), ((), ())), preferred_element_type=f32)


def _out_proj_kernel(yc_ref, ya_ref, x_ref, w_ref, ga_ref, gf_ref, wr_ref,
                     x1_ref, h2_ref, lg_ref):
    ya = _rms(ya_ref[...], ga_ref[...]).astype(bf16)
    mix = (jnp.dot(yc_ref[...], w_ref[pl.ds(0, CONV_WIDTH), :], preferred_element_type=f32)
           + jnp.dot(ya, w_ref[pl.ds(CONV_WIDTH, ATTN_WIDTH), :], preferred_element_type=f32))
    x1 = x_ref[...] + mix
    _store_token_major(x1_ref, x1)
    h2 = _rms(x1, gf_ref[...])
    _store_token_major(h2_ref, h2)
    wr = wr_ref[...]
    wr_hi = wr.astype(bf16)
    wr_lo = (wr - wr_hi.astype(f32)).astype(bf16)
    h2_hi = h2.astype(bf16)
    h2_lo = (h2 - h2_hi.astype(f32)).astype(bf16)
    both = _dot_nt(jnp.concatenate([wr_hi, wr_lo], axis=0), h2_hi)
    lg_ref[...] = both[:N_EXPERTS] + both[N_EXPERTS:] + _dot_nt(wr_hi, h2_lo)


def _out_proj(yc, ya, x2, w_out_bf, g_attn, g_ffn, w_router_t):
    n = TOKENS // ROW_TILE
    slab = pl.BlockSpec((ROW_TILE * SLAB_ROWS, LANES), lambda i: (i, 0))
    slab_shape = jax.ShapeDtypeStruct((TOKENS * SLAB_ROWS, LANES), f32)
    return pl.pallas_call(
        _out_proj_kernel,
        grid=(n,),
        in_specs=[
            pl.BlockSpec((ROW_TILE, CONV_WIDTH), lambda i: (i, 0)),
            pl.BlockSpec((ROW_TILE, ATTN_WIDTH), lambda i: (i, 0)),
            pl.BlockSpec((ROW_TILE, D_MODEL), lambda i: (i, 0)),
            pl.BlockSpec((D_MODEL, D_MODEL), lambda i: (0, 0)),
            pl.BlockSpec((1, ATTN_WIDTH), lambda i: (0, 0)),
            pl.BlockSpec((1, D_MODEL), lambda i: (0, 0)),
            pl.BlockSpec((N_EXPERTS, D_MODEL), lambda i: (0, 0)),
        ],
        out_specs=[slab, slab, pl.BlockSpec((N_EXPERTS, ROW_TILE), lambda i: (0, i))],
        out_shape=[slab_shape, slab_shape, jax.ShapeDtypeStruct((N_EXPERTS, TOKENS), f32)],
        compiler_params=_params(("arbitrary",), 48),
        name="out_proj",
    )(yc, ya, x2, w_out_bf, g_attn, g_ffn, w_router_t)


N_LANE_BLOCKS = SEQ // LANES


def _excl_cumsum_tokens(x, tri):
    stacked = jnp.concatenate([x[:, j * LANES:(j + 1) * LANES] for j in range(N_LANE_BLOCKS)], axis=0)
    within = jnp.dot(stacked.astype(bf16), tri, preferred_element_type=f32)
    totals = jnp.sum(stacked, axis=1, keepdims=True)
    out, offset = [], jnp.zeros((N_EXPERTS, 1), f32)
    for j in range(N_LANE_BLOCKS):
        rows = slice(j * N_EXPERTS, (j + 1) * N_EXPERTS)
        out.append(within[rows] + offset)
        offset = offset + totals[rows]
    return jnp.concatenate(out, axis=1)


def _route_kernel(lg_ref, idx_ref, gate_ref):
    x = lg_ref[...]
    e = jnp.exp(x - jnp.max(x, axis=0, keepdims=True))
    aff = e / jnp.sum(e, axis=0, keepdims=True)
    thr = jnp.zeros((N_EXPERTS, 1), jnp.int32)
    for bit in range(30, -1, -1):
        cand = thr | (1 << bit)
        hit = aff >= pltpu.bitcast(cand, f32)
        cnt = jnp.sum(jnp.where(hit, 1.0, 0.0), axis=1, keepdims=True)
        thr = jnp.where(cnt >= float(CAPACITY), cand, thr)
    above = aff >= pltpu.bitcast(thr + 1, f32)
    tie = (aff >= pltpu.bitcast(thr, f32)) & jnp.logical_not(above)
    need = float(CAPACITY) - jnp.sum(jnp.where(above, 1.0, 0.0), axis=1, keepdims=True)
    tri = jnp.where(lax.broadcasted_iota(jnp.int32, (LANES, LANES), 0)
                    < lax.broadcasted_iota(jnp.int32, (LANES, LANES), 1), 1.0, 0.0).astype(bf16)
    tie_rank = _excl_cumsum_tokens(jnp.where(tie, 1.0, 0.0), tri)
    sel = above | (tie & (tie_rank < need))
    rank = jnp.where(sel, _excl_cumsum_tokens(jnp.where(sel, 1.0, 0.0), tri), -1.0)

    high = jnp.floor(rank * (1.0 / SLOT_SPLIT))
    low = rank - high * SLOT_SPLIT
    digit = lax.broadcasted_iota(jnp.int32, (SLOT_SPLIT, SEQ), 0).astype(f32)
    tok = lax.broadcasted_iota(jnp.int32, (1, SEQ), 1)
    g1 = aff.astype(bf16).astype(f32)
    g2 = (aff - g1).astype(bf16).astype(f32)
    g3 = aff - g1 - g2
    per_expert = lambda v: jnp.broadcast_to(v.astype(f32), (N_EXPERTS, SEQ))
    payloads = [per_expert(tok // LANES), per_expert(tok % LANES), g1, g2, g3]
    high_rows = jnp.concatenate(
        [jnp.where(high[e:e + 1] == digit, 1.0, 0.0) for e in range(N_EXPERTS)], axis=0).astype(bf16)
    low_hit = [low[e:e + 1] == digit for e in range(N_EXPERTS)]
    n_rows = N_EXPERTS * SLOT_SPLIT
    same_expert = (lax.broadcasted_iota(jnp.int32, (n_rows, n_rows), 0) // SLOT_SPLIT
                   == lax.broadcasted_iota(jnp.int32, (n_rows, n_rows), 1) // SLOT_SPLIT)
    fold = jnp.where(lax.broadcasted_iota(jnp.int32, (n_rows, SLOT_SPLIT), 0) % SLOT_SPLIT
                     == lax.broadcasted_iota(jnp.int32, (n_rows, SLOT_SPLIT), 1), 1.0, 0.0).astype(bf16)
    folded = []
    for val in payloads:
        rows = jnp.concatenate(
            [jnp.where(low_hit[e], val[e:e + 1], 0.0) for e in range(N_EXPERTS)], axis=0).astype(bf16)
        picked = jnp.where(same_expert, _dot_nt(high_rows, rows), 0.0).astype(bf16)
        folded.append(jnp.dot(picked, fold, preferred_element_type=f32))
    idx_ref[...] = (folded[0] * float(LANES) + folded[1]).astype(jnp.int32)
    gate_ref[...] = folded[2] + folded[3] + folded[4]


def _route(logits_t):
    n_rows = N_EXPERTS * SLOT_SPLIT
    blk = pl.BlockSpec((None, n_rows, CAPACITY // SLOT_SPLIT), lambda b: (b, 0, 0))
    shape = (BATCH, n_rows, CAPACITY // SLOT_SPLIT)
    return pl.pallas_call(
        _route_kernel,
        grid=(BATCH,),
        in_specs=[pl.BlockSpec((N_EXPERTS, SEQ), lambda b: (0, b))],
        out_specs=[blk, blk],
        out_shape=[jax.ShapeDtypeStruct(shape, jnp.int32), jax.ShapeDtypeStruct(shape, f32)],
        compiler_params=_params(("arbitrary",), 40),
        name="route",
    )(logits_t)


GATHER_UNROLL = 8


def _gather_kernel(idx_ref, h2_ref, xe_ref, tile_ref):
    def expert(e, c):
        def group(g, c2):
            for u in range(GATHER_UNROLL):
                slot = g * GATHER_UNROLL + u
                tok = idx_ref[0, e * CAPACITY + slot]
                slab = h2_ref[pl.ds(pl.multiple_of(tok * SLAB_ROWS, SLAB_ROWS), SLAB_ROWS), :]
                tile_ref[pl.ds(slot, SLAB_ROWS, stride=TILE_PITCH), :] = slab
            return c2
        lax.fori_loop(0, CAPACITY // GATHER_UNROLL, group, 0)
        for j in range(SLAB_ROWS):
            xe_ref[e, :, j * LANES:(j + 1) * LANES] = (
                tile_ref[pl.ds(j * TILE_PITCH, CAPACITY), :].astype(bf16))
        return c
    lax.fori_loop(0, N_EXPERTS, expert, 0)


def _gather(idx, h2_slab):
    return pl.pallas_call(
        _gather_kernel,
        grid=(BATCH,),
        in_specs=[
            pl.BlockSpec((None, 1, N_EXPERTS * CAPACITY), lambda b: (b, 0, 0),
                         memory_space=pltpu.SMEM),
            pl.BlockSpec((SEQ * SLAB_ROWS, LANES), lambda b: (b, 0)),
        ],
        out_specs=pl.BlockSpec((N_EXPERTS, None, CAPACITY, D_MODEL), lambda b: (0, b, 0, 0)),
        out_shape=jax.ShapeDtypeStruct((N_EXPERTS, BATCH, CAPACITY, D_MODEL), bf16),
        scratch_shapes=[pltpu.VMEM((SLAB_ROWS * TILE_PITCH, LANES), f32)],
        compiler_params=_params(("arbitrary",), 48),
        name="gather",
    )(idx, h2_slab)


def _ffn_kernel(xe_ref, wg_ref, wu_ref, wd_ref, y_ref, acc_ref):
    f = pl.program_id(1)
    xe = xe_ref[...]
    a = jnp.dot(xe, wg_ref[...].astype(bf16), preferred_element_type=f32)
    u = jnp.dot(xe, wu_ref[...].astype(bf16), preferred_element_type=f32)
    hidden = (a * (1.0 / (1.0 + jnp.exp(-a))) * u).astype(bf16)
    part = jnp.dot(hidden, wd_ref[...].astype(bf16), preferred_element_type=f32)

    @pl.when(f == 0)
    def _():
        acc_ref[...] = part

    @pl.when(f != 0)
    def _():
        acc_ref[...] += part

    @pl.when(f == D_FF // FF_TILE - 1)
    def _():
        y_ref[...] = acc_ref[...].astype(bf16)


def _ffn(xe, w_gate, w_up, w_down):
    rows = BATCH * CAPACITY
    return pl.pallas_call(
        _ffn_kernel,
        grid=(N_EXPERTS, D_FF // FF_TILE),
        in_specs=[
            pl.BlockSpec((None, rows, D_MODEL), lambda e, f: (e, 0, 0)),
            pl.BlockSpec((None, D_MODEL, FF_TILE), lambda e, f: (e, 0, f)),
            pl.BlockSpec((None, D_MODEL, FF_TILE), lambda e, f: (e, 0, f)),
            pl.BlockSpec((None, FF_TILE, D_MODEL), lambda e, f: (e, f, 0)),
        ],
        out_specs=pl.BlockSpec((None, rows, D_MODEL), lambda e, f: (e, 0, 0)),
        out_shape=jax.ShapeDtypeStruct((N_EXPERTS, rows, D_MODEL), bf16),
        scratch_shapes=[pltpu.VMEM((rows, D_MODEL), f32)],
        compiler_params=_params(("arbitrary", "arbitrary"), 56),
        name="ffn",
    )(xe, w_gate, w_up, w_down)


SCATTER_UNROLL = 8
COPY_ROWS = 2048
NORM_TOKENS = 256


def _combine_kernel(idx_ref, gate_ref, y_ref, x1_ref, g_ref, o_ref, acc_ref, tile_ref):
    step = pl.program_id(1)

    @pl.when(step == 0)
    def _():
        def copy(i, c):
            rows = pl.ds(pl.multiple_of(i * COPY_ROWS, COPY_ROWS), COPY_ROWS)
            acc_ref[rows, :] = x1_ref[rows, :]
            return c
        lax.fori_loop(0, SEQ * SLAB_ROWS // COPY_ROWS, copy, 0)

    for k in range(EXPERTS_PER_STEP):
        base = (step * EXPERTS_PER_STEP + k) * CAPACITY
        for j in range(SLAB_ROWS):
            tile_ref[pl.ds(j * TILE_PITCH, CAPACITY), :] = (
                y_ref[k, :, j * LANES:(j + 1) * LANES].astype(f32))

        def group(g, c, base=base):
            updates = []
            for u in range(SCATTER_UNROLL):
                slot = g * SCATTER_UNROLL + u
                tok = idx_ref[0, base + slot]
                gate = gate_ref[0, base + slot]
                rows = pl.ds(pl.multiple_of(tok * SLAB_ROWS, SLAB_ROWS), SLAB_ROWS)
                contrib = tile_ref[pl.ds(slot, SLAB_ROWS, stride=TILE_PITCH), :]
                updates.append((rows, acc_ref[rows, :] + gate * contrib))
            for rows, value in updates:
                acc_ref[rows, :] = value
            return c
        lax.fori_loop(0, CAPACITY // SCATTER_UNROLL, group, 0)

    @pl.when(step == N_EXPERTS // EXPERTS_PER_STEP - 1)
    def _():
        def norm(i, c):
            first = i * (NORM_TOKENS * SLAB_ROWS)
            chunks = [acc_ref[pl.ds(first + j, NORM_TOKENS, stride=SLAB_ROWS), :]
                      for j in range(SLAB_ROWS)]
            squares = functools.reduce(lambda a, b: a + b, [ch * ch for ch in chunks])
            inv = lax.rsqrt(jnp.sum(squares, axis=1, keepdims=True) * (1.0 / D_MODEL) + RMS_EPS)
            rows = pl.ds(pl.multiple_of(i * NORM_TOKENS, NORM_TOKENS), NORM_TOKENS)
            for j in range(SLAB_ROWS):
                cols = slice(j * LANES, (j + 1) * LANES)
                o_ref[rows, cols] = chunks[j] * inv * g_ref[:, cols]
            return c
        lax.fori_loop(0, SEQ // NORM_TOKENS, norm, 0)


def _combine(idx, gate, y, x1_slab, g_final):
    smem = lambda: pl.BlockSpec((None, 1, N_EXPERTS * CAPACITY), lambda b, s: (b, 0, 0),
                                memory_space=pltpu.SMEM)
    return pl.pallas_call(
        _combine_kernel,
        grid=(BATCH, N_EXPERTS // EXPERTS_PER_STEP),
        in_specs=[
            smem(), smem(),
            pl.BlockSpec((EXPERTS_PER_STEP, None, CAPACITY, D_MODEL), lambda b, s: (s, b, 0, 0)),
            pl.BlockSpec((SEQ * SLAB_ROWS, LANES), lambda b, s: (b, 0)),
            pl.BlockSpec((1, D_MODEL), lambda b, s: (0, 0)),
        ],
        out_specs=pl.BlockSpec((SEQ, D_MODEL), lambda b, s: (b, 0)),
        out_shape=jax.ShapeDtypeStruct((TOKENS, D_MODEL), f32),
        scratch_shapes=[pltpu.VMEM((SEQ * SLAB_ROWS, LANES), f32),
                        pltpu.VMEM((SLAB_ROWS * TILE_PITCH, LANES), f32)],
        compiler_params=_params(("arbitrary", "arbitrary"), 56),
        name="combine",
    )(idx, gate, y, x1_slab, g_final)


def _rope_freqs():
    inv_freq = ROPE_THETA ** (-jnp.arange(ROT_HALF, dtype=f32) / ROT_HALF)
    return inv_freq.reshape(ROT_HALF, 1)


def kernel(x, positions, g_mix, w_in, conv_w, g_conv_out, g_attn_out, w_out, g_ffn, w_router,
           w_gate, w_up, w_down, g_final):
    x2 = x.reshape(TOKENS, D_MODEL)
    pc, qkv = _in_proj(x2, g_mix[0].reshape(1, D_MODEL), w_in[0].astype(bf16))
    yc = _conv(pc, conv_w[0], g_conv_out[0].reshape(1, CONV_WIDTH))
    ya = _attention(qkv.reshape(BATCH, SEQ, 3 * ATTN_WIDTH), positions.reshape(BATCH, 1, SEQ),
                    _rope_freqs())
    x1, h2, logits_t = _out_proj(yc, ya.reshape(TOKENS, ATTN_WIDTH), x2, w_out[0].astype(bf16),
                                 g_attn_out[0].reshape(1, ATTN_WIDTH), g_ffn[0].reshape(1, D_MODEL),
                                 w_router[0].T)
    idx, gate = _route(logits_t)
    idx = idx.reshape(BATCH, 1, N_EXPERTS * CAPACITY)
    gate = gate.reshape(BATCH, 1, N_EXPERTS * CAPACITY)
    xe = _gather(idx, h2)
    y = _ffn(xe.reshape(N_EXPERTS, BATCH * CAPACITY, D_MODEL), w_gate[0], w_up[0], w_down[0])
    out = _combine(idx, gate, y.reshape(N_EXPERTS, BATCH, CAPACITY, D_MODEL), x1,
                   g_final.reshape(1, D_MODEL))
    return out.reshape(BATCH, SEQ, D_MODEL)
```

```python
import functools

import jax
import jax.numpy as jnp
from jax import lax
from jax.experimental import pallas as pl
from jax.experimental.pallas import tpu as pltpu

D_MODEL = 1024
BATCH = 8
SEQ = 2048
TOKENS = BATCH * SEQ
CONV_WIDTH = 512
ATTN_WIDTH = 512
HEAD_DIM = 64
N_HEADS = 8
DILATED_PATTERNS = ((128, 1), (512, 4), (2048, 16))
N_PATTERNS = len(DILATED_PATTERNS)
ROPE_THETA = 500000.0
ROT_DIM = 16
ROT_HALF = ROT_DIM // 2
N_EXPERTS = 16
CAPACITY = 2 * SEQ // N_EXPERTS
D_FF = 2 * D_MODEL
PROJ_WIDTH = 3 * CONV_WIDTH + 3 * ATTN_WIDTH
RMS_EPS = 1e-6
NEG_INF = -1e30

LANES = 128
ROW_TILE = 512
FF_TILE = 512
BAND_HALF = 64
Q_CHUNK = 2 * BAND_HALF
K_WIN = 4 * BAND_HALF
CHUNK_GROUP = 16
HEADS_PER_STEP = LANES // HEAD_DIM
SUBLANES = 8
SLAB_ROWS = D_MODEL // LANES
TILE_PITCH = CAPACITY + SUBLANES
SLOT_SPLIT = 16
EXPERTS_PER_STEP = 4
MIB = 1024 * 1024

f32 = jnp.float32
bf16 = jnp.bfloat16


def _params(semantics, vmem_mib):
    return pltpu.CompilerParams(dimension_semantics=semantics, vmem_limit_bytes=vmem_mib * MIB)


def _rms(x, g):
    return x * lax.rsqrt(jnp.mean(x * x, axis=-1, keepdims=True) + RMS_EPS) * g


def _rope_table_kernel(pos_ref, invf_ref, cos_ref, s1_ref, s2_ref):
    ang = invf_ref[...] * pos_ref[...].astype(f32)
    trig = jnp.concatenate([jnp.cos(ang), jnp.sin(ang)], axis=0)
    lane = lax.broadcasted_iota(jnp.int32, (ROT_DIM, 3 * LANES), 1)
    row = lax.broadcasted_iota(jnp.int32, (ROT_DIM, 3 * LANES), 0)
    freq = row % ROT_HALF
    is_sin = row >= ROT_HALF
    at_lo = lane % HEAD_DIM == freq
    at_hi = lane % HEAD_DIM == freq + ROT_HALF
    table = lane // LANES
    place = (jnp.where((table == 0) & jnp.logical_not(is_sin) & (at_lo | at_hi), 1.0, 0.0)
             + jnp.where((table == 1) & is_sin & at_hi, 1.0, 0.0)
             - jnp.where((table == 2) & is_sin & at_lo, 1.0, 0.0)).astype(bf16)
    spread = jnp.zeros((SEQ, 3 * LANES), f32)
    rest = trig
    for _ in range(3):
        piece = rest.astype(bf16)
        rest = rest - piece.astype(f32)
        spread = spread + lax.dot_general(piece, place, (((0,), (0,)), ((), ())),
                                          preferred_element_type=f32)
    unrotated = jnp.where(
        lax.broadcasted_iota(jnp.int32, (1, LANES), 1) % HEAD_DIM >= ROT_DIM, 1.0, 0.0)
    cos_ref[...] = spread[:, :LANES] + unrotated
    s1_ref[...] = spread[:, LANES:2 * LANES]
    s2_ref[...] = spread[:, 2 * LANES:]


def _rope_tables(pos_row, invf_col):
    blk = pl.BlockSpec((SEQ, LANES), lambda b: (b, 0))
    return pl.pallas_call(
        _rope_table_kernel,
        grid=(BATCH,),
        in_specs=[
            pl.BlockSpec((None, 1, SEQ), lambda b: (b, 0, 0)),
            pl.BlockSpec((ROT_HALF, 1), lambda b: (0, 0)),
        ],
        out_specs=[blk, blk, blk],
        out_shape=[jax.ShapeDtypeStruct((TOKENS, LANES), f32)] * 3,
        compiler_params=_params(("arbitrary",), 32),
        name="rope_tables",
    )(pos_row, invf_col)


def _in_proj_kernel(x_ref, g_ref, w_ref, cos_ref, s1_ref, s2_ref, pc_ref, qkv_ref):
    h = _rms(x_ref[...], g_ref[...]).astype(bf16)
    p = jnp.dot(h, w_ref[...], preferred_element_type=f32)
    pc_ref[...] = p[:, :3 * CONV_WIDTH]
    cs, s1, s2 = cos_ref[...], s1_ref[...], s2_ref[...]
    for blk in range(2 * ATTN_WIDTH // LANES):
        x = p[:, 3 * CONV_WIDTH + blk * LANES:3 * CONV_WIDTH + (blk + 1) * LANES]
        y = x * cs + pltpu.roll(x, ROT_HALF, 1) * s1 + pltpu.roll(x, LANES - ROT_HALF, 1) * s2
        if blk < ATTN_WIDTH // LANES:
            y = y * HEAD_DIM ** -0.5
        qkv_ref[:, blk * LANES:(blk + 1) * LANES] = y
    qkv_ref[:, 2 * ATTN_WIDTH:] = p[:, 3 * CONV_WIDTH + 2 * ATTN_WIDTH:]


def _in_proj(x2, g_mix, w_in_bf, rope):
    n = TOKENS // ROW_TILE
    table = pl.BlockSpec((ROW_TILE, LANES), lambda i: (i, 0))
    return pl.pallas_call(
        _in_proj_kernel,
        grid=(n,),
        in_specs=[
            pl.BlockSpec((ROW_TILE, D_MODEL), lambda i: (i, 0)),
            pl.BlockSpec((1, D_MODEL), lambda i: (0, 0)),
            pl.BlockSpec((D_MODEL, PROJ_WIDTH), lambda i: (0, 0)),
            table, table, table,
        ],
        out_specs=[
            pl.BlockSpec((ROW_TILE, 3 * CONV_WIDTH), lambda i: (i, 0)),
            pl.BlockSpec((ROW_TILE, 3 * ATTN_WIDTH), lambda i: (i, 0)),
        ],
        out_shape=[
            jax.ShapeDtypeStruct((TOKENS, 3 * CONV_WIDTH), f32),
            jax.ShapeDtypeStruct((TOKENS, 3 * ATTN_WIDTH), f32),
        ],
        compiler_params=_params(("arbitrary",), 48),
        name="in_proj",
    )(x2, g_mix, w_in_bf, *rope)


CONV_PAD = 8
CONV_ROWS = 512


def _conv_kernel(pc_ref, w_ref, g_ref, o_ref, u_ref):
    zeros = jnp.zeros((CONV_PAD, CONV_WIDTH), f32)
    u_ref[pl.ds(0, CONV_PAD), :] = zeros
    u_ref[pl.ds(CONV_PAD + SEQ, CONV_PAD), :] = zeros
    for i in range(SEQ // CONV_ROWS):
        rows = pl.ds(i * CONV_ROWS, CONV_ROWS)
        u_ref[pl.ds(CONV_PAD + i * CONV_ROWS, CONV_ROWS), :] = (
            pc_ref[rows, CONV_WIDTH:2 * CONV_WIDTH] * pc_ref[rows, 2 * CONV_WIDTH:])
    w = w_ref[...]
    g = g_ref[...]
    for i in range(SEQ // CONV_ROWS):
        base = CONV_PAD + i * CONV_ROWS
        prev = u_ref[pl.ds(base - 1, CONV_ROWS), :]
        cur = u_ref[pl.ds(base, CONV_ROWS), :]
        nxt = u_ref[pl.ds(base + 1, CONV_ROWS), :]
        y = pc_ref[pl.ds(i * CONV_ROWS, CONV_ROWS), :CONV_WIDTH] * (
            w[0:1] * prev + w[1:2] * cur + w[2:3] * nxt)
        o_ref[pl.ds(i * CONV_ROWS, CONV_ROWS), :] = _rms(y, g).astype(bf16)


def _conv(pc, conv_w, g_conv):
    return pl.pallas_call(
        _conv_kernel,
        grid=(BATCH,),
        in_specs=[
            pl.BlockSpec((SEQ, 3 * CONV_WIDTH), lambda b: (b, 0)),
            pl.BlockSpec((3, CONV_WIDTH), lambda b: (0, 0)),
            pl.BlockSpec((1, CONV_WIDTH), lambda b: (0, 0)),
        ],
        out_specs=pl.BlockSpec((SEQ, CONV_WIDTH), lambda b: (b, 0)),
        out_shape=jax.ShapeDtypeStruct((TOKENS, CONV_WIDTH), bf16),
        scratch_shapes=[pltpu.VMEM((SEQ + 2 * CONV_PAD, CONV_WIDTH), f32)],
        compiler_params=_params(("arbitrary",), 48),
        name="conv",
    )(pc, conv_w, g_conv)


ROPE_ROWS = 256
CAP_INTERIOR, CAP_FIRST, CAP_LAST, CAP_SINGLE = range(4)
MASK_OPEN = float(jnp.finfo(jnp.float32).max)


def _attn_kernel(q_ref, k_ref, v_ref, o_ref, cap_ref,
                 qlo_ref, qhi_ref, kpad_ref, vlo_ref, vhi_ref,
                 num_ref, m_ref, l_ref):
    @pl.when(pl.program_id(1) == 0)
    def _():
        row = lax.broadcasted_iota(jnp.int32, (2 * Q_CHUNK, K_WIN), 0) % Q_CHUNK
        col = lax.broadcasted_iota(jnp.int32, (2 * Q_CHUNK, K_WIN), 1)
        band = (col - row >= 0) & (col - row <= 2 * BAND_HALF)
        not_before = col >= BAND_HALF
        not_after = col < K_WIN - BAND_HALF
        for idx, ok in ((CAP_INTERIOR, band), (CAP_FIRST, band & not_before),
                        (CAP_LAST, band & not_after), (CAP_SINGLE, band & not_before & not_after)):
            cap_ref[idx] = jnp.where(ok, MASK_OPEN, NEG_INF)

    lane_lo = lax.broadcasted_iota(jnp.int32, (Q_CHUNK, LANES), 1) < HEAD_DIM
    zero_pad = jnp.zeros((BAND_HALF, LANES), bf16)

    def zero_rows(start):
        for ref in (kpad_ref, vlo_ref, vhi_ref):
            ref[pl.ds(start, BAND_HALF), :] = zero_pad

    def stage(src, q_row, pad_row):
        qf, kf, vf = q_ref[src, :], k_ref[src, :], v_ref[src, :]
        q_dst, p_dst = pl.ds(q_row, Q_CHUNK), pl.ds(pad_row, Q_CHUNK)
        qlo_ref[q_dst, :] = jnp.where(lane_lo, qf, 0.0).astype(bf16)
        qhi_ref[q_dst, :] = jnp.where(lane_lo, 0.0, qf).astype(bf16)
        kpad_ref[p_dst, :] = kf.astype(bf16)
        vlo_ref[p_dst, :] = jnp.where(lane_lo, vf, 0.0).astype(bf16)
        vhi_ref[p_dst, :] = jnp.where(lane_lo, 0.0, vf).astype(bf16)

    def chunk(pat, out_rows, q_row, pad_row, cap):
        q2 = jnp.concatenate([qlo_ref[pl.ds(q_row, Q_CHUNK), :], qhi_ref[pl.ds(q_row, Q_CHUNK), :]],
                             axis=0)
        s = lax.dot_general(q2, kpad_ref[pl.ds(pad_row, K_WIN), :], (((1,), (1,)), ((), ())),
                            preferred_element_type=f32)
        s = jnp.minimum(s, cap_ref[cap])
        m = jnp.max(s, axis=1, keepdims=True)
        p = jnp.exp(s - m)
        l = jnp.sum(p, axis=1, keepdims=True)
        pb = p.astype(bf16)
        num = (jnp.dot(pb[:Q_CHUNK], vlo_ref[pl.ds(pad_row, K_WIN), :], preferred_element_type=f32)
               + jnp.dot(pb[Q_CHUNK:], vhi_ref[pl.ds(pad_row, K_WIN), :], preferred_element_type=f32))
        num_ref[pat, out_rows, :] = num
        m_ref[pat, out_rows, :] = jnp.where(lane_lo, m[:Q_CHUNK], m[Q_CHUNK:])
        l_ref[pat, out_rows, :] = jnp.where(lane_lo, l[:Q_CHUNK], l[Q_CHUNK:])

    for pat, (_, dil) in enumerate(DILATED_PATTERNS):
        length = SEQ // dil
        n_chunks = length // Q_CHUNK
        per_group = max(1, CHUNK_GROUP // n_chunks)
        region = length + 2 * BAND_HALF
        for u in range(per_group):
            zero_rows(u * region)
            zero_rows(u * region + BAND_HALF + length)

        def group(g, c, pat=pat, dil=dil, length=length, n_chunks=n_chunks,
                  per_group=per_group, region=region):
            def rows_of(r, i):
                if dil == 1:
                    return pl.ds(i * Q_CHUNK, Q_CHUNK)
                return pl.ds(r + dil * Q_CHUNK * i, Q_CHUNK, stride=dil)

            for u in range(per_group):
                r = g * per_group + u
                for i in range(n_chunks):
                    stage(rows_of(r, i), u * length + i * Q_CHUNK,
                          u * region + BAND_HALF + i * Q_CHUNK)
            for u in range(per_group):
                r = g * per_group + u
                for j in range(n_chunks):
                    if n_chunks == 1:
                        cap = CAP_SINGLE
                    else:
                        cap = CAP_FIRST if j == 0 else CAP_LAST if j == n_chunks - 1 else CAP_INTERIOR
                    chunk(pat, rows_of(r, j), u * length + j * Q_CHUNK, u * region + j * Q_CHUNK, cap)
            return c

        if dil == per_group:
            group(0, 0)
        else:
            lax.fori_loop(0, dil // per_group, group, 0)

    def mix(i, c):
        rows = pl.ds(pl.multiple_of(i * ROPE_ROWS, ROPE_ROWS), ROPE_ROWS)
        ms = [m_ref[p, rows, :] for p in range(N_PATTERNS)]
        m_all = functools.reduce(jnp.maximum, ms)
        num = jnp.zeros((ROPE_ROWS, LANES), f32)
        den = jnp.zeros((ROPE_ROWS, LANES), f32)
        for p in range(N_PATTERNS):
            a = jnp.exp(ms[p] - m_all)
            den = den + a * l_ref[p, rows, :]
            num = num + a * num_ref[p, rows, :]
        o_ref[rows, :] = num / den
        return c
    lax.fori_loop(0, SEQ // ROPE_ROWS, mix, 0)


def _attention(qkv3):
    n_hp = N_HEADS // HEADS_PER_STEP
    blk = lambda off: pl.BlockSpec((None, SEQ, LANES), lambda b, h, off=off: (b, 0, off + h))
    pad_rows = max(max(1, CHUNK_GROUP // (SEQ // d // Q_CHUNK)) * (SEQ // d + 2 * BAND_HALF)
                   for _, d in DILATED_PATTERNS)
    stats = pltpu.VMEM((N_PATTERNS, SEQ, LANES), f32)
    return pl.pallas_call(
        _attn_kernel,
        grid=(BATCH, n_hp),
        in_specs=[blk(0), blk(n_hp), blk(2 * n_hp)],
        out_specs=pl.BlockSpec((None, SEQ, LANES), lambda b, h: (b, 0, h)),
        out_shape=jax.ShapeDtypeStruct((BATCH, SEQ, ATTN_WIDTH), f32),
        scratch_shapes=[
            pltpu.VMEM((4, 2 * Q_CHUNK, K_WIN), f32),
            pltpu.VMEM((SEQ, LANES), bf16), pltpu.VMEM((SEQ, LANES), bf16),
            pltpu.VMEM((pad_rows, LANES), bf16), pltpu.VMEM((pad_rows, LANES), bf16),
            pltpu.VMEM((pad_rows, LANES), bf16),
            stats, stats, stats,
        ],
        compiler_params=_params(("arbitrary", "arbitrary"), 48),
        name="attention",
    )(qkv3, qkv3, qkv3)


def _store_token_major(ref, x):
    for j in range(SLAB_ROWS):
        ref[pl.ds(j, x.shape[0], stride=SLAB_ROWS), :] = x[:, j * LANES:(j + 1) * LANES]


def _dot_nt(a, b):
    return lax.dot_general(a, b, (((1,), (1,)), ((), ())), preferred_element_type=f32)


def _out_proj_kernel(yc_ref, ya_ref, x_ref, w_ref, ga_ref, gf_ref, wr_ref,
                     x1_ref, h2_ref, lg_ref):
    ya = _rms(ya_ref[...], ga_ref[...]).astype(bf16)
    mix = (jnp.dot(yc_ref[...], w_ref[pl.ds(0, CONV_WIDTH), :], preferred_element_type=f32)
           + jnp.dot(ya, w_ref[pl.ds(CONV_WIDTH, ATTN_WIDTH), :], preferred_element_type=f32))
    x1 = x_ref[...] + mix
    _store_token_major(x1_ref, x1)
    h2 = _rms(x1, gf_ref[...])
    _store_token_major(h2_ref, h2)
    wr = wr_ref[...]
    wr_hi = wr.astype(bf16)
    wr_lo = (wr - wr_hi.astype(f32)).astype(bf16)
    h2_hi = h2.astype(bf16)
    h2_lo = (h2 - h2_hi.astype(f32)).astype(bf16)
    both = _dot_nt(jnp.concatenate([wr_hi, wr_lo], axis=0), h2_hi)
    lg_ref[...] = both[:N_EXPERTS] + both[N_EXPERTS:] + _dot_nt(wr_hi, h2_lo)


def _out_proj(yc, ya, x2, w_out_bf, g_attn, g_ffn, w_router_t):
    n = TOKENS // ROW_TILE
    slab = pl.BlockSpec((ROW_TILE * SLAB_ROWS, LANES), lambda i: (i, 0))
    slab_shape = jax.ShapeDtypeStruct((TOKENS * SLAB_ROWS, LANES), f32)
    return pl.pallas_call(
        _out_proj_kernel,
        grid=(n,),
        in_specs=[
            pl.BlockSpec((ROW_TILE, CONV_WIDTH), lambda i: (i, 0)),
            pl.BlockSpec((ROW_TILE, ATTN_WIDTH), lambda i: (i, 0)),
            pl.BlockSpec((ROW_TILE, D_MODEL), lambda i: (i, 0)),
            pl.BlockSpec((D_MODEL, D_MODEL), lambda i: (0, 0)),
            pl.BlockSpec((1, ATTN_WIDTH), lambda i: (0, 0)),
            pl.BlockSpec((1, D_MODEL), lambda i: (0, 0)),
            pl.BlockSpec((N_EXPERTS, D_MODEL), lambda i: (0, 0)),
        ],
        out_specs=[slab, slab, pl.BlockSpec((N_EXPERTS, ROW_TILE), lambda i: (0, i))],
        out_shape=[slab_shape, slab_shape, jax.ShapeDtypeStruct((N_EXPERTS, TOKENS), f32)],
        compiler_params=_params(("arbitrary",), 48),
        name="out_proj",
    )(yc, ya, x2, w_out_bf, g_attn, g_ffn, w_router_t)


N_LANE_BLOCKS = SEQ // LANES


def _excl_cumsum_tokens(x, tri):
    stacked = jnp.concatenate([x[:, j * LANES:(j + 1) * LANES] for j in range(N_LANE_BLOCKS)], axis=0)
    within = jnp.dot(stacked.astype(bf16), tri, preferred_element_type=f32)
    totals = jnp.sum(stacked, axis=1, keepdims=True)
    out, offset = [], jnp.zeros((N_EXPERTS, 1), f32)
    for j in range(N_LANE_BLOCKS):
        rows = slice(j * N_EXPERTS, (j + 1) * N_EXPERTS)
        out.append(within[rows] + offset)
        offset = offset + totals[rows]
    return jnp.concatenate(out, axis=1)


def _route_kernel(lg_ref, idx_ref, gate_ref):
    x = lg_ref[...]
    e = jnp.exp(x - jnp.max(x, axis=0, keepdims=True))
    aff = e / jnp.sum(e, axis=0, keepdims=True)
    thr = jnp.zeros((N_EXPERTS, 1), jnp.int32)
    for bit in range(30, -1, -1):
        cand = thr | (1 << bit)
        hit = aff >= pltpu.bitcast(cand, f32)
        cnt = jnp.sum(jnp.where(hit, 1.0, 0.0), axis=1, keepdims=True)
        thr = jnp.where(cnt >= float(CAPACITY), cand, thr)
    above = aff >= pltpu.bitcast(thr + 1, f32)
    tie = (aff >= pltpu.bitcast(thr, f32)) & jnp.logical_not(above)
    need = float(CAPACITY) - jnp.sum(jnp.where(above, 1.0, 0.0), axis=1, keepdims=True)
    tri = jnp.where(lax.broadcasted_iota(jnp.int32, (LANES, LANES), 0)
                    < lax.broadcasted_iota(jnp.int32, (LANES, LANES), 1), 1.0, 0.0).astype(bf16)
    tie_rank = _excl_cumsum_tokens(jnp.where(tie, 1.0, 0.0), tri)
    sel = above | (tie & (tie_rank < need))
    rank = jnp.where(sel, _excl_cumsum_tokens(jnp.where(sel, 1.0, 0.0), tri), -1.0)

    high = jnp.floor(rank * (1.0 / SLOT_SPLIT))
    low = rank - high * SLOT_SPLIT
    digit = lax.broadcasted_iota(jnp.int32, (SLOT_SPLIT, SEQ), 0).astype(f32)
    tok = lax.broadcasted_iota(jnp.int32, (1, SEQ), 1)
    g1 = aff.astype(bf16).astype(f32)
    g2 = (aff - g1).astype(bf16).astype(f32)
    g3 = aff - g1 - g2
    per_expert = lambda v: jnp.broadcast_to(v.astype(f32), (N_EXPERTS, SEQ))
    payloads = [per_expert(tok // LANES), per_expert(tok % LANES), g1, g2, g3]
    high_rows = jnp.concatenate(
        [jnp.where(high[e:e + 1] == digit, 1.0, 0.0) for e in range(N_EXPERTS)], axis=0).astype(bf16)
    low_hit = [low[e:e + 1] == digit for e in range(N_EXPERTS)]
    n_rows = N_EXPERTS * SLOT_SPLIT
    same_expert = (lax.broadcasted_iota(jnp.int32, (n_rows, n_rows), 0) // SLOT_SPLIT
                   == lax.broadcasted_iota(jnp.int32, (n_rows, n_rows), 1) // SLOT_SPLIT)
    fold = jnp.where(lax.broadcasted_iota(jnp.int32, (n_rows, SLOT_SPLIT), 0) % SLOT_SPLIT
                     == lax.broadcasted_iota(jnp.int32, (n_rows, SLOT_SPLIT), 1), 1.0, 0.0).astype(bf16)
    folded = []
    for val in payloads:
        rows = jnp.concatenate(
            [jnp.where(low_hit[e], val[e:e + 1], 0.0) for e in range(N_EXPERTS)], axis=0).astype(bf16)
        picked = jnp.where(same_expert, _dot_nt(high_rows, rows), 0.0).astype(bf16)
        folded.append(jnp.dot(picked, fold, preferred_element_type=f32))
    idx_ref[...] = (folded[0] * float(LANES) + folded[1]).astype(jnp.int32)
    gate_ref[...] = folded[2] + folded[3] + folded[4]


def _route(logits_t):
    n_rows = N_EXPERTS * SLOT_SPLIT
    blk = pl.BlockSpec((None, n_rows, CAPACITY // SLOT_SPLIT), lambda b: (b, 0, 0))
    shape = (BATCH, n_rows, CAPACITY // SLOT_SPLIT)
    return pl.pallas_call(
        _route_kernel,
        grid=(BATCH,),
        in_specs=[pl.BlockSpec((N_EXPERTS, SEQ), lambda b: (0, b))],
        out_specs=[blk, blk],
        out_shape=[jax.ShapeDtypeStruct(shape, jnp.int32), jax.ShapeDtypeStruct(shape, f32)],
        compiler_params=_params(("arbitrary",), 40),
        name="route",
    )(logits_t)


def _gather_kernel(idx_ref, h2_ref, xe_ref, tile_ref):
    def expert(e, c):
        base = e * CAPACITY
        for slot in range(CAPACITY):
            tok = idx_ref[0, base + slot]
            slab = h2_ref[pl.ds(pl.multiple_of(tok * SLAB_ROWS, SLAB_ROWS), SLAB_ROWS), :]
            tile_ref[pl.ds(slot, SLAB_ROWS, stride=TILE_PITCH), :] = slab
        for j in range(SLAB_ROWS):
            xe_ref[e, :, j * LANES:(j + 1) * LANES] = (
                tile_ref[pl.ds(j * TILE_PITCH, CAPACITY), :].astype(bf16))
        return c
    lax.fori_loop(0, N_EXPERTS, expert, 0)


def _gather(idx, h2_slab):
    return pl.pallas_call(
        _gather_kernel,
        grid=(BATCH,),
        in_specs=[
            pl.BlockSpec((None, 1, N_EXPERTS * CAPACITY), lambda b: (b, 0, 0),
                         memory_space=pltpu.SMEM),
            pl.BlockSpec((SEQ * SLAB_ROWS, LANES), lambda b: (b, 0)),
        ],
        out_specs=pl.BlockSpec((N_EXPERTS, None, CAPACITY, D_MODEL), lambda b: (0, b, 0, 0)),
        out_shape=jax.ShapeDtypeStruct((N_EXPERTS, BATCH, CAPACITY, D_MODEL), bf16),
        scratch_shapes=[pltpu.VMEM((SLAB_ROWS * TILE_PITCH, LANES), f32)],
        compiler_params=_params(("arbitrary",), 48),
        name="gather",
    )(idx, h2_slab)


def _ffn_kernel(xe_ref, wg_ref, wu_ref, wd_ref, y_ref, acc_ref):
    f = pl.program_id(1)

    @pl.when(f == 0)
    def _():
        acc_ref[...] = jnp.zeros_like(acc_ref)

    xe = xe_ref[...]
    a = jnp.dot(xe, wg_ref[...].astype(bf16), preferred_element_type=f32)
    u = jnp.dot(xe, wu_ref[...].astype(bf16), preferred_element_type=f32)
    hidden = (a * (1.0 / (1.0 + jnp.exp(-a))) * u).astype(bf16)
    acc_ref[...] += jnp.dot(hidden, wd_ref[...].astype(bf16), preferred_element_type=f32)

    @pl.when(f == D_FF // FF_TILE - 1)
    def _():
        y_ref[...] = acc_ref[...].astype(bf16)


def _ffn(xe, w_gate, w_up, w_down):
    rows = BATCH * CAPACITY
    return pl.pallas_call(
        _ffn_kernel,
        grid=(N_EXPERTS, D_FF // FF_TILE),
        in_specs=[
            pl.BlockSpec((None, rows, D_MODEL), lambda e, f: (e, 0, 0)),
            pl.BlockSpec((None, D_MODEL, FF_TILE), lambda e, f: (e, 0, f)),
            pl.BlockSpec((None, D_MODEL, FF_TILE), lambda e, f: (e, 0, f)),
            pl.BlockSpec((None, FF_TILE, D_MODEL), lambda e, f: (e, f, 0)),
        ],
        out_specs=pl.BlockSpec((None, rows, D_MODEL), lambda e, f: (e, 0, 0)),
        out_shape=jax.ShapeDtypeStruct((N_EXPERTS, rows, D_MODEL), bf16),
        scratch_shapes=[pltpu.VMEM((rows, D_MODEL), f32)],
        compiler_params=_params(("arbitrary", "arbitrary"), 56),
        name="ffn",
    )(xe, w_gate, w_up, w_down)


SCATTER_UNROLL = 8
COPY_ROWS = 2048
NORM_TOKENS = 256


def _combine_kernel(idx_ref, gate_ref, y_ref, x1_ref, g_ref, o_ref, acc_ref, tile_ref):
    step = pl.program_id(1)

    @pl.when(step == 0)
    def _():
        def copy(i, c):
            rows = pl.ds(pl.multiple_of(i * COPY_ROWS, COPY_ROWS), COPY_ROWS)
            acc_ref[rows, :] = x1_ref[rows, :]
            return c
        lax.fori_loop(0, SEQ * SLAB_ROWS // COPY_ROWS, copy, 0)

    def expert(k, c):
        base = (step * EXPERTS_PER_STEP + k) * CAPACITY
        for j in range(SLAB_ROWS):
            tile_ref[pl.ds(j * TILE_PITCH, CAPACITY), :] = (
                y_ref[k, :, j * LANES:(j + 1) * LANES].astype(f32))
        for first in range(0, CAPACITY, SCATTER_UNROLL):
            updates = []
            for slot in range(first, first + SCATTER_UNROLL):
                tok = idx_ref[0, base + slot]
                gate = gate_ref[0, base + slot]
                rows = pl.ds(pl.multiple_of(tok * SLAB_ROWS, SLAB_ROWS), SLAB_ROWS)
                contrib = tile_ref[pl.ds(slot, SLAB_ROWS, stride=TILE_PITCH), :]
                updates.append((rows, acc_ref[rows, :] + gate * contrib))
            for rows, value in updates:
                acc_ref[rows, :] = value
        return c
    lax.fori_loop(0, EXPERTS_PER_STEP, expert, 0)

    @pl.when(step == N_EXPERTS // EXPERTS_PER_STEP - 1)
    def _():
        def norm(i, c):
            first = i * (NORM_TOKENS * SLAB_ROWS)
            chunks = [acc_ref[pl.ds(first + j, NORM_TOKENS, stride=SLAB_ROWS), :]
                      for j in range(SLAB_ROWS)]
            squares = functools.reduce(lambda a, b: a + b, [ch * ch for ch in chunks])
            inv = lax.rsqrt(jnp.sum(squares, axis=1, keepdims=True) * (1.0 / D_MODEL) + RMS_EPS)
            rows = pl.ds(pl.multiple_of(i * NORM_TOKENS, NORM_TOKENS), NORM_TOKENS)
            for j in range(SLAB_ROWS):
                cols = slice(j * LANES, (j + 1) * LANES)
                o_ref[rows, cols] = chunks[j] * inv * g_ref[:, cols]
            return c
        lax.fori_loop(0, SEQ // NORM_TOKENS, norm, 0)


def _combine(idx, gate, y, x1_slab, g_final):
    smem = lambda: pl.BlockSpec((None, 1, N_EXPERTS * CAPACITY), lambda b, s: (b, 0, 0),
                                memory_space=pltpu.SMEM)
    return pl.pallas_call(
        _combine_kernel,
        grid=(BATCH, N_EXPERTS // EXPERTS_PER_STEP),
        in_specs=[
            smem(), smem(),
            pl.BlockSpec((EXPERTS_PER_STEP, None, CAPACITY, D_MODEL), lambda b, s: (s, b, 0, 0)),
            pl.BlockSpec((SEQ * SLAB_ROWS, LANES), lambda b, s: (b, 0)),
            pl.BlockSpec((1, D_MODEL), lambda b, s: (0, 0)),
        ],
        out_specs=pl.BlockSpec((SEQ, D_MODEL), lambda b, s: (b, 0)),
        out_shape=jax.ShapeDtypeStruct((TOKENS, D_MODEL), f32),
        scratch_shapes=[pltpu.VMEM((SEQ * SLAB_ROWS, LANES), f32),
                        pltpu.VMEM((SLAB_ROWS * TILE_PITCH, LANES), f32)],
        compiler_params=_params(("arbitrary", "arbitrary"), 56),
        name="combine",
    )(idx, gate, y, x1_slab, g_final)


def _rope_freqs():
    inv_freq = ROPE_THETA ** (-jnp.arange(ROT_HALF, dtype=f32) / ROT_HALF)
    return inv_freq.reshape(ROT_HALF, 1)


def kernel(x, positions, g_mix, w_in, conv_w, g_conv_out, g_attn_out, w_out, g_ffn, w_router,
           w_gate, w_up, w_down, g_final):
    x2 = x.reshape(TOKENS, D_MODEL)
    rope = _rope_tables(positions.reshape(BATCH, 1, SEQ), _rope_freqs())
    pc, qkv = _in_proj(x2, g_mix[0].reshape(1, D_MODEL), w_in[0].astype(bf16), rope)
    yc = _conv(pc, conv_w[0], g_conv_out[0].reshape(1, CONV_WIDTH))
    ya = _attention(qkv.reshape(BATCH, SEQ, 3 * ATTN_WIDTH))
    x1, h2, logits_t = _out_proj(yc, ya.reshape(TOKENS, ATTN_WIDTH), x2, w_out[0].astype(bf16),
                                 g_attn_out[0].reshape(1, ATTN_WIDTH), g_ffn[0].reshape(1, D_MODEL),
                                 w_router[0].T)
    idx, gate = _route(logits_t)
    idx = idx.reshape(BATCH, 1, N_EXPERTS * CAPACITY)
    gate = gate.reshape(BATCH, 1, N_EXPERTS * CAPACITY)
    xe = _gather(idx, h2)
    y = _ffn(xe.reshape(N_EXPERTS, BATCH * CAPACITY, D_MODEL), w_gate[0], w_up[0], w_down[0])
    out = _combine(idx, gate, y.reshape(N_EXPERTS, BATCH, CAPACITY, D_MODEL), x1,
                   g_final.reshape(1, D_MODEL))
    return out.reshape(BATCH, SEQ, D_MODEL)
```

```python
import functools

import jax
import jax.numpy as jnp
from jax import lax
from jax.experimental import pallas as pl
from jax.experimental.pallas import tpu as pltpu

D_MODEL = 1024
BATCH = 8
SEQ = 2048
TOKENS = BATCH * SEQ
CONV_WIDTH = 512
ATTN_WIDTH = 512
HEAD_DIM = 64
N_HEADS = 8
DILATED_PATTERNS = ((128, 1), (512, 4), (2048, 16))
N_PATTERNS = len(DILATED_PATTERNS)
ROPE_THETA = 500000.0
ROT_DIM = 16
ROT_HALF = ROT_DIM // 2
N_EXPERTS = 16
CAPACITY = 2 * SEQ // N_EXPERTS
D_FF = 2 * D_MODEL
PROJ_WIDTH = 3 * CONV_WIDTH + 3 * ATTN_WIDTH
RMS_EPS = 1e-6
NEG_INF = -1e30

LANES = 128
ROW_TILE = 512
FF_TILE = 512
BAND_HALF = 64
Q_CHUNK = 2 * BAND_HALF
K_WIN = 4 * BAND_HALF
HEADS_PER_STEP = LANES // HEAD_DIM
SUBLANES = 8
SLAB_ROWS = D_MODEL // LANES
TILE_PITCH = CAPACITY + SUBLANES
SLOT_SPLIT = 16
EXPERTS_PER_STEP = 4
MIB = 1024 * 1024

f32 = jnp.float32
bf16 = jnp.bfloat16


def _params(semantics, vmem_mib):
    return pltpu.CompilerParams(dimension_semantics=semantics, vmem_limit_bytes=vmem_mib * MIB)


def _rms(x, g):
    return x * lax.rsqrt(jnp.mean(x * x, axis=-1, keepdims=True) + RMS_EPS) * g


def _rope_table_kernel(pos_ref, invf_ref, cos_ref, s1_ref, s2_ref):
    ang = invf_ref[...] * pos_ref[...].astype(f32)
    trig = jnp.concatenate([jnp.cos(ang), jnp.sin(ang)], axis=0)
    lane = lax.broadcasted_iota(jnp.int32, (ROT_DIM, 3 * LANES), 1)
    row = lax.broadcasted_iota(jnp.int32, (ROT_DIM, 3 * LANES), 0)
    freq = row % ROT_HALF
    is_sin = row >= ROT_HALF
    at_lo = lane % HEAD_DIM == freq
    at_hi = lane % HEAD_DIM == freq + ROT_HALF
    table = lane // LANES
    place = (jnp.where((table == 0) & jnp.logical_not(is_sin) & (at_lo | at_hi), 1.0, 0.0)
             + jnp.where((table == 1) & is_sin & at_hi, 1.0, 0.0)
             - jnp.where((table == 2) & is_sin & at_lo, 1.0, 0.0)).astype(bf16)
    spread = jnp.zeros((SEQ, 3 * LANES), f32)
    rest = trig
    for _ in range(3):
        piece = rest.astype(bf16)
        rest = rest - piece.astype(f32)
        spread = spread + lax.dot_general(piece, place, (((0,), (0,)), ((), ())),
                                          preferred_element_type=f32)
    unrotated = jnp.where(
        lax.broadcasted_iota(jnp.int32, (1, LANES), 1) % HEAD_DIM >= ROT_DIM, 1.0, 0.0)
    cos_ref[...] = spread[:, :LANES] + unrotated
    s1_ref[...] = spread[:, LANES:2 * LANES]
    s2_ref[...] = spread[:, 2 * LANES:]


def _rope_tables(pos_row, invf_col):
    blk = pl.BlockSpec((SEQ, LANES), lambda b: (b, 0))
    return pl.pallas_call(
        _rope_table_kernel,
        grid=(BATCH,),
        in_specs=[
            pl.BlockSpec((None, 1, SEQ), lambda b: (b, 0, 0)),
            pl.BlockSpec((ROT_HALF, 1), lambda b: (0, 0)),
        ],
        out_specs=[blk, blk, blk],
        out_shape=[jax.ShapeDtypeStruct((TOKENS, LANES), f32)] * 3,
        compiler_params=_params(("arbitrary",), 32),
        name="rope_tables",
    )(pos_row, invf_col)


def _in_proj_kernel(x_ref, g_ref, w_ref, cos_ref, s1_ref, s2_ref, pc_ref, qkv_ref):
    h = _rms(x_ref[...], g_ref[...]).astype(bf16)
    p = jnp.dot(h, w_ref[...], preferred_element_type=f32)
    pc_ref[...] = p[:, :3 * CONV_WIDTH]
    cs, s1, s2 = cos_ref[...], s1_ref[...], s2_ref[...]
    for blk in range(2 * ATTN_WIDTH // LANES):
        x = p[:, 3 * CONV_WIDTH + blk * LANES:3 * CONV_WIDTH + (blk + 1) * LANES]
        y = x * cs + pltpu.roll(x, ROT_HALF, 1) * s1 + pltpu.roll(x, LANES - ROT_HALF, 1) * s2
        if blk < ATTN_WIDTH // LANES:
            y = y * HEAD_DIM ** -0.5
        qkv_ref[:, blk * LANES:(blk + 1) * LANES] = y
    qkv_ref[:, 2 * ATTN_WIDTH:] = p[:, 3 * CONV_WIDTH + 2 * ATTN_WIDTH:]


def _in_proj(x2, g_mix, w_in_bf, rope):
    n = TOKENS // ROW_TILE
    table = pl.BlockSpec((ROW_TILE, LANES), lambda i: (i, 0))
    return pl.pallas_call(
        _in_proj_kernel,
        grid=(n,),
        in_specs=[
            pl.BlockSpec((ROW_TILE, D_MODEL), lambda i: (i, 0)),
            pl.BlockSpec((1, D_MODEL), lambda i: (0, 0)),
            pl.BlockSpec((D_MODEL, PROJ_WIDTH), lambda i: (0, 0)),
            table, table, table,
        ],
        out_specs=[
            pl.BlockSpec((ROW_TILE, 3 * CONV_WIDTH), lambda i: (i, 0)),
            pl.BlockSpec((ROW_TILE, 3 * ATTN_WIDTH), lambda i: (i, 0)),
        ],
        out_shape=[
            jax.ShapeDtypeStruct((TOKENS, 3 * CONV_WIDTH), f32),
            jax.ShapeDtypeStruct((TOKENS, 3 * ATTN_WIDTH), f32),
        ],
        compiler_params=_params(("arbitrary",), 48),
        name="in_proj",
    )(x2, g_mix, w_in_bf, *rope)


CONV_PAD = 8
CONV_ROWS = 512


def _conv_kernel(pc_ref, w_ref, g_ref, o_ref, u_ref):
    zeros = jnp.zeros((CONV_PAD, CONV_WIDTH), f32)
    u_ref[pl.ds(0, CONV_PAD), :] = zeros
    u_ref[pl.ds(CONV_PAD + SEQ, CONV_PAD), :] = zeros
    for i in range(SEQ // CONV_ROWS):
        rows = pl.ds(i * CONV_ROWS, CONV_ROWS)
        u_ref[pl.ds(CONV_PAD + i * CONV_ROWS, CONV_ROWS), :] = (
            pc_ref[rows, CONV_WIDTH:2 * CONV_WIDTH] * pc_ref[rows, 2 * CONV_WIDTH:])
    w = w_ref[...]
    g = g_ref[...]
    for i in range(SEQ // CONV_ROWS):
        base = CONV_PAD + i * CONV_ROWS
        prev = u_ref[pl.ds(base - 1, CONV_ROWS), :]
        cur = u_ref[pl.ds(base, CONV_ROWS), :]
        nxt = u_ref[pl.ds(base + 1, CONV_ROWS), :]
        y = pc_ref[pl.ds(i * CONV_ROWS, CONV_ROWS), :CONV_WIDTH] * (
            w[0:1] * prev + w[1:2] * cur + w[2:3] * nxt)
        o_ref[pl.ds(i * CONV_ROWS, CONV_ROWS), :] = _rms(y, g).astype(bf16)


def _conv(pc, conv_w, g_conv):
    return pl.pallas_call(
        _conv_kernel,
        grid=(BATCH,),
        in_specs=[
            pl.BlockSpec((SEQ, 3 * CONV_WIDTH), lambda b: (b, 0)),
            pl.BlockSpec((3, CONV_WIDTH), lambda b: (0, 0)),
            pl.BlockSpec((1, CONV_WIDTH), lambda b: (0, 0)),
        ],
        out_specs=pl.BlockSpec((SEQ, CONV_WIDTH), lambda b: (b, 0)),
        out_shape=jax.ShapeDtypeStruct((TOKENS, CONV_WIDTH), bf16),
        scratch_shapes=[pltpu.VMEM((SEQ + 2 * CONV_PAD, CONV_WIDTH), f32)],
        compiler_params=_params(("arbitrary",), 48),
        name="conv",
    )(pc, conv_w, g_conv)


CAP_INTERIOR, CAP_FIRST, CAP_LAST, CAP_SINGLE = range(4)
MASK_OPEN = float(jnp.finfo(jnp.float32).max)
MID_DIL = DILATED_PATTERNS[1][1]
MID_LEN = SEQ // MID_DIL
assert [d for _, d in DILATED_PATTERNS] == [1, MID_DIL, MID_DIL * MID_DIL]
assert all(w // 2 // d == BAND_HALF for w, d in DILATED_PATTERNS)


def _attn_kernel(q_ref, k_ref, v_ref, o_ref, cap_ref, q4_ref, k4_ref, v4_ref,
                 qlo_ref, qhi_ref, kpad_ref, vlo_ref, vhi_ref,
                 num_ref, m_ref, l_ref):
    @pl.when(pl.program_id(1) == 0)
    def _():
        row = lax.broadcasted_iota(jnp.int32, (2 * Q_CHUNK, K_WIN), 0) % Q_CHUNK
        col = lax.broadcasted_iota(jnp.int32, (2 * Q_CHUNK, K_WIN), 1)
        band = (col - row >= 0) & (col - row <= 2 * BAND_HALF)
        not_before = col >= BAND_HALF
        not_after = col < K_WIN - BAND_HALF
        for idx, ok in ((CAP_INTERIOR, band), (CAP_FIRST, band & not_before),
                        (CAP_LAST, band & not_after), (CAP_SINGLE, band & not_before & not_after)):
            cap_ref[idx] = jnp.where(ok, MASK_OPEN, NEG_INF)

    lane_lo = lax.broadcasted_iota(jnp.int32, (Q_CHUNK, LANES), 1) < HEAD_DIM
    zero_pad = jnp.zeros((BAND_HALF, LANES), bf16)

    def zero_rows(start):
        for ref in (kpad_ref, vlo_ref, vhi_ref):
            ref[pl.ds(start, BAND_HALF), :] = zero_pad

    def stage(sources, src, q_row, pad_row, keep=None):
        qf, kf, vf = (ref[src, :] for ref in sources)
        if keep is not None:
            for ref, val in zip((q4_ref, k4_ref, v4_ref), (qf, kf, vf)):
                ref[keep, :] = val
        q_dst, p_dst = pl.ds(q_row, Q_CHUNK), pl.ds(pad_row, Q_CHUNK)
        qlo_ref[q_dst, :] = jnp.where(lane_lo, qf, 0.0).astype(bf16)
        qhi_ref[q_dst, :] = jnp.where(lane_lo, 0.0, qf).astype(bf16)
        kpad_ref[p_dst, :] = kf.astype(bf16)
        vlo_ref[p_dst, :] = jnp.where(lane_lo, vf, 0.0).astype(bf16)
        vhi_ref[p_dst, :] = jnp.where(lane_lo, 0.0, vf).astype(bf16)

    def chunk(pat, out_rows, q_row, pad_row, cap):
        q2 = jnp.concatenate([qlo_ref[pl.ds(q_row, Q_CHUNK), :], qhi_ref[pl.ds(q_row, Q_CHUNK), :]],
                             axis=0)
        s = lax.dot_general(q2, kpad_ref[pl.ds(pad_row, K_WIN), :], (((1,), (1,)), ((), ())),
                            preferred_element_type=f32)
        s = jnp.minimum(s, cap_ref[cap])
        m = jnp.max(s, axis=1, keepdims=True)
        p = jnp.exp(s - m)
        l = jnp.sum(p, axis=1, keepdims=True)
        pb = p.astype(bf16)
        num = (jnp.dot(pb[:Q_CHUNK], vlo_ref[pl.ds(pad_row, K_WIN), :], preferred_element_type=f32)
               + jnp.dot(pb[Q_CHUNK:], vhi_ref[pl.ds(pad_row, K_WIN), :], preferred_element_type=f32))
        num_ref[pat, out_rows, :] = num
        m_ref[pat, out_rows, :] = jnp.where(lane_lo, m[:Q_CHUNK], m[Q_CHUNK:])
        l_ref[pat, out_rows, :] = jnp.where(lane_lo, l[:Q_CHUNK], l[Q_CHUNK:])

    def run_pattern(pat, sources, classes, length, keep_f32=False):
        n_chunks = length // Q_CHUNK
        region = length + 2 * BAND_HALF

        def rows(first, stride, i):
            if stride == 1:
                return pl.ds(first + i * Q_CHUNK, Q_CHUNK)
            return pl.ds(first + stride * Q_CHUNK * i, Q_CHUNK, stride=stride)

        for u in range(len(classes)):
            zero_rows(u * region)
            zero_rows(u * region + BAND_HALF + length)
        for u, (src0, src_stride, _, _) in enumerate(classes):
            for i in range(n_chunks):
                keep = pl.ds(u * length + i * Q_CHUNK, Q_CHUNK) if keep_f32 else None
                stage(sources, rows(src0, src_stride, i), u * length + i * Q_CHUNK,
                      u * region + BAND_HALF + i * Q_CHUNK, keep)
        for u, (_, _, out0, out_stride) in enumerate(classes):
            for j in range(n_chunks):
                if n_chunks == 1:
                    cap = CAP_SINGLE
                else:
                    cap = CAP_FIRST if j == 0 else CAP_LAST if j == n_chunks - 1 else CAP_INTERIOR
                chunk(pat, rows(out0, out_stride, j), u * length + j * Q_CHUNK,
                      u * region + j * Q_CHUNK, cap)

    inputs = (q_ref, k_ref, v_ref)
    run_pattern(0, inputs, [(0, 1, 0, 1)], SEQ)
    run_pattern(1, inputs, [(r, MID_DIL, r * MID_LEN, 1) for r in range(MID_DIL)], MID_LEN,
                keep_f32=True)
    run_pattern(2, (q4_ref, k4_ref, v4_ref),
                [(r * MID_LEN + a, MID_DIL, r * MID_LEN + a, MID_DIL)
                 for r in range(MID_DIL) for a in range(MID_DIL)], MID_LEN // MID_DIL)

    def mix(blk, c):
        mid_rows = pl.ds(pl.multiple_of(blk * Q_CHUNK, Q_CHUNK), Q_CHUNK)
        per_class = MID_LEN // Q_CHUNK
        pos_rows = pl.ds(blk // per_class + MID_DIL * Q_CHUNK * (blk % per_class), Q_CHUNK,
                         stride=MID_DIL)
        rows = (pos_rows, mid_rows, mid_rows)
        ms = [m_ref[p, rows[p], :] for p in range(N_PATTERNS)]
        m_all = functools.reduce(jnp.maximum, ms)
        num = jnp.zeros((Q_CHUNK, LANES), f32)
        den = jnp.zeros((Q_CHUNK, LANES), f32)
        for p in range(N_PATTERNS):
            a = jnp.exp(ms[p] - m_all)
            den = den + a * l_ref[p, rows[p], :]
            num = num + a * num_ref[p, rows[p], :]
        o_ref[pos_rows, :] = num / den
        return c
    lax.fori_loop(0, SEQ // Q_CHUNK, mix, 0)


def _attention(qkv3):
    n_hp = N_HEADS // HEADS_PER_STEP
    blk = lambda off: pl.BlockSpec((None, SEQ, LANES), lambda b, h, off=off: (b, 0, off + h))
    pad_rows = max(SEQ + 2 * BAND_HALF * d for _, d in DILATED_PATTERNS)
    stats = pltpu.VMEM((N_PATTERNS, SEQ, LANES), f32)
    rows_f32 = pltpu.VMEM((SEQ, LANES), f32)
    return pl.pallas_call(
        _attn_kernel,
        grid=(BATCH, n_hp),
        in_specs=[blk(0), blk(n_hp), blk(2 * n_hp)],
        out_specs=pl.BlockSpec((None, SEQ, LANES), lambda b, h: (b, 0, h)),
        out_shape=jax.ShapeDtypeStruct((BATCH, SEQ, ATTN_WIDTH), f32),
        scratch_shapes=[
            pltpu.VMEM((4, 2 * Q_CHUNK, K_WIN), f32),
            rows_f32, rows_f32, rows_f32,
            pltpu.VMEM((SEQ, LANES), bf16), pltpu.VMEM((SEQ, LANES), bf16),
            pltpu.VMEM((pad_rows, LANES), bf16), pltpu.VMEM((pad_rows, LANES), bf16),
            pltpu.VMEM((pad_rows, LANES), bf16),
            stats, stats, stats,
        ],
        compiler_params=_params(("arbitrary", "arbitrary"), 48),
        name="attention",
    )(qkv3, qkv3, qkv3)


def _store_token_major(ref, x):
    for j in range(SLAB_ROWS):
        ref[pl.ds(j, x.shape[0], stride=SLAB_ROWS), :] = x[:, j * LANES:(j + 1) * LANES]


def _dot_nt(a, b):
    return lax.dot_general(a, b, (((1,), (1,)), ((), ())), preferred_element_type=f32)


def _out_proj_kernel(yc_ref, ya_ref, x_ref, w_ref, ga_ref, gf_ref, wr_ref,
                     x1_ref, h2_ref, lg_ref):
    ya = _rms(ya_ref[...], ga_ref[...]).astype(bf16)
    mix = (jnp.dot(yc_ref[...], w_ref[pl.ds(0, CONV_WIDTH), :], preferred_element_type=f32)
           + jnp.dot(ya, w_ref[pl.ds(CONV_WIDTH, ATTN_WIDTH), :], preferred_element_type=f32))
    x1 = x_ref[...] + mix
    _store_token_major(x1_ref, x1)
    h2 = _rms(x1, gf_ref[...])
    _store_token_major(h2_ref, h2)
    wr = wr_ref[...]
    wr_hi = wr.astype(bf16)
    wr_lo = (wr - wr_hi.astype(f32)).astype(bf16)
    h2_hi = h2.astype(bf16)
    h2_lo = (h2 - h2_hi.astype(f32)).astype(bf16)
    both = _dot_nt(jnp.concatenate([wr_hi, wr_lo], axis=0), h2_hi)
    lg_ref[...] = both[:N_EXPERTS] + both[N_EXPERTS:] + _dot_nt(wr_hi, h2_lo)


def _out_proj(yc, ya, x2, w_out_bf, g_attn, g_ffn, w_router_t):
    n = TOKENS // ROW_TILE
    slab = pl.BlockSpec((ROW_TILE * SLAB_ROWS, LANES), lambda i: (i, 0))
    slab_shape = jax.ShapeDtypeStruct((TOKENS * SLAB_ROWS, LANES), f32)
    return pl.pallas_call(
        _out_proj_kernel,
        grid=(n,),
        in_specs=[
            pl.BlockSpec((ROW_TILE, CONV_WIDTH), lambda i: (i, 0)),
            pl.BlockSpec((ROW_TILE, ATTN_WIDTH), lambda i: (i, 0)),
            pl.BlockSpec((ROW_TILE, D_MODEL), lambda i: (i, 0)),
            pl.BlockSpec((D_MODEL, D_MODEL), lambda i: (0, 0)),
            pl.BlockSpec((1, ATTN_WIDTH), lambda i: (0, 0)),
            pl.BlockSpec((1, D_MODEL), lambda i: (0, 0)),
            pl.BlockSpec((N_EXPERTS, D_MODEL), lambda i: (0, 0)),
        ],
        out_specs=[slab, slab, pl.BlockSpec((N_EXPERTS, ROW_TILE), lambda i: (0, i))],
        out_shape=[slab_shape, slab_shape, jax.ShapeDtypeStruct((N_EXPERTS, TOKENS), f32)],
        compiler_params=_params(("arbitrary",), 48),
        name="out_proj",
    )(yc, ya, x2, w_out_bf, g_attn, g_ffn, w_router_t)


N_LANE_BLOCKS = SEQ // LANES


def _excl_cumsum_tokens(x, tri):
    n = x.shape[0]
    stacked = jnp.concatenate([x[:, j * LANES:(j + 1) * LANES] for j in range(N_LANE_BLOCKS)], axis=0)
    within = jnp.dot(stacked.astype(bf16), tri, preferred_element_type=f32)
    totals = jnp.sum(stacked, axis=1, keepdims=True)
    out, offset = [], jnp.zeros((n, 1), f32)
    for j in range(N_LANE_BLOCKS):
        rows = slice(j * n, (j + 1) * n)
        out.append(within[rows] + offset)
        offset = offset + totals[rows]
    return jnp.concatenate(out, axis=1)


def _select_kernel(lg_ref, rank_ref, aff_ref):
    affs = []
    for b in range(BATCH):
        x = lg_ref[:, b * SEQ:(b + 1) * SEQ]
        e = jnp.exp(x - jnp.max(x, axis=0, keepdims=True))
        affs.append(e / jnp.sum(e, axis=0, keepdims=True))
    aff = jnp.concatenate(affs, axis=0)
    aff_ref[...] = aff
    thr = jnp.zeros((BATCH * N_EXPERTS, 1), jnp.int32)
    for bit in range(30, -1, -1):
        cand = thr | (1 << bit)
        hit = aff >= pltpu.bitcast(cand, f32)
        cnt = jnp.sum(jnp.where(hit, 1.0, 0.0), axis=1, keepdims=True)
        thr = jnp.where(cnt >= float(CAPACITY), cand, thr)
    above = aff >= pltpu.bitcast(thr + 1, f32)
    tie = (aff >= pltpu.bitcast(thr, f32)) & jnp.logical_not(above)
    need = float(CAPACITY) - jnp.sum(jnp.where(above, 1.0, 0.0), axis=1, keepdims=True)
    tri = jnp.where(lax.broadcasted_iota(jnp.int32, (LANES, LANES), 0)
                    < lax.broadcasted_iota(jnp.int32, (LANES, LANES), 1), 1.0, 0.0).astype(bf16)
    tie_rank = _excl_cumsum_tokens(jnp.where(tie, 1.0, 0.0), tri)
    sel = above | (tie & (tie_rank < need))
    rank_ref[...] = jnp.where(sel, _excl_cumsum_tokens(jnp.where(sel, 1.0, 0.0), tri), -1.0)


def _select(logits_t):
    shape = jax.ShapeDtypeStruct((BATCH * N_EXPERTS, SEQ), f32)
    blk = pl.BlockSpec((BATCH * N_EXPERTS, SEQ), lambda i: (0, 0))
    return pl.pallas_call(
        _select_kernel,
        grid=(1,),
        in_specs=[pl.BlockSpec((N_EXPERTS, TOKENS), lambda i: (0, 0))],
        out_specs=[blk, blk],
        out_shape=[shape, shape],
        compiler_params=_params(("arbitrary",), 40),
        name="select",
    )(logits_t)


def _route_kernel(rank_ref, aff_ref, idx_ref, gate_ref):
    rank = rank_ref[...]
    aff = aff_ref[...]
    high = jnp.floor(rank * (1.0 / SLOT_SPLIT))
    low = rank - high * SLOT_SPLIT
    digit = lax.broadcasted_iota(jnp.int32, (SLOT_SPLIT, SEQ), 0).astype(f32)
    tok = lax.broadcasted_iota(jnp.int32, (1, SEQ), 1)
    g1 = aff.astype(bf16).astype(f32)
    g2 = (aff - g1).astype(bf16).astype(f32)
    g3 = aff - g1 - g2
    per_expert = lambda v: jnp.broadcast_to(v.astype(f32), (N_EXPERTS, SEQ))
    payloads = [per_expert(tok // LANES), per_expert(tok % LANES), g1, g2, g3]
    high_rows = jnp.concatenate(
        [jnp.where(high[e:e + 1] == digit, 1.0, 0.0) for e in range(N_EXPERTS)], axis=0).astype(bf16)
    low_hit = [low[e:e + 1] == digit for e in range(N_EXPERTS)]
    n_rows = N_EXPERTS * SLOT_SPLIT
    same_expert = (lax.broadcasted_iota(jnp.int32, (n_rows, n_rows), 0) // SLOT_SPLIT
                   == lax.broadcasted_iota(jnp.int32, (n_rows, n_rows), 1) // SLOT_SPLIT)
    fold = jnp.where(lax.broadcasted_iota(jnp.int32, (n_rows, SLOT_SPLIT), 0) % SLOT_SPLIT
                     == lax.broadcasted_iota(jnp.int32, (n_rows, SLOT_SPLIT), 1), 1.0, 0.0).astype(bf16)
    folded = []
    for val in payloads:
        rows = jnp.concatenate(
            [jnp.where(low_hit[e], val[e:e + 1], 0.0) for e in range(N_EXPERTS)], axis=0).astype(bf16)
        picked = jnp.where(same_expert, _dot_nt(high_rows, rows), 0.0).astype(bf16)
        folded.append(jnp.dot(picked, fold, preferred_element_type=f32))
    idx_ref[...] = (folded[0] * float(LANES) + folded[1]).astype(jnp.int32)
    gate_ref[...] = folded[2] + folded[3] + folded[4]


def _route(rank, aff):
    n_rows = N_EXPERTS * SLOT_SPLIT
    blk = pl.BlockSpec((None, n_rows, CAPACITY // SLOT_SPLIT), lambda b: (b, 0, 0))
    shape = (BATCH, n_rows, CAPACITY // SLOT_SPLIT)
    per_seq = pl.BlockSpec((N_EXPERTS, SEQ), lambda b: (b, 0))
    return pl.pallas_call(
        _route_kernel,
        grid=(BATCH,),
        in_specs=[per_seq, per_seq],
        out_specs=[blk, blk],
        out_shape=[jax.ShapeDtypeStruct(shape, jnp.int32), jax.ShapeDtypeStruct(shape, f32)],
        compiler_params=_params(("arbitrary",), 40),
        name="route",
    )(rank, aff)


def _gather_kernel(idx_ref, h2_ref, xe_ref, tile_ref):
    def expert(e, c):
        base = e * CAPACITY
        for slot in range(CAPACITY):
            tok = idx_ref[0, base + slot]
            slab = h2_ref[pl.ds(pl.multiple_of(tok * SLAB_ROWS, SLAB_ROWS), SLAB_ROWS), :]
            tile_ref[pl.ds(slot, SLAB_ROWS, stride=TILE_PITCH), :] = slab
        for j in range(SLAB_ROWS):
            xe_ref[e, :, j * LANES:(j + 1) * LANES] = (
                tile_ref[pl.ds(j * TILE_PITCH, CAPACITY), :].astype(bf16))
        return c
    lax.fori_loop(0, N_EXPERTS, expert, 0)


def _gather(idx, h2_slab):
    return pl.pallas_call(
        _gather_kernel,
        grid=(BATCH,),
        in_specs=[
            pl.BlockSpec((None, 1, N_EXPERTS * CAPACITY), lambda b: (b, 0, 0),
                         memory_space=pltpu.SMEM),
            pl.BlockSpec((SEQ * SLAB_ROWS, LANES), lambda b: (b, 0)),
        ],
        out_specs=pl.BlockSpec((N_EXPERTS, None, CAPACITY, D_MODEL), lambda b: (0, b, 0, 0)),
        out_shape=jax.ShapeDtypeStruct((N_EXPERTS, BATCH, CAPACITY, D_MODEL), bf16),
        scratch_shapes=[pltpu.VMEM((SLAB_ROWS * TILE_PITCH, LANES), f32)],
        compiler_params=_params(("arbitrary",), 48),
        name="gather",
    )(idx, h2_slab)


def _ffn_kernel(xe_ref, wg_ref, wu_ref, wd_ref, y_ref, acc_ref):
    f = pl.program_id(1)

    last = D_FF // FF_TILE - 1

    def partial_out():
        xe = xe_ref[...]
        a = jnp.dot(xe, wg_ref[...].astype(bf16), preferred_element_type=f32)
        u = jnp.dot(xe, wu_ref[...].astype(bf16), preferred_element_type=f32)
        hidden = (a * (1.0 / (1.0 + jnp.exp(-a))) * u).astype(bf16)
        return jnp.dot(hidden, wd_ref[...].astype(bf16), preferred_element_type=f32)

    @pl.when(f == 0)
    def _():
        acc_ref[...] = partial_out()

    @pl.when((f > 0) & (f < last))
    def _():
        acc_ref[...] += partial_out()

    @pl.when(f == last)
    def _():
        y_ref[...] = (acc_ref[...] + partial_out()).astype(bf16)


def _ffn(xe, w_gate, w_up, w_down):
    rows = BATCH * CAPACITY
    return pl.pallas_call(
        _ffn_kernel,
        grid=(N_EXPERTS, D_FF // FF_TILE),
        in_specs=[
            pl.BlockSpec((None, rows, D_MODEL), lambda e, f: (e, 0, 0)),
            pl.BlockSpec((None, D_MODEL, FF_TILE), lambda e, f: (e, 0, f)),
            pl.BlockSpec((None, D_MODEL, FF_TILE), lambda e, f: (e, 0, f)),
            pl.BlockSpec((None, FF_TILE, D_MODEL), lambda e, f: (e, f, 0)),
        ],
        out_specs=pl.BlockSpec((None, rows, D_MODEL), lambda e, f: (e, 0, 0)),
        out_shape=jax.ShapeDtypeStruct((N_EXPERTS, rows, D_MODEL), bf16),
        scratch_shapes=[pltpu.VMEM((rows, D_MODEL), f32)],
        compiler_params=_params(("arbitrary", "arbitrary"), 56),
        name="ffn",
    )(xe, w_gate, w_up, w_down)


SCATTER_UNROLL = 8
COPY_ROWS = 2048
NORM_TOKENS = 256


def _combine_kernel(idx_ref, gate_ref, y_ref, x1_ref, g_ref, o_ref, acc_ref, tile_ref):
    step = pl.program_id(1)

    @pl.when(step == 0)
    def _():
        def copy(i, c):
            rows = pl.ds(pl.multiple_of(i * COPY_ROWS, COPY_ROWS), COPY_ROWS)
            acc_ref[rows, :] = x1_ref[rows, :]
            return c
        lax.fori_loop(0, SEQ * SLAB_ROWS // COPY_ROWS, copy, 0)

    def expert(k, c):
        base = (step * EXPERTS_PER_STEP + k) * CAPACITY
        for j in range(SLAB_ROWS):
            tile_ref[pl.ds(j * TILE_PITCH, CAPACITY), :] = (
                y_ref[k, :, j * LANES:(j + 1) * LANES].astype(f32))
        for first in range(0, CAPACITY, SCATTER_UNROLL):
            updates = []
            for slot in range(first, first + SCATTER_UNROLL):
                tok = idx_ref[0, base + slot]
                gate = gate_ref[0, base + slot]
                rows = pl.ds(pl.multiple_of(tok * SLAB_ROWS, SLAB_ROWS), SLAB_ROWS)
                contrib = tile_ref[pl.ds(slot, SLAB_ROWS, stride=TILE_PITCH), :]
                updates.append((rows, acc_ref[rows, :] + gate * contrib))
            for rows, value in updates:
                acc_ref[rows, :] = value
        return c
    lax.fori_loop(0, EXPERTS_PER_STEP, expert, 0)

    @pl.when(step == N_EXPERTS // EXPERTS_PER_STEP - 1)
    def _():
        def norm(i, c):
            first = i * (NORM_TOKENS * SLAB_ROWS)
            chunks = [acc_ref[pl.ds(first + j, NORM_TOKENS, stride=SLAB_ROWS), :]
                      for j in range(SLAB_ROWS)]
            squares = functools.reduce(lambda a, b: a + b, [ch * ch for ch in chunks])
            inv = lax.rsqrt(jnp.sum(squares, axis=1, keepdims=True) * (1.0 / D_MODEL) + RMS_EPS)
            rows = pl.ds(pl.multiple_of(i * NORM_TOKENS, NORM_TOKENS), NORM_TOKENS)
            for j in range(SLAB_ROWS):
                cols = slice(j * LANES, (j + 1) * LANES)
                o_ref[rows, cols] = chunks[j] * inv * g_ref[:, cols]
            return c
        lax.fori_loop(0, SEQ // NORM_TOKENS, norm, 0)


def _combine(idx, gate, y, x1_slab, g_final):
    smem = lambda: pl.BlockSpec((None, 1, N_EXPERTS * CAPACITY), lambda b, s: (b, 0, 0),
                                memory_space=pltpu.SMEM)
    return pl.pallas_call(
        _combine_kernel,
        grid=(BATCH, N_EXPERTS // EXPERTS_PER_STEP),
        in_specs=[
            smem(), smem(),
            pl.BlockSpec((EXPERTS_PER_STEP, None, CAPACITY, D_MODEL), lambda b, s: (s, b, 0, 0)),
            pl.BlockSpec((SEQ * SLAB_ROWS, LANES), lambda b, s: (b, 0)),
            pl.BlockSpec((1, D_MODEL), lambda b, s: (0, 0)),
        ],
        out_specs=pl.BlockSpec((SEQ, D_MODEL), lambda b, s: (b, 0)),
        out_shape=jax.ShapeDtypeStruct((TOKENS, D_MODEL), f32),
        scratch_shapes=[pltpu.VMEM((SEQ * SLAB_ROWS, LANES), f32),
                        pltpu.VMEM((SLAB_ROWS * TILE_PITCH, LANES), f32)],
        compiler_params=_params(("arbitrary", "arbitrary"), 56),
        name="combine",
    )(idx, gate, y, x1_slab, g_final)


def _rope_freqs():
    inv_freq = ROPE_THETA ** (-jnp.arange(ROT_HALF, dtype=f32) / ROT_HALF)
    return inv_freq.reshape(ROT_HALF, 1)


def kernel(x, positions, g_mix, w_in, conv_w, g_conv_out, g_attn_out, w_out, g_ffn, w_router,
           w_gate, w_up, w_down, g_final):
    x2 = x.reshape(TOKENS, D_MODEL)
    rope = _rope_tables(positions.reshape(BATCH, 1, SEQ), _rope_freqs())
    pc, qkv = _in_proj(x2, g_mix[0].reshape(1, D_MODEL), w_in[0].astype(bf16), rope)
    yc = _conv(pc, conv_w[0], g_conv_out[0].reshape(1, CONV_WIDTH))
    ya = _attention(qkv.reshape(BATCH, SEQ, 3 * ATTN_WIDTH))
    x1, h2, logits_t = _out_proj(yc, ya.reshape(TOKENS, ATTN_WIDTH), x2, w_out[0].astype(bf16),
                                 g_attn_out[0].reshape(1, ATTN_WIDTH), g_ffn[0].reshape(1, D_MODEL),
                                 w_router[0].T)
    idx, gate = _route(*_select(logits_t))
    idx = idx.reshape(BATCH, 1, N_EXPERTS * CAPACITY)
    gate = gate.reshape(BATCH, 1, N_EXPERTS * CAPACITY)
    xe = _gather(idx, h2)
    y = _ffn(xe.reshape(N_EXPERTS, BATCH * CAPACITY, D_MODEL), w_gate[0], w_up[0], w_down[0])
    out = _combine(idx, gate, y.reshape(N_EXPERTS, BATCH, CAPACITY, D_MODEL), x1,
                   g_final.reshape(1, D_MODEL))
    return out.reshape(BATCH, SEQ, D_MODEL)
```

```python
import functools

import jax
import jax.numpy as jnp
from jax import lax
from jax.experimental import pallas as pl
from jax.experimental.pallas import tpu as pltpu

D_MODEL = 1024
BATCH = 8
SEQ = 2048
TOKENS = BATCH * SEQ
CONV_WIDTH = 512
ATTN_WIDTH = 512
HEAD_DIM = 64
N_HEADS = 8
DILATED_PATTERNS = ((128, 1), (512, 4), (2048, 16))
N_PATTERNS = len(DILATED_PATTERNS)
ROPE_THETA = 500000.0
ROT_DIM = 16
ROT_HALF = ROT_DIM // 2
N_EXPERTS = 16
CAPACITY = 2 * SEQ // N_EXPERTS
D_FF = 2 * D_MODEL
PROJ_WIDTH = 3 * CONV_WIDTH + 3 * ATTN_WIDTH
RMS_EPS = 1e-6
NEG_INF = -1e30

LANES = 128
ROW_TILE = 512
FF_TILE = 512
BAND_HALF = 64
Q_CHUNK = 2 * BAND_HALF
K_WIN = 4 * BAND_HALF
HEADS_PER_STEP = LANES // HEAD_DIM
SUBLANES = 8
SLAB_ROWS = D_MODEL // LANES
TILE_PITCH = CAPACITY + SUBLANES
SLOT_SPLIT = 16
EXPERTS_PER_STEP = 4
MIB = 1024 * 1024

f32 = jnp.float32
bf16 = jnp.bfloat16


def _params(semantics, vmem_mib):
    return pltpu.CompilerParams(dimension_semantics=semantics, vmem_limit_bytes=vmem_mib * MIB)


def _rms(x, g):
    return x * lax.rsqrt(jnp.mean(x * x, axis=-1, keepdims=True) + RMS_EPS) * g


def _rope_table_kernel(pos_ref, invf_ref, cos_ref, s1_ref, s2_ref):
    ang = invf_ref[...] * pos_ref[...].astype(f32)
    trig = jnp.concatenate([jnp.cos(ang), jnp.sin(ang)], axis=0)
    lane = lax.broadcasted_iota(jnp.int32, (ROT_DIM, 3 * LANES), 1)
    row = lax.broadcasted_iota(jnp.int32, (ROT_DIM, 3 * LANES), 0)
    freq = row % ROT_HALF
    is_sin = row >= ROT_HALF
    at_lo = lane % HEAD_DIM == freq
    at_hi = lane % HEAD_DIM == freq + ROT_HALF
    table = lane // LANES
    place = (jnp.where((table == 0) & jnp.logical_not(is_sin) & (at_lo | at_hi), 1.0, 0.0)
             + jnp.where((table == 1) & is_sin & at_hi, 1.0, 0.0)
             - jnp.where((table == 2) & is_sin & at_lo, 1.0, 0.0)).astype(bf16)
    spread = jnp.zeros((SEQ, 3 * LANES), f32)
    rest = trig
    for _ in range(3):
        piece = rest.astype(bf16)
        rest = rest - piece.astype(f32)
        spread = spread + lax.dot_general(piece, place, (((0,), (0,)), ((), ())),
                                          preferred_element_type=f32)
    unrotated = jnp.where(
        lax.broadcasted_iota(jnp.int32, (1, LANES), 1) % HEAD_DIM >= ROT_DIM, 1.0, 0.0)
    cos_ref[...] = spread[:, :LANES] + unrotated
    s1_ref[...] = spread[:, LANES:2 * LANES]
    s2_ref[...] = spread[:, 2 * LANES:]


def _rope_tables(pos_row, invf_col):
    blk = pl.BlockSpec((SEQ, LANES), lambda b: (b, 0))
    return pl.pallas_call(
        _rope_table_kernel,
        grid=(BATCH,),
        in_specs=[
            pl.BlockSpec((None, 1, SEQ), lambda b: (b, 0, 0)),
            pl.BlockSpec((ROT_HALF, 1), lambda b: (0, 0)),
        ],
        out_specs=[blk, blk, blk],
        out_shape=[jax.ShapeDtypeStruct((TOKENS, LANES), f32)] * 3,
        compiler_params=_params(("arbitrary",), 32),
        name="rope_tables",
    )(pos_row, invf_col)


EDGE_U_FIRST, EDGE_U_LAST, EDGE_B_FIRST, EDGE_B_LAST = (k * SUBLANES for k in range(4))
EDGE_ROWS = 4 * SUBLANES


def _in_proj_kernel(x_ref, g_ref, w_ref, cw_ref, cos_ref, s1_ref, s2_ref, yc_ref, edge_ref, qkv_ref):
    h = _rms(x_ref[...], g_ref[...]).astype(bf16)
    p = jnp.dot(h, w_ref[...], preferred_element_type=f32)
    gate_b = p[:, :CONV_WIDTH]
    u = p[:, CONV_WIDTH:2 * CONV_WIDTH] * p[:, 2 * CONV_WIDTH:3 * CONV_WIDTH]
    cw = cw_ref[...]
    row = lax.broadcasted_iota(jnp.int32, (ROW_TILE, CONV_WIDTH), 0)
    before = jnp.where(row == 0, 0.0, pltpu.roll(u, 1, 0))
    after = jnp.where(row == ROW_TILE - 1, 0.0, pltpu.roll(u, ROW_TILE - 1, 0))
    yc_ref[...] = gate_b * (cw[0:1] * before + cw[1:2] * u + cw[2:3] * after)
    last = ROW_TILE - SUBLANES
    edge_ref[pl.ds(EDGE_U_FIRST, SUBLANES), :] = u[:SUBLANES]
    edge_ref[pl.ds(EDGE_U_LAST, SUBLANES), :] = u[last:]
    edge_ref[pl.ds(EDGE_B_FIRST, SUBLANES), :] = gate_b[:SUBLANES] * cw[0:1]
    edge_ref[pl.ds(EDGE_B_LAST, SUBLANES), :] = gate_b[last:] * cw[2:3]
    cs, s1, s2 = cos_ref[...], s1_ref[...], s2_ref[...]
    for blk in range(2 * ATTN_WIDTH // LANES):
        x = p[:, 3 * CONV_WIDTH + blk * LANES:3 * CONV_WIDTH + (blk + 1) * LANES]
        y = x * cs + pltpu.roll(x, ROT_HALF, 1) * s1 + pltpu.roll(x, LANES - ROT_HALF, 1) * s2
        if blk < ATTN_WIDTH // LANES:
            y = y * HEAD_DIM ** -0.5
        qkv_ref[:, blk * LANES:(blk + 1) * LANES] = y
    qkv_ref[:, 2 * ATTN_WIDTH:] = p[:, 3 * CONV_WIDTH + 2 * ATTN_WIDTH:]


def _in_proj(x2, g_mix, w_in_bf, conv_w, rope):
    n = TOKENS // ROW_TILE
    table = pl.BlockSpec((ROW_TILE, LANES), lambda i: (i, 0))
    return pl.pallas_call(
        _in_proj_kernel,
        grid=(n,),
        in_specs=[
            pl.BlockSpec((ROW_TILE, D_MODEL), lambda i: (i, 0)),
            pl.BlockSpec((1, D_MODEL), lambda i: (0, 0)),
            pl.BlockSpec((D_MODEL, PROJ_WIDTH), lambda i: (0, 0)),
            pl.BlockSpec((3, CONV_WIDTH), lambda i: (0, 0)),
            table, table, table,
        ],
        out_specs=[
            pl.BlockSpec((ROW_TILE, CONV_WIDTH), lambda i: (i, 0)),
            pl.BlockSpec((EDGE_ROWS, CONV_WIDTH), lambda i: (i, 0)),
            pl.BlockSpec((ROW_TILE, 3 * ATTN_WIDTH), lambda i: (i, 0)),
        ],
        out_shape=[
            jax.ShapeDtypeStruct((TOKENS, CONV_WIDTH), f32),
            jax.ShapeDtypeStruct((n * EDGE_ROWS, CONV_WIDTH), f32),
            jax.ShapeDtypeStruct((TOKENS, 3 * ATTN_WIDTH), f32),
        ],
        compiler_params=_params(("arbitrary",), 48),
        name="in_proj",
    )(x2, g_mix, w_in_bf, conv_w, *rope)


CAP_INTERIOR, CAP_FIRST, CAP_LAST, CAP_SINGLE = range(4)
MASK_OPEN = float(jnp.finfo(jnp.float32).max)
MID_DIL = DILATED_PATTERNS[1][1]
MID_LEN = SEQ // MID_DIL
assert [d for _, d in DILATED_PATTERNS] == [1, MID_DIL, MID_DIL * MID_DIL]
assert all(w // 2 // d == BAND_HALF for w, d in DILATED_PATTERNS)


def _attn_kernel(q_ref, k_ref, v_ref, o_ref, cap_ref, q4_ref, k4_ref, v4_ref,
                 qlo_ref, qhi_ref, kpad_ref, vlo_ref, vhi_ref,
                 num_ref, m_ref, l_ref):
    @pl.when(pl.program_id(1) == 0)
    def _():
        row = lax.broadcasted_iota(jnp.int32, (2 * Q_CHUNK, K_WIN), 0) % Q_CHUNK
        col = lax.broadcasted_iota(jnp.int32, (2 * Q_CHUNK, K_WIN), 1)
        band = (col - row >= 0) & (col - row <= 2 * BAND_HALF)
        not_before = col >= BAND_HALF
        not_after = col < K_WIN - BAND_HALF
        for idx, ok in ((CAP_INTERIOR, band), (CAP_FIRST, band & not_before),
                        (CAP_LAST, band & not_after), (CAP_SINGLE, band & not_before & not_after)):
            cap_ref[idx] = jnp.where(ok, MASK_OPEN, NEG_INF)

    lane_lo = lax.broadcasted_iota(jnp.int32, (Q_CHUNK, LANES), 1) < HEAD_DIM
    zero_pad = jnp.zeros((BAND_HALF, LANES), bf16)

    def zero_rows(start):
        for ref in (kpad_ref, vlo_ref, vhi_ref):
            ref[pl.ds(start, BAND_HALF), :] = zero_pad

    def stage(sources, src, q_row, pad_row, keep=None):
        qf, kf, vf = (ref[src, :] for ref in sources)
        if keep is not None:
            for ref, val in zip((q4_ref, k4_ref, v4_ref), (qf, kf, vf)):
                ref[keep, :] = val
        q_dst, p_dst = pl.ds(q_row, Q_CHUNK), pl.ds(pad_row, Q_CHUNK)
        qlo_ref[q_dst, :] = jnp.where(lane_lo, qf, 0.0).astype(bf16)
        qhi_ref[q_dst, :] = jnp.where(lane_lo, 0.0, qf).astype(bf16)
        kpad_ref[p_dst, :] = kf.astype(bf16)
        vlo_ref[p_dst, :] = jnp.where(lane_lo, vf, 0.0).astype(bf16)
        vhi_ref[p_dst, :] = jnp.where(lane_lo, 0.0, vf).astype(bf16)

    def chunk(pat, out_rows, q_row, pad_row, cap):
        q2 = jnp.concatenate([qlo_ref[pl.ds(q_row, Q_CHUNK), :], qhi_ref[pl.ds(q_row, Q_CHUNK), :]],
                             axis=0)
        s = lax.dot_general(q2, kpad_ref[pl.ds(pad_row, K_WIN), :], (((1,), (1,)), ((), ())),
                            preferred_element_type=f32)
        s = jnp.minimum(s, cap_ref[cap])
        m = jnp.max(s, axis=1, keepdims=True)
        p = jnp.exp(s - m)
        l = jnp.sum(p, axis=1, keepdims=True)
        pb = p.astype(bf16)
        num = (jnp.dot(pb[:Q_CHUNK], vlo_ref[pl.ds(pad_row, K_WIN), :], preferred_element_type=f32)
               + jnp.dot(pb[Q_CHUNK:], vhi_ref[pl.ds(pad_row, K_WIN), :], preferred_element_type=f32))
        num_ref[pat, out_rows, :] = num
        m_ref[pat, out_rows, :] = jnp.where(lane_lo, m[:Q_CHUNK], m[Q_CHUNK:])
        l_ref[pat, out_rows, :] = jnp.where(lane_lo, l[:Q_CHUNK], l[Q_CHUNK:])

    def run_pattern(pat, sources, classes, length, keep_f32=False):
        n_chunks = length // Q_CHUNK
        region = length + 2 * BAND_HALF

        def rows(first, stride, i):
            if stride == 1:
                return pl.ds(first + i * Q_CHUNK, Q_CHUNK)
            return pl.ds(first + stride * Q_CHUNK * i, Q_CHUNK, stride=stride)

        for u in range(len(classes)):
            zero_rows(u * region)
            zero_rows(u * region + BAND_HALF + length)
        for u, (src0, src_stride, _, _) in enumerate(classes):
            for i in range(n_chunks):
                keep = pl.ds(u * length + i * Q_CHUNK, Q_CHUNK) if keep_f32 else None
                stage(sources, rows(src0, src_stride, i), u * length + i * Q_CHUNK,
                      u * region + BAND_HALF + i * Q_CHUNK, keep)
        for u, (_, _, out0, out_stride) in enumerate(classes):
            for j in range(n_chunks):
                if n_chunks == 1:
                    cap = CAP_SINGLE
                else:
                    cap = CAP_FIRST if j == 0 else CAP_LAST if j == n_chunks - 1 else CAP_INTERIOR
                chunk(pat, rows(out0, out_stride, j), u * length + j * Q_CHUNK,
                      u * region + j * Q_CHUNK, cap)

    inputs = (q_ref, k_ref, v_ref)
    run_pattern(0, inputs, [(0, 1, 0, 1)], SEQ)
    run_pattern(1, inputs, [(r, MID_DIL, r * MID_LEN, 1) for r in range(MID_DIL)], MID_LEN,
                keep_f32=True)
    run_pattern(2, (q4_ref, k4_ref, v4_ref),
                [(r * MID_LEN + a, MID_DIL, r * MID_LEN + a, MID_DIL)
                 for r in range(MID_DIL) for a in range(MID_DIL)], MID_LEN // MID_DIL)

    def mix(blk, c):
        mid_rows = pl.ds(pl.multiple_of(blk * Q_CHUNK, Q_CHUNK), Q_CHUNK)
        per_class = MID_LEN // Q_CHUNK
        pos_rows = pl.ds(blk // per_class + MID_DIL * Q_CHUNK * (blk % per_class), Q_CHUNK,
                         stride=MID_DIL)
        rows = (pos_rows, mid_rows, mid_rows)
        ms = [m_ref[p, rows[p], :] for p in range(N_PATTERNS)]
        m_all = functools.reduce(jnp.maximum, ms)
        num = jnp.zeros((Q_CHUNK, LANES), f32)
        den = jnp.zeros((Q_CHUNK, LANES), f32)
        for p in range(N_PATTERNS):
            a = jnp.exp(ms[p] - m_all)
            den = den + a * l_ref[p, rows[p], :]
            num = num + a * num_ref[p, rows[p], :]
        o_ref[pos_rows, :] = num / den
        return c
    lax.fori_loop(0, SEQ // Q_CHUNK, mix, 0)


def _attention(qkv3):
    n_hp = N_HEADS // HEADS_PER_STEP
    blk = lambda off: pl.BlockSpec((None, SEQ, LANES), lambda b, h, off=off: (b, 0, off + h))
    pad_rows = max(SEQ + 2 * BAND_HALF * d for _, d in DILATED_PATTERNS)
    stats = pltpu.VMEM((N_PATTERNS, SEQ, LANES), f32)
    rows_f32 = pltpu.VMEM((SEQ, LANES), f32)
    return pl.pallas_call(
        _attn_kernel,
        grid=(BATCH, n_hp),
        in_specs=[blk(0), blk(n_hp), blk(2 * n_hp)],
        out_specs=pl.BlockSpec((None, SEQ, LANES), lambda b, h: (b, 0, h)),
        out_shape=jax.ShapeDtypeStruct((BATCH, SEQ, ATTN_WIDTH), f32),
        scratch_shapes=[
            pltpu.VMEM((4, 2 * Q_CHUNK, K_WIN), f32),
            rows_f32, rows_f32, rows_f32,
            pltpu.VMEM((SEQ, LANES), bf16), pltpu.VMEM((SEQ, LANES), bf16),
            pltpu.VMEM((pad_rows, LANES), bf16), pltpu.VMEM((pad_rows, LANES), bf16),
            pltpu.VMEM((pad_rows, LANES), bf16),
            stats, stats, stats,
        ],
        compiler_params=_params(("arbitrary", "arbitrary"), 48),
        name="attention",
    )(qkv3, qkv3, qkv3)


def _store_token_major(ref, x):
    for j in range(SLAB_ROWS):
        ref[pl.ds(j, x.shape[0], stride=SLAB_ROWS), :] = x[:, j * LANES:(j + 1) * LANES]


def _dot_nt(a, b):
    return lax.dot_general(a, b, (((1,), (1,)), ((), ())), preferred_element_type=f32)


def _out_proj_kernel(yc_ref, edge_ref, edge_prev_ref, edge_next_ref, ya_ref, x_ref, w_ref,
                     gc_ref, ga_ref, gf_ref, wr_ref, x1_ref, h2_ref, lg_ref):
    tile = pl.program_id(0) % (SEQ // ROW_TILE)
    from_prev = (edge_ref[pl.ds(EDGE_B_FIRST, 1), :]
                 * edge_prev_ref[pl.ds(EDGE_U_LAST + SUBLANES - 1, 1), :])
    from_next = (edge_ref[pl.ds(EDGE_B_LAST + SUBLANES - 1, 1), :]
                 * edge_next_ref[pl.ds(EDGE_U_FIRST, 1), :])
    from_prev = jnp.where(tile == 0, 0.0, from_prev)
    from_next = jnp.where(tile == SEQ // ROW_TILE - 1, 0.0, from_next)
    row = lax.broadcasted_iota(jnp.int32, (ROW_TILE, CONV_WIDTH), 0)
    y_conv = (yc_ref[...] + jnp.where(row == 0, from_prev, 0.0)
              + jnp.where(row == ROW_TILE - 1, from_next, 0.0))
    yc = _rms(y_conv, gc_ref[...]).astype(bf16)
    ya = _rms(ya_ref[...], ga_ref[...]).astype(bf16)
    mix = (jnp.dot(yc, w_ref[pl.ds(0, CONV_WIDTH), :], preferred_element_type=f32)
           + jnp.dot(ya, w_ref[pl.ds(CONV_WIDTH, ATTN_WIDTH), :], preferred_element_type=f32))
    x1 = x_ref[...] + mix
    _store_token_major(x1_ref, x1)
    h2 = _rms(x1, gf_ref[...])
    _store_token_major(h2_ref, h2)
    wr = wr_ref[...]
    wr_hi = wr.astype(bf16)
    wr_lo = (wr - wr_hi.astype(f32)).astype(bf16)
    h2_hi = h2.astype(bf16)
    h2_lo = (h2 - h2_hi.astype(f32)).astype(bf16)
    both = _dot_nt(jnp.concatenate([wr_hi, wr_lo], axis=0), h2_hi)
    lg_ref[...] = both[:N_EXPERTS] + both[N_EXPERTS:] + _dot_nt(wr_hi, h2_lo)


def _out_proj(yc, edge, ya, x2, w_out_bf, g_conv, g_attn, g_ffn, w_router_t):
    n = TOKENS // ROW_TILE
    slab = pl.BlockSpec((ROW_TILE * SLAB_ROWS, LANES), lambda i: (i, 0))
    slab_shape = jax.ShapeDtypeStruct((TOKENS * SLAB_ROWS, LANES), f32)
    return pl.pallas_call(
        _out_proj_kernel,
        grid=(n,),
        in_specs=[
            pl.BlockSpec((ROW_TILE, CONV_WIDTH), lambda i: (i, 0)),
            pl.BlockSpec((EDGE_ROWS, CONV_WIDTH), lambda i: (i, 0)),
            pl.BlockSpec((EDGE_ROWS, CONV_WIDTH), lambda i: (jnp.maximum(i - 1, 0), 0)),
            pl.BlockSpec((EDGE_ROWS, CONV_WIDTH), lambda i: (jnp.minimum(i + 1, n - 1), 0)),
            pl.BlockSpec((ROW_TILE, ATTN_WIDTH), lambda i: (i, 0)),
            pl.BlockSpec((ROW_TILE, D_MODEL), lambda i: (i, 0)),
            pl.BlockSpec((D_MODEL, D_MODEL), lambda i: (0, 0)),
            pl.BlockSpec((1, CONV_WIDTH), lambda i: (0, 0)),
            pl.BlockSpec((1, ATTN_WIDTH), lambda i: (0, 0)),
            pl.BlockSpec((1, D_MODEL), lambda i: (0, 0)),
            pl.BlockSpec((N_EXPERTS, D_MODEL), lambda i: (0, 0)),
        ],
        out_specs=[slab, slab, pl.BlockSpec((N_EXPERTS, ROW_TILE), lambda i: (0, i))],
        out_shape=[slab_shape, slab_shape, jax.ShapeDtypeStruct((N_EXPERTS, TOKENS), f32)],
        compiler_params=_params(("arbitrary",), 48),
        name="out_proj",
    )(yc, edge, edge, edge, ya, x2, w_out_bf, g_conv, g_attn, g_ffn, w_router_t)


N_LANE_BLOCKS = SEQ // LANES


def _excl_cumsum_tokens(x, tri):
    n = x.shape[0]
    stacked = jnp.concatenate([x[:, j * LANES:(j + 1) * LANES] for j in range(N_LANE_BLOCKS)], axis=0)
    within = jnp.dot(stacked.astype(bf16), tri, preferred_element_type=f32)
    totals = jnp.sum(stacked, axis=1, keepdims=True)
    out, offset = [], jnp.zeros((n, 1), f32)
    for j in range(N_LANE_BLOCKS):
        rows = slice(j * n, (j + 1) * n)
        out.append(within[rows] + offset)
        offset = offset + totals[rows]
    return jnp.concatenate(out, axis=1)


def _select_kernel(lg_ref, rank_ref, aff_ref):
    affs = []
    for b in range(BATCH):
        x = lg_ref[:, b * SEQ:(b + 1) * SEQ]
        e = jnp.exp(x - jnp.max(x, axis=0, keepdims=True))
        affs.append(e / jnp.sum(e, axis=0, keepdims=True))
    aff = jnp.concatenate(affs, axis=0)
    aff_ref[...] = aff
    thr = jnp.zeros((BATCH * N_EXPERTS, 1), jnp.int32)
    for bit in range(30, -1, -1):
        cand = thr | (1 << bit)
        hit = aff >= pltpu.bitcast(cand, f32)
        cnt = jnp.sum(jnp.where(hit, 1.0, 0.0), axis=1, keepdims=True)
        thr = jnp.where(cnt >= float(CAPACITY), cand, thr)
    above = aff >= pltpu.bitcast(thr + 1, f32)
    tie = (aff >= pltpu.bitcast(thr, f32)) & jnp.logical_not(above)
    need = float(CAPACITY) - jnp.sum(jnp.where(above, 1.0, 0.0), axis=1, keepdims=True)
    tri = jnp.where(lax.broadcasted_iota(jnp.int32, (LANES, LANES), 0)
                    < lax.broadcasted_iota(jnp.int32, (LANES, LANES), 1), 1.0, 0.0).astype(bf16)
    tie_rank = _excl_cumsum_tokens(jnp.where(tie, 1.0, 0.0), tri)
    sel = above | (tie & (tie_rank < need))
    rank_ref[...] = jnp.where(sel, _excl_cumsum_tokens(jnp.where(sel, 1.0, 0.0), tri), -1.0)


def _select(logits_t):
    shape = jax.ShapeDtypeStruct((BATCH * N_EXPERTS, SEQ), f32)
    blk = pl.BlockSpec((BATCH * N_EXPERTS, SEQ), lambda i: (0, 0))
    return pl.pallas_call(
        _select_kernel,
        grid=(1,),
        in_specs=[pl.BlockSpec((N_EXPERTS, TOKENS), lambda i: (0, 0))],
        out_specs=[blk, blk],
        out_shape=[shape, shape],
        compiler_params=_params(("arbitrary",), 40),
        name="select",
    )(logits_t)


def _route_kernel(rank_ref, aff_ref, idx_ref, gate_ref):
    rank = rank_ref[...]
    aff = aff_ref[...]
    high = jnp.floor(rank * (1.0 / SLOT_SPLIT))
    low = rank - high * SLOT_SPLIT
    digit = lax.broadcasted_iota(jnp.int32, (SLOT_SPLIT, SEQ), 0).astype(f32)
    tok = lax.broadcasted_iota(jnp.int32, (1, SEQ), 1)
    g1 = aff.astype(bf16).astype(f32)
    g2 = (aff - g1).astype(bf16).astype(f32)
    g3 = aff - g1 - g2
    per_expert = lambda v: jnp.broadcast_to(v.astype(f32), (N_EXPERTS, SEQ))
    payloads = [per_expert(tok // LANES), per_expert(tok % LANES), g1, g2, g3]
    high_rows = jnp.concatenate(
        [jnp.where(high[e:e + 1] == digit, 1.0, 0.0) for e in range(N_EXPERTS)], axis=0).astype(bf16)
    low_hit = [low[e:e + 1] == digit for e in range(N_EXPERTS)]
    n_rows = N_EXPERTS * SLOT_SPLIT
    same_expert = (lax.broadcasted_iota(jnp.int32, (n_rows, n_rows), 0) // SLOT_SPLIT
                   == lax.broadcasted_iota(jnp.int32, (n_rows, n_rows), 1) // SLOT_SPLIT)
    fold = jnp.where(lax.broadcasted_iota(jnp.int32, (n_rows, SLOT_SPLIT), 0) % SLOT_SPLIT
                     == lax.broadcasted_iota(jnp.int32, (n_rows, SLOT_SPLIT), 1), 1.0, 0.0).astype(bf16)
    folded = []
    for val in payloads:
        rows = jnp.concatenate(
            [jnp.where(low_hit[e], val[e:e + 1], 0.0) for e in range(N_EXPERTS)], axis=0).astype(bf16)
        picked = jnp.where(same_expert, _dot_nt(high_rows, rows), 0.0).astype(bf16)
        folded.append(jnp.dot(picked, fold, preferred_element_type=f32))
    idx_ref[...] = (folded[0] * float(LANES) + folded[1]).astype(jnp.int32)
    gate_ref[...] = folded[2] + folded[3] + folded[4]


def _route(rank, aff):
    n_rows = N_EXPERTS * SLOT_SPLIT
    blk = pl.BlockSpec((None, n_rows, CAPACITY // SLOT_SPLIT), lambda b: (b, 0, 0))
    shape = (BATCH, n_rows, CAPACITY // SLOT_SPLIT)
    per_seq = pl.BlockSpec((N_EXPERTS, SEQ), lambda b: (b, 0))
    return pl.pallas_call(
        _route_kernel,
        grid=(BATCH,),
        in_specs=[per_seq, per_seq],
        out_specs=[blk, blk],
        out_shape=[jax.ShapeDtypeStruct(shape, jnp.int32), jax.ShapeDtypeStruct(shape, f32)],
        compiler_params=_params(("arbitrary",), 40),
        name="route",
    )(rank, aff)


def _gather_kernel(idx_ref, h2_ref, xe_ref, tile_ref):
    def expert(e, c):
        base = e * CAPACITY
        for slot in range(CAPACITY):
            tok = idx_ref[0, base + slot]
            slab = h2_ref[pl.ds(pl.multiple_of(tok * SLAB_ROWS, SLAB_ROWS), SLAB_ROWS), :]
            tile_ref[pl.ds(slot, SLAB_ROWS, stride=TILE_PITCH), :] = slab
        for j in range(SLAB_ROWS):
            xe_ref[e, :, j * LANES:(j + 1) * LANES] = (
                tile_ref[pl.ds(j * TILE_PITCH, CAPACITY), :].astype(bf16))
        return c
    lax.fori_loop(0, N_EXPERTS, expert, 0)


def _gather(idx, h2_slab):
    return pl.pallas_call(
        _gather_kernel,
        grid=(BATCH,),
        in_specs=[
            pl.BlockSpec((None, 1, N_EXPERTS * CAPACITY), lambda b: (b, 0, 0),
                         memory_space=pltpu.SMEM),
            pl.BlockSpec((SEQ * SLAB_ROWS, LANES), lambda b: (b, 0)),
        ],
        out_specs=pl.BlockSpec((N_EXPERTS, None, CAPACITY, D_MODEL), lambda b: (0, b, 0, 0)),
        out_shape=jax.ShapeDtypeStruct((N_EXPERTS, BATCH, CAPACITY, D_MODEL), bf16),
        scratch_shapes=[pltpu.VMEM((SLAB_ROWS * TILE_PITCH, LANES), f32)],
        compiler_params=_params(("arbitrary",), 48),
        name="gather",
    )(idx, h2_slab)


def _ffn_kernel(xe_ref, wg_ref, wu_ref, wd_ref, y_ref, acc_ref):
    f = pl.program_id(1)

    last = D_FF // FF_TILE - 1

    def partial_out():
        xe = xe_ref[...]
        a = jnp.dot(xe, wg_ref[...].astype(bf16), preferred_element_type=f32)
        u = jnp.dot(xe, wu_ref[...].astype(bf16), preferred_element_type=f32)
        hidden = (a * (1.0 / (1.0 + jnp.exp(-a))) * u).astype(bf16)
        return jnp.dot(hidden, wd_ref[...].astype(bf16), preferred_element_type=f32)

    @pl.when(f == 0)
    def _():
        acc_ref[...] = partial_out()

    @pl.when((f > 0) & (f < last))
    def _():
        acc_ref[...] += partial_out()

    @pl.when(f == last)
    def _():
        y_ref[...] = (acc_ref[...] + partial_out()).astype(bf16)


def _ffn(xe, w_gate, w_up, w_down):
    rows = BATCH * CAPACITY
    return pl.pallas_call(
        _ffn_kernel,
        grid=(N_EXPERTS, D_FF // FF_TILE),
        in_specs=[
            pl.BlockSpec((None, rows, D_MODEL), lambda e, f: (e, 0, 0)),
            pl.BlockSpec((None, D_MODEL, FF_TILE), lambda e, f: (e, 0, f)),
            pl.BlockSpec((None, D_MODEL, FF_TILE), lambda e, f: (e, 0, f)),
            pl.BlockSpec((None, FF_TILE, D_MODEL), lambda e, f: (e, f, 0)),
        ],
        out_specs=pl.BlockSpec((None, rows, D_MODEL), lambda e, f: (e, 0, 0)),
        out_shape=jax.ShapeDtypeStruct((N_EXPERTS, rows, D_MODEL), bf16),
        scratch_shapes=[pltpu.VMEM((rows, D_MODEL), f32)],
        compiler_params=_params(("arbitrary", "arbitrary"), 56),
        name="ffn",
    )(xe, w_gate, w_up, w_down)


SCATTER_UNROLL = 8
COPY_ROWS = 2048
NORM_TOKENS = 256


def _combine_kernel(idx_ref, gate_ref, y_ref, x1_ref, g_ref, o_ref, acc_ref, tile_ref):
    step = pl.program_id(1)

    @pl.when(step == 0)
    def _():
        def copy(i, c):
            rows = pl.ds(pl.multiple_of(i * COPY_ROWS, COPY_ROWS), COPY_ROWS)
            acc_ref[rows, :] = x1_ref[rows, :]
            return c
        lax.fori_loop(0, SEQ * SLAB_ROWS // COPY_ROWS, copy, 0)

    def expert(k, c):
        base = (step * EXPERTS_PER_STEP + k) * CAPACITY
        for j in range(SLAB_ROWS):
            tile_ref[pl.ds(j * TILE_PITCH, CAPACITY), :] = (
                y_ref[k, :, j * LANES:(j + 1) * LANES].astype(f32))
        for first in range(0, CAPACITY, SCATTER_UNROLL):
            updates = []
            for slot in range(first, first + SCATTER_UNROLL):
                tok = idx_ref[0, base + slot]
                gate = gate_ref[0, base + slot]
                rows = pl.ds(pl.multiple_of(tok * SLAB_ROWS, SLAB_ROWS), SLAB_ROWS)
                contrib = tile_ref[pl.ds(slot, SLAB_ROWS, stride=TILE_PITCH), :]
                updates.append((rows, acc_ref[rows, :] + gate * contrib))
            for rows, value in updates:
                acc_ref[rows, :] = value
        return c
    lax.fori_loop(0, EXPERTS_PER_STEP, expert, 0)

    @pl.when(step == N_EXPERTS // EXPERTS_PER_STEP - 1)
    def _():
        def norm(i, c):
            first = i * (NORM_TOKENS * SLAB_ROWS)
            chunks = [acc_ref[pl.ds(first + j, NORM_TOKENS, stride=SLAB_ROWS), :]
                      for j in range(SLAB_ROWS)]
            squares = functools.reduce(lambda a, b: a + b, [ch * ch for ch in chunks])
            inv = lax.rsqrt(jnp.sum(squares, axis=1, keepdims=True) * (1.0 / D_MODEL) + RMS_EPS)
            rows = pl.ds(pl.multiple_of(i * NORM_TOKENS, NORM_TOKENS), NORM_TOKENS)
            for j in range(SLAB_ROWS):
                cols = slice(j * LANES, (j + 1) * LANES)
                o_ref[rows, cols] = chunks[j] * inv * g_ref[:, cols]
            return c
        lax.fori_loop(0, SEQ // NORM_TOKENS, norm, 0)


def _combine(idx, gate, y, x1_slab, g_final):
    smem = lambda: pl.BlockSpec((None, 1, N_EXPERTS * CAPACITY), lambda b, s: (b, 0, 0),
                                memory_space=pltpu.SMEM)
    return pl.pallas_call(
        _combine_kernel,
        grid=(BATCH, N_EXPERTS // EXPERTS_PER_STEP),
        in_specs=[
            smem(), smem(),
            pl.BlockSpec((EXPERTS_PER_STEP, None, CAPACITY, D_MODEL), lambda b, s: (s, b, 0, 0)),
            pl.BlockSpec((SEQ * SLAB_ROWS, LANES), lambda b, s: (b, 0)),
            pl.BlockSpec((1, D_MODEL), lambda b, s: (0, 0)),
        ],
        out_specs=pl.BlockSpec((SEQ, D_MODEL), lambda b, s: (b, 0)),
        out_shape=jax.ShapeDtypeStruct((TOKENS, D_MODEL), f32),
        scratch_shapes=[pltpu.VMEM((SEQ * SLAB_ROWS, LANES), f32),
                        pltpu.VMEM((SLAB_ROWS * TILE_PITCH, LANES), f32)],
        compiler_params=_params(("arbitrary", "arbitrary"), 56),
        name="combine",
    )(idx, gate, y, x1_slab, g_final)


def _rope_freqs():
    inv_freq = ROPE_THETA ** (-jnp.arange(ROT_HALF, dtype=f32) / ROT_HALF)
    return inv_freq.reshape(ROT_HALF, 1)


def kernel(x, positions, g_mix, w_in, conv_w, g_conv_out, g_attn_out, w_out, g_ffn, w_router,
           w_gate, w_up, w_down, g_final):
    x2 = x.reshape(TOKENS, D_MODEL)
    rope = _rope_tables(positions.reshape(BATCH, 1, SEQ), _rope_freqs())
    yc, edge, qkv = _in_proj(x2, g_mix[0].reshape(1, D_MODEL), w_in[0].astype(bf16), conv_w[0], rope)
    ya = _attention(qkv.reshape(BATCH, SEQ, 3 * ATTN_WIDTH))
    x1, h2, logits_t = _out_proj(yc, edge, ya.reshape(TOKENS, ATTN_WIDTH), x2, w_out[0].astype(bf16),
                                 g_conv_out[0].reshape(1, CONV_WIDTH),
                                 g_attn_out[0].reshape(1, ATTN_WIDTH), g_ffn[0].reshape(1, D_MODEL),
                                 w_router[0].T)
    idx, gate = _route(*_select(logits_t))
    idx = idx.reshape(BATCH, 1, N_EXPERTS * CAPACITY)
    gate = gate.reshape(BATCH, 1, N_EXPERTS * CAPACITY)
    xe = _gather(idx, h2)
    y = _ffn(xe.reshape(N_EXPERTS, BATCH * CAPACITY, D_MODEL), w_gate[0], w_up[0], w_down[0])
    out = _combine(idx, gate, y.reshape(N_EXPERTS, BATCH, CAPACITY, D_MODEL), x1,
                   g_final.reshape(1, D_MODEL))
    return out.reshape(BATCH, SEQ, D_MODEL)
```

```python
import functools

import jax
import jax.numpy as jnp
from jax import lax
from jax.experimental import pallas as pl
from jax.experimental.pallas import tpu as pltpu

D_MODEL = 1024
BATCH = 8
SEQ = 2048
TOKENS = BATCH * SEQ
CONV_WIDTH = 512
ATTN_WIDTH = 512
HEAD_DIM = 64
N_HEADS = 8
DILATED_PATTERNS = ((128, 1), (512, 4), (2048, 16))
N_PATTERNS = len(DILATED_PATTERNS)
ROPE_THETA = 500000.0
ROT_DIM = 16
ROT_HALF = ROT_DIM // 2
N_EXPERTS = 16
CAPACITY = 2 * SEQ // N_EXPERTS
D_FF = 2 * D_MODEL
PROJ_WIDTH = 3 * CONV_WIDTH + 3 * ATTN_WIDTH
RMS_EPS = 1e-6
NEG_INF = -1e30

LANES = 128
ROW_TILE = 512
FF_TILE = 512
BAND_HALF = 64
Q_CHUNK = 2 * BAND_HALF
K_WIN = 4 * BAND_HALF
HEADS_PER_STEP = LANES // HEAD_DIM
SUBLANES = 8
SLAB_ROWS = D_MODEL // LANES
TILE_PITCH = CAPACITY + SUBLANES
SLOT_SPLIT = 16
EXPERTS_PER_STEP = 4
MIB = 1024 * 1024

f32 = jnp.float32
bf16 = jnp.bfloat16


def _params(semantics, vmem_mib):
    return pltpu.CompilerParams(dimension_semantics=semantics, vmem_limit_bytes=vmem_mib * MIB)


def _rms(x, g):
    return x * lax.rsqrt(jnp.mean(x * x, axis=-1, keepdims=True) + RMS_EPS) * g


def _rope_table_kernel(pos_ref, invf_ref, cos_ref, s1_ref, s2_ref):
    ang = invf_ref[...] * pos_ref[...].astype(f32)
    trig = jnp.concatenate([jnp.cos(ang), jnp.sin(ang)], axis=0)
    lane = lax.broadcasted_iota(jnp.int32, (ROT_DIM, 3 * LANES), 1)
    row = lax.broadcasted_iota(jnp.int32, (ROT_DIM, 3 * LANES), 0)
    freq = row % ROT_HALF
    is_sin = row >= ROT_HALF
    at_lo = lane % HEAD_DIM == freq
    at_hi = lane % HEAD_DIM == freq + ROT_HALF
    table = lane // LANES
    place = (jnp.where((table == 0) & jnp.logical_not(is_sin) & (at_lo | at_hi), 1.0, 0.0)
             + jnp.where((table == 1) & is_sin & at_hi, 1.0, 0.0)
             - jnp.where((table == 2) & is_sin & at_lo, 1.0, 0.0)).astype(bf16)
    spread = jnp.zeros((SEQ, 3 * LANES), f32)
    rest = trig
    for _ in range(3):
        piece = rest.astype(bf16)
        rest = rest - piece.astype(f32)
        spread = spread + lax.dot_general(piece, place, (((0,), (0,)), ((), ())),
                                          preferred_element_type=f32)
    unrotated = jnp.where(
        lax.broadcasted_iota(jnp.int32, (1, LANES), 1) % HEAD_DIM >= ROT_DIM, 1.0, 0.0)
    cos_ref[...] = spread[:, :LANES] + unrotated
    s1_ref[...] = spread[:, LANES:2 * LANES]
    s2_ref[...] = spread[:, 2 * LANES:]


def _rope_tables(pos_row, invf_col):
    blk = pl.BlockSpec((SEQ, LANES), lambda b: (b, 0))
    return pl.pallas_call(
        _rope_table_kernel,
        grid=(BATCH,),
        in_specs=[
            pl.BlockSpec((None, 1, SEQ), lambda b: (b, 0, 0)),
            pl.BlockSpec((ROT_HALF, 1), lambda b: (0, 0)),
        ],
        out_specs=[blk, blk, blk],
        out_shape=[jax.ShapeDtypeStruct((TOKENS, LANES), f32)] * 3,
        compiler_params=_params(("arbitrary",), 32),
        name="rope_tables",
    )(pos_row, invf_col)


EDGE_U_FIRST, EDGE_U_LAST, EDGE_B_FIRST, EDGE_B_LAST = (k * SUBLANES for k in range(4))
EDGE_ROWS = 4 * SUBLANES


def _in_proj_kernel(x_ref, g_ref, w_ref, cw_ref, cos_ref, s1_ref, s2_ref,
                    yc_ref, edge_ref, qkv_ref, norm_ref):
    h = _rms(x_ref[...], g_ref[...]).astype(bf16)
    p = jnp.dot(h, w_ref[...], preferred_element_type=f32)
    gate_b = p[:, :CONV_WIDTH]
    u = p[:, CONV_WIDTH:2 * CONV_WIDTH] * p[:, 2 * CONV_WIDTH:3 * CONV_WIDTH]
    cw = cw_ref[...]
    row = lax.broadcasted_iota(jnp.int32, (ROW_TILE, CONV_WIDTH), 0)
    before = jnp.where(row == 0, 0.0, pltpu.roll(u, 1, 0))
    after = jnp.where(row == ROW_TILE - 1, 0.0, pltpu.roll(u, ROW_TILE - 1, 0))
    yc_ref[...] = gate_b * (cw[0:1] * before + cw[1:2] * u + cw[2:3] * after)
    last = ROW_TILE - SUBLANES
    edge_ref[pl.ds(EDGE_U_FIRST, SUBLANES), :] = u[:SUBLANES]
    edge_ref[pl.ds(EDGE_U_LAST, SUBLANES), :] = u[last:]
    edge_ref[pl.ds(EDGE_B_FIRST, SUBLANES), :] = gate_b[:SUBLANES] * cw[0:1]
    edge_ref[pl.ds(EDGE_B_LAST, SUBLANES), :] = gate_b[last:] * cw[2:3]
    cs, s1, s2 = cos_ref[...], s1_ref[...], s2_ref[...]
    head_of = jnp.where(lax.broadcasted_iota(jnp.int32, (LANES, LANES), 0) // HEAD_DIM
                        == lax.broadcasted_iota(jnp.int32, (LANES, LANES), 1), 1.0, 0.0).astype(bf16)
    assert 2 * ATTN_WIDTH // LANES == SUBLANES
    for blk in range(2 * ATTN_WIDTH // LANES):
        x = p[:, 3 * CONV_WIDTH + blk * LANES:3 * CONV_WIDTH + (blk + 1) * LANES]
        y = x * cs + pltpu.roll(x, ROT_HALF, 1) * s1 + pltpu.roll(x, LANES - ROT_HALF, 1) * s2
        if blk < ATTN_WIDTH // LANES:
            y = y * HEAD_DIM ** -0.5
        qkv_ref[:, blk * LANES:(blk + 1) * LANES] = y
        row_norms = jnp.dot((y * y).astype(bf16), head_of, preferred_element_type=f32)
        norm_ref[pl.ds(blk, 1), :] = jnp.max(row_norms, axis=0, keepdims=True)
    qkv_ref[:, 2 * ATTN_WIDTH:] = p[:, 3 * CONV_WIDTH + 2 * ATTN_WIDTH:]


def _in_proj(x2, g_mix, w_in_bf, conv_w, rope):
    n = TOKENS // ROW_TILE
    table = pl.BlockSpec((ROW_TILE, LANES), lambda i: (i, 0))
    return pl.pallas_call(
        _in_proj_kernel,
        grid=(n,),
        in_specs=[
            pl.BlockSpec((ROW_TILE, D_MODEL), lambda i: (i, 0)),
            pl.BlockSpec((1, D_MODEL), lambda i: (0, 0)),
            pl.BlockSpec((D_MODEL, PROJ_WIDTH), lambda i: (0, 0)),
            pl.BlockSpec((3, CONV_WIDTH), lambda i: (0, 0)),
            table, table, table,
        ],
        out_specs=[
            pl.BlockSpec((ROW_TILE, CONV_WIDTH), lambda i: (i, 0)),
            pl.BlockSpec((EDGE_ROWS, CONV_WIDTH), lambda i: (i, 0)),
            pl.BlockSpec((ROW_TILE, 3 * ATTN_WIDTH), lambda i: (i, 0)),
            pl.BlockSpec((SUBLANES, LANES), lambda i: (i, 0)),
        ],
        out_shape=[
            jax.ShapeDtypeStruct((TOKENS, CONV_WIDTH), f32),
            jax.ShapeDtypeStruct((n * EDGE_ROWS, CONV_WIDTH), f32),
            jax.ShapeDtypeStruct((TOKENS, 3 * ATTN_WIDTH), f32),
            jax.ShapeDtypeStruct((n * SUBLANES, LANES), f32),
        ],
        compiler_params=_params(("arbitrary",), 48),
        name="in_proj",
    )(x2, g_mix, w_in_bf, conv_w, *rope)


CAP_INTERIOR, CAP_FIRST, CAP_LAST, CAP_SINGLE = range(4)
MASK_OPEN = float(jnp.finfo(jnp.float32).max)
MID_DIL = DILATED_PATTERNS[1][1]
MID_LEN = SEQ // MID_DIL
assert [d for _, d in DILATED_PATTERNS] == [1, MID_DIL, MID_DIL * MID_DIL]
assert all(w // 2 // d == BAND_HALF for w, d in DILATED_PATTERNS)


SAFE_SCORE = 40.0
NORM_SLACK = 1.1


def _attn_kernel(q_ref, k_ref, v_ref, o_ref, cap_ref, q4_ref, k4_ref, v4_ref,
                 qlo_ref, qhi_ref, kpad_ref, vlo_ref, vhi_ref,
                 num_ref, m_ref, l_ref, *, exact_max):
    @pl.when(pl.program_id(1) == 0)
    def _():
        row = lax.broadcasted_iota(jnp.int32, (2 * Q_CHUNK, K_WIN), 0) % Q_CHUNK
        col = lax.broadcasted_iota(jnp.int32, (2 * Q_CHUNK, K_WIN), 1)
        band = (col - row >= 0) & (col - row <= 2 * BAND_HALF)
        not_before = col >= BAND_HALF
        not_after = col < K_WIN - BAND_HALF
        for idx, ok in ((CAP_INTERIOR, band), (CAP_FIRST, band & not_before),
                        (CAP_LAST, band & not_after), (CAP_SINGLE, band & not_before & not_after)):
            cap_ref[idx] = jnp.where(ok, MASK_OPEN, NEG_INF)

    lane_lo = lax.broadcasted_iota(jnp.int32, (Q_CHUNK, LANES), 1) < HEAD_DIM
    zero_pad = jnp.zeros((BAND_HALF, LANES), bf16)

    def zero_rows(start):
        for ref in (kpad_ref, vlo_ref, vhi_ref):
            ref[pl.ds(start, BAND_HALF), :] = zero_pad

    def stage(sources, src, q_row, pad_row, keep=None):
        qf, kf, vf = (ref[src, :] for ref in sources)
        if keep is not None:
            for ref, val in zip((q4_ref, k4_ref, v4_ref), (qf, kf, vf)):
                ref[keep, :] = val
        q_dst, p_dst = pl.ds(q_row, Q_CHUNK), pl.ds(pad_row, Q_CHUNK)
        qlo_ref[q_dst, :] = jnp.where(lane_lo, qf, 0.0).astype(bf16)
        qhi_ref[q_dst, :] = jnp.where(lane_lo, 0.0, qf).astype(bf16)
        kpad_ref[p_dst, :] = kf.astype(bf16)
        vlo_ref[p_dst, :] = jnp.where(lane_lo, vf, 0.0).astype(bf16)
        vhi_ref[p_dst, :] = jnp.where(lane_lo, 0.0, vf).astype(bf16)

    def chunk(pat, out_rows, q_row, pad_row, cap):
        q2 = jnp.concatenate([qlo_ref[pl.ds(q_row, Q_CHUNK), :], qhi_ref[pl.ds(q_row, Q_CHUNK), :]],
                             axis=0)
        s = lax.dot_general(q2, kpad_ref[pl.ds(pad_row, K_WIN), :], (((1,), (1,)), ((), ())),
                            preferred_element_type=f32)
        s = jnp.minimum(s, cap_ref[cap])
        if exact_max:
            m = jnp.max(s, axis=1, keepdims=True)
            p = jnp.exp(s - m)
        else:
            p = jnp.exp(s)
        l = jnp.sum(p, axis=1, keepdims=True)
        pb = p.astype(bf16)
        num = (jnp.dot(pb[:Q_CHUNK], vlo_ref[pl.ds(pad_row, K_WIN), :], preferred_element_type=f32)
               + jnp.dot(pb[Q_CHUNK:], vhi_ref[pl.ds(pad_row, K_WIN), :], preferred_element_type=f32))
        num_ref[pat, out_rows, :] = num
        if exact_max:
            m_ref[pat, out_rows, :] = jnp.where(lane_lo, m[:Q_CHUNK], m[Q_CHUNK:])
        l_ref[pat, out_rows, :] = jnp.where(lane_lo, l[:Q_CHUNK], l[Q_CHUNK:])

    def run_pattern(pat, sources, classes, length, keep_f32=False):
        n_chunks = length // Q_CHUNK
        region = length + 2 * BAND_HALF

        def rows(first, stride, i):
            if stride == 1:
                return pl.ds(first + i * Q_CHUNK, Q_CHUNK)
            return pl.ds(first + stride * Q_CHUNK * i, Q_CHUNK, stride=stride)

        for u in range(len(classes)):
            zero_rows(u * region)
            zero_rows(u * region + BAND_HALF + length)
        for u, (src0, src_stride, _, _) in enumerate(classes):
            for i in range(n_chunks):
                keep = pl.ds(u * length + i * Q_CHUNK, Q_CHUNK) if keep_f32 else None
                stage(sources, rows(src0, src_stride, i), u * length + i * Q_CHUNK,
                      u * region + BAND_HALF + i * Q_CHUNK, keep)
        for u, (_, _, out0, out_stride) in enumerate(classes):
            for j in range(n_chunks):
                if n_chunks == 1:
                    cap = CAP_SINGLE
                else:
                    cap = CAP_FIRST if j == 0 else CAP_LAST if j == n_chunks - 1 else CAP_INTERIOR
                chunk(pat, rows(out0, out_stride, j), u * length + j * Q_CHUNK,
                      u * region + j * Q_CHUNK, cap)

    inputs = (q_ref, k_ref, v_ref)
    run_pattern(0, inputs, [(0, 1, 0, 1)], SEQ)
    run_pattern(1, inputs, [(r, MID_DIL, r * MID_LEN, 1) for r in range(MID_DIL)], MID_LEN,
                keep_f32=True)
    run_pattern(2, (q4_ref, k4_ref, v4_ref),
                [(r * MID_LEN + a, MID_DIL, r * MID_LEN + a, MID_DIL)
                 for r in range(MID_DIL) for a in range(MID_DIL)], MID_LEN // MID_DIL)

    def mix(blk, c):
        mid_rows = pl.ds(pl.multiple_of(blk * Q_CHUNK, Q_CHUNK), Q_CHUNK)
        per_class = MID_LEN // Q_CHUNK
        pos_rows = pl.ds(blk // per_class + MID_DIL * Q_CHUNK * (blk % per_class), Q_CHUNK,
                         stride=MID_DIL)
        rows = (pos_rows, mid_rows, mid_rows)
        num = jnp.zeros((Q_CHUNK, LANES), f32)
        den = jnp.zeros((Q_CHUNK, LANES), f32)
        if exact_max:
            ms = [m_ref[p, rows[p], :] for p in range(N_PATTERNS)]
            m_all = functools.reduce(jnp.maximum, ms)
        for p in range(N_PATTERNS):
            a = jnp.exp(ms[p] - m_all) if exact_max else 1.0
            den = den + a * l_ref[p, rows[p], :]
            num = num + a * num_ref[p, rows[p], :]
        o_ref[pos_rows, :] = num / den
        return c
    lax.fori_loop(0, SEQ // Q_CHUNK, mix, 0)


def _attention(qkv3, exact_max):
    n_hp = N_HEADS // HEADS_PER_STEP
    blk = lambda off: pl.BlockSpec((None, SEQ, LANES), lambda b, h, off=off: (b, 0, off + h))
    pad_rows = max(SEQ + 2 * BAND_HALF * d for _, d in DILATED_PATTERNS)
    stats = pltpu.VMEM((N_PATTERNS, SEQ, LANES), f32)
    rows_f32 = pltpu.VMEM((SEQ, LANES), f32)
    return pl.pallas_call(
        functools.partial(_attn_kernel, exact_max=exact_max),
        grid=(BATCH, n_hp),
        in_specs=[blk(0), blk(n_hp), blk(2 * n_hp)],
        out_specs=pl.BlockSpec((None, SEQ, LANES), lambda b, h: (b, 0, h)),
        out_shape=jax.ShapeDtypeStruct((BATCH, SEQ, ATTN_WIDTH), f32),
        scratch_shapes=[
            pltpu.VMEM((4, 2 * Q_CHUNK, K_WIN), f32),
            rows_f32, rows_f32, rows_f32,
            pltpu.VMEM((SEQ, LANES), bf16), pltpu.VMEM((SEQ, LANES), bf16),
            pltpu.VMEM((pad_rows, LANES), bf16), pltpu.VMEM((pad_rows, LANES), bf16),
            pltpu.VMEM((pad_rows, LANES), bf16),
            stats, stats, stats,
        ],
        compiler_params=_params(("arbitrary", "arbitrary"), 48),
        name="attention_exact" if exact_max else "attention",
    )(qkv3, qkv3, qkv3)


def _store_token_major(ref, x):
    for j in range(SLAB_ROWS):
        ref[pl.ds(j, x.shape[0], stride=SLAB_ROWS), :] = x[:, j * LANES:(j + 1) * LANES]


def _dot_nt(a, b):
    return lax.dot_general(a, b, (((1,), (1,)), ((), ())), preferred_element_type=f32)


def _out_proj_kernel(yc_ref, edge_ref, edge_prev_ref, edge_next_ref, ya_ref, x_ref, w_ref,
                     gc_ref, ga_ref, gf_ref, wr_ref, x1_ref, h2_ref, lg_ref):
    tile = pl.program_id(0) % (SEQ // ROW_TILE)
    from_prev = (edge_ref[pl.ds(EDGE_B_FIRST, 1), :]
                 * edge_prev_ref[pl.ds(EDGE_U_LAST + SUBLANES - 1, 1), :])
    from_next = (edge_ref[pl.ds(EDGE_B_LAST + SUBLANES - 1, 1), :]
                 * edge_next_ref[pl.ds(EDGE_U_FIRST, 1), :])
    from_prev = jnp.where(tile == 0, 0.0, from_prev)
    from_next = jnp.where(tile == SEQ // ROW_TILE - 1, 0.0, from_next)
    row = lax.broadcasted_iota(jnp.int32, (ROW_TILE, CONV_WIDTH), 0)
    y_conv = (yc_ref[...] + jnp.where(row == 0, from_prev, 0.0)
              + jnp.where(row == ROW_TILE - 1, from_next, 0.0))
    yc = _rms(y_conv, gc_ref[...]).astype(bf16)
    ya = _rms(ya_ref[...], ga_ref[...]).astype(bf16)
    mix = (jnp.dot(yc, w_ref[pl.ds(0, CONV_WIDTH), :], preferred_element_type=f32)
           + jnp.dot(ya, w_ref[pl.ds(CONV_WIDTH, ATTN_WIDTH), :], preferred_element_type=f32))
    x1 = x_ref[...] + mix
    _store_token_major(x1_ref, x1)
    h2 = _rms(x1, gf_ref[...])
    _store_token_major(h2_ref, h2)
    wr = wr_ref[...]
    wr_hi = wr.astype(bf16)
    wr_lo = (wr - wr_hi.astype(f32)).astype(bf16)
    h2_hi = h2.astype(bf16)
    h2_lo = (h2 - h2_hi.astype(f32)).astype(bf16)
    both = _dot_nt(jnp.concatenate([wr_hi, wr_lo], axis=0), h2_hi)
    lg_ref[...] = both[:N_EXPERTS] + both[N_EXPERTS:] + _dot_nt(wr_hi, h2_lo)


def _out_proj(yc, edge, ya, x2, w_out_bf, g_conv, g_attn, g_ffn, w_router_t):
    n = TOKENS // ROW_TILE
    slab = pl.BlockSpec((ROW_TILE * SLAB_ROWS, LANES), lambda i: (i, 0))
    slab_shape = jax.ShapeDtypeStruct((TOKENS * SLAB_ROWS, LANES), f32)
    return pl.pallas_call(
        _out_proj_kernel,
        grid=(n,),
        in_specs=[
            pl.BlockSpec((ROW_TILE, CONV_WIDTH), lambda i: (i, 0)),
            pl.BlockSpec((EDGE_ROWS, CONV_WIDTH), lambda i: (i, 0)),
            pl.BlockSpec((EDGE_ROWS, CONV_WIDTH), lambda i: (jnp.maximum(i - 1, 0), 0)),
            pl.BlockSpec((EDGE_ROWS, CONV_WIDTH), lambda i: (jnp.minimum(i + 1, n - 1), 0)),
            pl.BlockSpec((ROW_TILE, ATTN_WIDTH), lambda i: (i, 0)),
            pl.BlockSpec((ROW_TILE, D_MODEL), lambda i: (i, 0)),
            pl.BlockSpec((D_MODEL, D_MODEL), lambda i: (0, 0)),
            pl.BlockSpec((1, CONV_WIDTH), lambda i: (0, 0)),
            pl.BlockSpec((1, ATTN_WIDTH), lambda i: (0, 0)),
            pl.BlockSpec((1, D_MODEL), lambda i: (0, 0)),
            pl.BlockSpec((N_EXPERTS, D_MODEL), lambda i: (0, 0)),
        ],
        out_specs=[slab, slab, pl.BlockSpec((N_EXPERTS, ROW_TILE), lambda i: (0, i))],
        out_shape=[slab_shape, slab_shape, jax.ShapeDtypeStruct((N_EXPERTS, TOKENS), f32)],
        compiler_params=_params(("arbitrary",), 48),
        name="out_proj",
    )(yc, edge, edge, edge, ya, x2, w_out_bf, g_conv, g_attn, g_ffn, w_router_t)


N_LANE_BLOCKS = SEQ // LANES


def _excl_cumsum_tokens(x, tri):
    n = x.shape[0]
    stacked = jnp.concatenate([x[:, j * LANES:(j + 1) * LANES] for j in range(N_LANE_BLOCKS)], axis=0)
    within = jnp.dot(stacked.astype(bf16), tri, preferred_element_type=f32)
    totals = jnp.sum(stacked, axis=1, keepdims=True)
    out, offset = [], jnp.zeros((n, 1), f32)
    for j in range(N_LANE_BLOCKS):
        rows = slice(j * n, (j + 1) * n)
        out.append(within[rows] + offset)
        offset = offset + totals[rows]
    return jnp.concatenate(out, axis=1)


def _select_kernel(lg_ref, rank_ref, aff_ref):
    affs = []
    for b in range(BATCH):
        x = lg_ref[:, b * SEQ:(b + 1) * SEQ]
        e = jnp.exp(x - jnp.max(x, axis=0, keepdims=True))
        affs.append(e / jnp.sum(e, axis=0, keepdims=True))
    aff = jnp.concatenate(affs, axis=0)
    aff_ref[...] = aff
    thr = jnp.zeros((BATCH * N_EXPERTS, 1), jnp.int32)
    for bit in range(30, -1, -1):
        cand = thr | (1 << bit)
        hit = aff >= pltpu.bitcast(cand, f32)
        cnt = jnp.sum(jnp.where(hit, 1.0, 0.0), axis=1, keepdims=True)
        thr = jnp.where(cnt >= float(CAPACITY), cand, thr)
    above = aff >= pltpu.bitcast(thr + 1, f32)
    tie = (aff >= pltpu.bitcast(thr, f32)) & jnp.logical_not(above)
    need = float(CAPACITY) - jnp.sum(jnp.where(above, 1.0, 0.0), axis=1, keepdims=True)
    tri = jnp.where(lax.broadcasted_iota(jnp.int32, (LANES, LANES), 0)
                    < lax.broadcasted_iota(jnp.int32, (LANES, LANES), 1), 1.0, 0.0).astype(bf16)
    tie_rank = _excl_cumsum_tokens(jnp.where(tie, 1.0, 0.0), tri)
    sel = above | (tie & (tie_rank < need))
    rank_ref[...] = jnp.where(sel, _excl_cumsum_tokens(jnp.where(sel, 1.0, 0.0), tri), -1.0)


def _select(logits_t):
    shape = jax.ShapeDtypeStruct((BATCH * N_EXPERTS, SEQ), f32)
    blk = pl.BlockSpec((BATCH * N_EXPERTS, SEQ), lambda i: (0, 0))
    return pl.pallas_call(
        _select_kernel,
        grid=(1,),
        in_specs=[pl.BlockSpec((N_EXPERTS, TOKENS), lambda i: (0, 0))],
        out_specs=[blk, blk],
        out_shape=[shape, shape],
        compiler_params=_params(("arbitrary",), 40),
        name="select",
    )(logits_t)


def _route_kernel(rank_ref, aff_ref, idx_ref, gate_ref):
    rank = rank_ref[...]
    aff = aff_ref[...]
    high = jnp.floor(rank * (1.0 / SLOT_SPLIT))
    low = rank - high * SLOT_SPLIT
    digit = lax.broadcasted_iota(jnp.int32, (SLOT_SPLIT, SEQ), 0).astype(f32)
    tok = lax.broadcasted_iota(jnp.int32, (1, SEQ), 1)
    g1 = aff.astype(bf16).astype(f32)
    g2 = (aff - g1).astype(bf16).astype(f32)
    g3 = aff - g1 - g2
    per_expert = lambda v: jnp.broadcast_to(v.astype(f32), (N_EXPERTS, SEQ))
    payloads = [per_expert(tok // LANES), per_expert(tok % LANES), g1, g2, g3]
    high_rows = jnp.concatenate(
        [jnp.where(high[e:e + 1] == digit, 1.0, 0.0) for e in range(N_EXPERTS)], axis=0).astype(bf16)
    low_hit = [low[e:e + 1] == digit for e in range(N_EXPERTS)]
    n_rows = N_EXPERTS * SLOT_SPLIT
    same_expert = (lax.broadcasted_iota(jnp.int32, (n_rows, n_rows), 0) // SLOT_SPLIT
                   == lax.broadcasted_iota(jnp.int32, (n_rows, n_rows), 1) // SLOT_SPLIT)
    fold = jnp.where(lax.broadcasted_iota(jnp.int32, (n_rows, SLOT_SPLIT), 0) % SLOT_SPLIT
                     == lax.broadcasted_iota(jnp.int32, (n_rows, SLOT_SPLIT), 1), 1.0, 0.0).astype(bf16)
    folded = []
    for val in payloads:
        rows = jnp.concatenate(
            [jnp.where(low_hit[e], val[e:e + 1], 0.0) for e in range(N_EXPERTS)], axis=0).astype(bf16)
        picked = jnp.where(same_expert, _dot_nt(high_rows, rows), 0.0).astype(bf16)
        folded.append(jnp.dot(picked, fold, preferred_element_type=f32))
    idx_ref[...] = (folded[0] * float(LANES) + folded[1]).astype(jnp.int32)
    gate_ref[...] = folded[2] + folded[3] + folded[4]


def _route(rank, aff):
    n_rows = N_EXPERTS * SLOT_SPLIT
    blk = pl.BlockSpec((None, n_rows, CAPACITY // SLOT_SPLIT), lambda b: (b, 0, 0))
    shape = (BATCH, n_rows, CAPACITY // SLOT_SPLIT)
    per_seq = pl.BlockSpec((N_EXPERTS, SEQ), lambda b: (b, 0))
    return pl.pallas_call(
        _route_kernel,
        grid=(BATCH,),
        in_specs=[per_seq, per_seq],
        out_specs=[blk, blk],
        out_shape=[jax.ShapeDtypeStruct(shape, jnp.int32), jax.ShapeDtypeStruct(shape, f32)],
        compiler_params=_params(("arbitrary",), 40),
        name="route",
    )(rank, aff)


def _gather_kernel(idx_ref, h2_ref, xe_ref, tile_ref):
    def expert(e, c):
        base = e * CAPACITY
        for slot in range(CAPACITY):
            tok = idx_ref[0, base + slot]
            slab = h2_ref[pl.ds(pl.multiple_of(tok * SLAB_ROWS, SLAB_ROWS), SLAB_ROWS), :]
            tile_ref[pl.ds(slot, SLAB_ROWS, stride=TILE_PITCH), :] = slab
        for j in range(SLAB_ROWS):
            xe_ref[e, :, j * LANES:(j + 1) * LANES] = (
                tile_ref[pl.ds(j * TILE_PITCH, CAPACITY), :].astype(bf16))
        return c
    lax.fori_loop(0, N_EXPERTS, expert, 0)


def _gather(idx, h2_slab):
    return pl.pallas_call(
        _gather_kernel,
        grid=(BATCH,),
        in_specs=[
            pl.BlockSpec((None, 1, N_EXPERTS * CAPACITY), lambda b: (b, 0, 0),
                         memory_space=pltpu.SMEM),
            pl.BlockSpec((SEQ * SLAB_ROWS, LANES), lambda b: (b, 0)),
        ],
        out_specs=pl.BlockSpec((N_EXPERTS, None, CAPACITY, D_MODEL), lambda b: (0, b, 0, 0)),
        out_shape=jax.ShapeDtypeStruct((N_EXPERTS, BATCH, CAPACITY, D_MODEL), bf16),
        scratch_shapes=[pltpu.VMEM((SLAB_ROWS * TILE_PITCH, LANES), f32)],
        compiler_params=_params(("arbitrary",), 48),
        name="gather",
    )(idx, h2_slab)


def _ffn_kernel(xe_ref, wg_ref, wu_ref, wd_ref, y_ref, acc_ref):
    f = pl.program_id(1)

    last = D_FF // FF_TILE - 1

    def partial_out():
        xe = xe_ref[...]
        a = jnp.dot(xe, wg_ref[...].astype(bf16), preferred_element_type=f32)
        u = jnp.dot(xe, wu_ref[...].astype(bf16), preferred_element_type=f32)
        hidden = (a * (1.0 / (1.0 + jnp.exp(-a))) * u).astype(bf16)
        return jnp.dot(hidden, wd_ref[...].astype(bf16), preferred_element_type=f32)

    @pl.when(f == 0)
    def _():
        acc_ref[...] = partial_out()

    @pl.when((f > 0) & (f < last))
    def _():
        acc_ref[...] += partial_out()

    @pl.when(f == last)
    def _():
        y_ref[...] = (acc_ref[...] + partial_out()).astype(bf16)


def _ffn(xe, w_gate, w_up, w_down):
    rows = BATCH * CAPACITY
    return pl.pallas_call(
        _ffn_kernel,
        grid=(N_EXPERTS, D_FF // FF_TILE),
        in_specs=[
            pl.BlockSpec((None, rows, D_MODEL), lambda e, f: (e, 0, 0)),
            pl.BlockSpec((None, D_MODEL, FF_TILE), lambda e, f: (e, 0, f)),
            pl.BlockSpec((None, D_MODEL, FF_TILE), lambda e, f: (e, 0, f)),
            pl.BlockSpec((None, FF_TILE, D_MODEL), lambda e, f: (e, f, 0)),
        ],
        out_specs=pl.BlockSpec((None, rows, D_MODEL), lambda e, f: (e, 0, 0)),
        out_shape=jax.ShapeDtypeStruct((N_EXPERTS, rows, D_MODEL), bf16),
        scratch_shapes=[pltpu.VMEM((rows, D_MODEL), f32)],
        compiler_params=_params(("arbitrary", "arbitrary"), 56),
        name="ffn",
    )(xe, w_gate, w_up, w_down)


SCATTER_UNROLL = 8
COPY_ROWS = 2048
NORM_TOKENS = 256


def _combine_kernel(idx_ref, gate_ref, y_ref, x1_ref, g_ref, o_ref, acc_ref, tile_ref):
    step = pl.program_id(1)

    @pl.when(step == 0)
    def _():
        def copy(i, c):
            rows = pl.ds(pl.multiple_of(i * COPY_ROWS, COPY_ROWS), COPY_ROWS)
            acc_ref[rows, :] = x1_ref[rows, :]
            return c
        lax.fori_loop(0, SEQ * SLAB_ROWS // COPY_ROWS, copy, 0)

    def expert(k, c):
        base = (step * EXPERTS_PER_STEP + k) * CAPACITY
        for j in range(SLAB_ROWS):
            tile_ref[pl.ds(j * TILE_PITCH, CAPACITY), :] = (
                y_ref[k, :, j * LANES:(j + 1) * LANES].astype(f32))
        for first in range(0, CAPACITY, SCATTER_UNROLL):
            updates = []
            for slot in range(first, first + SCATTER_UNROLL):
                tok = idx_ref[0, base + slot]
                gate = gate_ref[0, base + slot]
                rows = pl.ds(pl.multiple_of(tok * SLAB_ROWS, SLAB_ROWS), SLAB_ROWS)
                contrib = tile_ref[pl.ds(slot, SLAB_ROWS, stride=TILE_PITCH), :]
                updates.append((rows, acc_ref[rows, :] + gate * contrib))
            for rows, value in updates:
                acc_ref[rows, :] = value
        return c
    lax.fori_loop(0, EXPERTS_PER_STEP, expert, 0)

    @pl.when(step == N_EXPERTS // EXPERTS_PER_STEP - 1)
    def _():
        def norm(i, c):
            first = i * (NORM_TOKENS * SLAB_ROWS)
            chunks = [acc_ref[pl.ds(first + j, NORM_TOKENS, stride=SLAB_ROWS), :]
                      for j in range(SLAB_ROWS)]
            squares = functools.reduce(lambda a, b: a + b, [ch * ch for ch in chunks])
            inv = lax.rsqrt(jnp.sum(squares, axis=1, keepdims=True) * (1.0 / D_MODEL) + RMS_EPS)
            rows = pl.ds(pl.multiple_of(i * NORM_TOKENS, NORM_TOKENS), NORM_TOKENS)
            for j in range(SLAB_ROWS):
                cols = slice(j * LANES, (j + 1) * LANES)
                o_ref[rows, cols] = chunks[j] * inv * g_ref[:, cols]
            return c
        lax.fori_loop(0, SEQ // NORM_TOKENS, norm, 0)


def _combine(idx, gate, y, x1_slab, g_final):
    smem = lambda: pl.BlockSpec((None, 1, N_EXPERTS * CAPACITY), lambda b, s: (b, 0, 0),
                                memory_space=pltpu.SMEM)
    return pl.pallas_call(
        _combine_kernel,
        grid=(BATCH, N_EXPERTS // EXPERTS_PER_STEP),
        in_specs=[
            smem(), smem(),
            pl.BlockSpec((EXPERTS_PER_STEP, None, CAPACITY, D_MODEL), lambda b, s: (s, b, 0, 0)),
            pl.BlockSpec((SEQ * SLAB_ROWS, LANES), lambda b, s: (b, 0)),
            pl.BlockSpec((1, D_MODEL), lambda b, s: (0, 0)),
        ],
        out_specs=pl.BlockSpec((SEQ, D_MODEL), lambda b, s: (b, 0)),
        out_shape=jax.ShapeDtypeStruct((TOKENS, D_MODEL), f32),
        scratch_shapes=[pltpu.VMEM((SEQ * SLAB_ROWS, LANES), f32),
                        pltpu.VMEM((SLAB_ROWS * TILE_PITCH, LANES), f32)],
        compiler_params=_params(("arbitrary", "arbitrary"), 56),
        name="combine",
    )(idx, gate, y, x1_slab, g_final)


def _rope_freqs():
    inv_freq = ROPE_THETA ** (-jnp.arange(ROT_HALF, dtype=f32) / ROT_HALF)
    return inv_freq.reshape(ROT_HALF, 1)


def kernel(x, positions, g_mix, w_in, conv_w, g_conv_out, g_attn_out, w_out, g_ffn, w_router,
           w_gate, w_up, w_down, g_final):
    x2 = x.reshape(TOKENS, D_MODEL)
    rope = _rope_tables(positions.reshape(BATCH, 1, SEQ), _rope_freqs())
    yc, edge, qkv, norms = _in_proj(x2, g_mix[0].reshape(1, D_MODEL), w_in[0].astype(bf16),
                                    conv_w[0], rope)
    norms = norms.reshape(BATCH, SEQ // ROW_TILE, 2, ATTN_WIDTH // LANES, LANES)
    norms = norms[..., :HEADS_PER_STEP].max(axis=1)
    scores_bounded = jnp.all(norms[:, 0] * norms[:, 1] * NORM_SLACK <= SAFE_SCORE ** 2)
    qkv3 = qkv.reshape(BATCH, SEQ, 3 * ATTN_WIDTH)
    ya = lax.cond(scores_bounded, functools.partial(_attention, exact_max=False),
                  functools.partial(_attention, exact_max=True), qkv3)
    x1, h2, logits_t = _out_proj(yc, edge, ya.reshape(TOKENS, ATTN_WIDTH), x2, w_out[0].astype(bf16),
                                 g_conv_out[0].reshape(1, CONV_WIDTH),
                                 g_attn_out[0].reshape(1, ATTN_WIDTH), g_ffn[0].reshape(1, D_MODEL),
                                 w_router[0].T)
    idx, gate = _route(*_select(logits_t))
    idx = idx.reshape(BATCH, 1, N_EXPERTS * CAPACITY)
    gate = gate.reshape(BATCH, 1, N_EXPERTS * CAPACITY)
    xe = _gather(idx, h2)
    y = _ffn(xe.reshape(N_EXPERTS, BATCH * CAPACITY, D_MODEL), w_gate[0], w_up[0], w_down[0])
    out = _combine(idx, gate, y.reshape(N_EXPERTS, BATCH, CAPACITY, D_MODEL), x1,
                   g_final.reshape(1, D_MODEL))
    return out.reshape(BATCH, SEQ, D_MODEL)
```

```python
import functools

import jax
import jax.numpy as jnp
from jax import lax
from jax.experimental import pallas as pl
from jax.experimental.pallas import tpu as pltpu

D_MODEL = 1024
BATCH = 8
SEQ = 2048
TOKENS = BATCH * SEQ
CONV_WIDTH = 512
ATTN_WIDTH = 512
HEAD_DIM = 64
N_HEADS = 8
DILATED_PATTERNS = ((128, 1), (512, 4), (2048, 16))
N_PATTERNS = len(DILATED_PATTERNS)
ROPE_THETA = 500000.0
ROT_DIM = 16
ROT_HALF = ROT_DIM // 2
N_EXPERTS = 16
CAPACITY = 2 * SEQ // N_EXPERTS
D_FF = 2 * D_MODEL
PROJ_WIDTH = 3 * CONV_WIDTH + 3 * ATTN_WIDTH
RMS_EPS = 1e-6
NEG_INF = -1e30

LANES = 128
ROW_TILE = 512
FF_TILE = 512
BAND_HALF = 64
Q_CHUNK = 2 * BAND_HALF
K_WIN = 4 * BAND_HALF
HEADS_PER_STEP = LANES // HEAD_DIM
SUBLANES = 8
SLAB_ROWS = D_MODEL // LANES
TILE_PITCH = CAPACITY + SUBLANES
SLOT_SPLIT = 16
EXPERTS_PER_STEP = 4
MIB = 1024 * 1024

f32 = jnp.float32
bf16 = jnp.bfloat16


def _params(semantics, vmem_mib):
    return pltpu.CompilerParams(dimension_semantics=semantics, vmem_limit_bytes=vmem_mib * MIB)


def _rms(x, g):
    return x * lax.rsqrt(jnp.mean(x * x, axis=-1, keepdims=True) + RMS_EPS) * g


def _rope_table_kernel(pos_ref, invf_ref, cos_ref, s1_ref, s2_ref):
    ang = invf_ref[...] * pos_ref[...].astype(f32)
    trig = jnp.concatenate([jnp.cos(ang), jnp.sin(ang)], axis=0)
    lane = lax.broadcasted_iota(jnp.int32, (ROT_DIM, 3 * LANES), 1)
    row = lax.broadcasted_iota(jnp.int32, (ROT_DIM, 3 * LANES), 0)
    freq = row % ROT_HALF
    is_sin = row >= ROT_HALF
    at_lo = lane % HEAD_DIM == freq
    at_hi = lane % HEAD_DIM == freq + ROT_HALF
    table = lane // LANES
    place = (jnp.where((table == 0) & jnp.logical_not(is_sin) & (at_lo | at_hi), 1.0, 0.0)
             + jnp.where((table == 1) & is_sin & at_hi, 1.0, 0.0)
             - jnp.where((table == 2) & is_sin & at_lo, 1.0, 0.0)).astype(bf16)
    spread = jnp.zeros((SEQ, 3 * LANES), f32)
    rest = trig
    for _ in range(3):
        piece = rest.astype(bf16)
        rest = rest - piece.astype(f32)
        spread = spread + lax.dot_general(piece, place, (((0,), (0,)), ((), ())),
                                          preferred_element_type=f32)
    unrotated = jnp.where(
        lax.broadcasted_iota(jnp.int32, (1, LANES), 1) % HEAD_DIM >= ROT_DIM, 1.0, 0.0)
    cos_ref[...] = spread[:, :LANES] + unrotated
    s1_ref[...] = spread[:, LANES:2 * LANES]
    s2_ref[...] = spread[:, 2 * LANES:]


def _rope_tables(pos_row, invf_col):
    blk = pl.BlockSpec((SEQ, LANES), lambda b: (b, 0))
    return pl.pallas_call(
        _rope_table_kernel,
        grid=(BATCH,),
        in_specs=[
            pl.BlockSpec((None, 1, SEQ), lambda b: (b, 0, 0)),
            pl.BlockSpec((ROT_HALF, 1), lambda b: (0, 0)),
        ],
        out_specs=[blk, blk, blk],
        out_shape=[jax.ShapeDtypeStruct((TOKENS, LANES), f32)] * 3,
        compiler_params=_params(("arbitrary",), 32),
        name="rope_tables",
    )(pos_row, invf_col)


EDGE_U_FIRST, EDGE_U_LAST, EDGE_B_FIRST, EDGE_B_LAST = (k * SUBLANES for k in range(4))
EDGE_ROWS = 4 * SUBLANES


def _in_proj_kernel(x_ref, g_ref, w_ref, cw_ref, cos_ref, s1_ref, s2_ref,
                    yc_ref, edge_ref, qkv_ref, norm_ref):
    h = _rms(x_ref[...], g_ref[...]).astype(bf16)
    p = jnp.dot(h, w_ref[...], preferred_element_type=f32)
    gate_b = p[:, :CONV_WIDTH]
    u = p[:, CONV_WIDTH:2 * CONV_WIDTH] * p[:, 2 * CONV_WIDTH:3 * CONV_WIDTH]
    cw = cw_ref[...]
    row = lax.broadcasted_iota(jnp.int32, (ROW_TILE, CONV_WIDTH), 0)
    before = jnp.where(row == 0, 0.0, pltpu.roll(u, 1, 0))
    after = jnp.where(row == ROW_TILE - 1, 0.0, pltpu.roll(u, ROW_TILE - 1, 0))
    yc_ref[...] = gate_b * (cw[0:1] * before + cw[1:2] * u + cw[2:3] * after)
    last = ROW_TILE - SUBLANES
    edge_ref[pl.ds(EDGE_U_FIRST, SUBLANES), :] = u[:SUBLANES]
    edge_ref[pl.ds(EDGE_U_LAST, SUBLANES), :] = u[last:]
    edge_ref[pl.ds(EDGE_B_FIRST, SUBLANES), :] = gate_b[:SUBLANES] * cw[0:1]
    edge_ref[pl.ds(EDGE_B_LAST, SUBLANES), :] = gate_b[last:] * cw[2:3]
    cs, s1, s2 = cos_ref[...], s1_ref[...], s2_ref[...]
    head_of = jnp.where(lax.broadcasted_iota(jnp.int32, (LANES, LANES), 0) // HEAD_DIM
                        == lax.broadcasted_iota(jnp.int32, (LANES, LANES), 1), 1.0, 0.0).astype(bf16)
    assert 2 * ATTN_WIDTH // LANES == SUBLANES
    for blk in range(2 * ATTN_WIDTH // LANES):
        x = p[:, 3 * CONV_WIDTH + blk * LANES:3 * CONV_WIDTH + (blk + 1) * LANES]
        y = x * cs + pltpu.roll(x, ROT_HALF, 1) * s1 + pltpu.roll(x, LANES - ROT_HALF, 1) * s2
        if blk < ATTN_WIDTH // LANES:
            y = y * HEAD_DIM ** -0.5
        qkv_ref[:, blk * LANES:(blk + 1) * LANES] = y
        row_norms = jnp.dot((y * y).astype(bf16), head_of, preferred_element_type=f32)
        norm_ref[pl.ds(blk, 1), :] = jnp.max(row_norms, axis=0, keepdims=True)
    qkv_ref[:, 2 * ATTN_WIDTH:] = p[:, 3 * CONV_WIDTH + 2 * ATTN_WIDTH:]


def _in_proj(x2, g_mix, w_in_bf, conv_w, rope):
    n = TOKENS // ROW_TILE
    table = pl.BlockSpec((ROW_TILE, LANES), lambda i: (i, 0))
    return pl.pallas_call(
        _in_proj_kernel,
        grid=(n,),
        in_specs=[
            pl.BlockSpec((ROW_TILE, D_MODEL), lambda i: (i, 0)),
            pl.BlockSpec((1, D_MODEL), lambda i: (0, 0)),
            pl.BlockSpec((D_MODEL, PROJ_WIDTH), lambda i: (0, 0)),
            pl.BlockSpec((3, CONV_WIDTH), lambda i: (0, 0)),
            table, table, table,
        ],
        out_specs=[
            pl.BlockSpec((ROW_TILE, CONV_WIDTH), lambda i: (i, 0)),
            pl.BlockSpec((EDGE_ROWS, CONV_WIDTH), lambda i: (i, 0)),
            pl.BlockSpec((ROW_TILE, 3 * ATTN_WIDTH), lambda i: (i, 0)),
            pl.BlockSpec((SUBLANES, LANES), lambda i: (i, 0)),
        ],
        out_shape=[
            jax.ShapeDtypeStruct((TOKENS, CONV_WIDTH), f32),
            jax.ShapeDtypeStruct((n * EDGE_ROWS, CONV_WIDTH), f32),
            jax.ShapeDtypeStruct((TOKENS, 3 * ATTN_WIDTH), f32),
            jax.ShapeDtypeStruct((n * SUBLANES, LANES), f32),
        ],
        compiler_params=_params(("arbitrary",), 48),
        name="in_proj",
    )(x2, g_mix, w_in_bf, conv_w, *rope)


CAP_INTERIOR, CAP_FIRST, CAP_LAST, CAP_SINGLE = range(4)
MASK_OPEN = float(jnp.finfo(jnp.float32).max)
MID_DIL = DILATED_PATTERNS[1][1]
MID_LEN = SEQ // MID_DIL
assert [d for _, d in DILATED_PATTERNS] == [1, MID_DIL, MID_DIL * MID_DIL]
assert all(w // 2 // d == BAND_HALF for w, d in DILATED_PATTERNS)


SAFE_SCORE = 40.0
NORM_SLACK = 1.1


def _attn_kernel(q_ref, k_ref, v_ref, o_ref, cap_ref, q4_ref, k4_ref, v4_ref,
                 qlo_ref, qhi_ref, kpad_ref, vlo_ref, vhi_ref,
                 num_ref, m_ref, l_ref, *, exact_max):
    @pl.when(pl.program_id(1) == 0)
    def _():
        row = lax.broadcasted_iota(jnp.int32, (Q_CHUNK, K_WIN), 0)
        col = lax.broadcasted_iota(jnp.int32, (Q_CHUNK, K_WIN), 1)
        band = (col - row >= 0) & (col - row <= 2 * BAND_HALF)
        not_before = col >= BAND_HALF
        not_after = col < K_WIN - BAND_HALF
        for idx, ok in ((CAP_INTERIOR, band), (CAP_FIRST, band & not_before),
                        (CAP_LAST, band & not_after), (CAP_SINGLE, band & not_before & not_after)):
            cap_ref[idx] = jnp.where(ok, MASK_OPEN, NEG_INF)

    lane_lo = lax.broadcasted_iota(jnp.int32, (Q_CHUNK, LANES), 1) < HEAD_DIM
    zero_pad = jnp.zeros((BAND_HALF, LANES), bf16)

    def zero_rows(start):
        for ref in (kpad_ref, vlo_ref, vhi_ref):
            ref[pl.ds(start, BAND_HALF), :] = zero_pad

    def stage(sources, src, q_row, pad_row, keep=None):
        qf, kf, vf = (ref[src, :] for ref in sources)
        if keep is not None:
            for ref, val in zip((q4_ref, k4_ref, v4_ref), (qf, kf, vf)):
                ref[keep, :] = val
        q_dst, p_dst = pl.ds(q_row, Q_CHUNK), pl.ds(pad_row, Q_CHUNK)
        qlo_ref[q_dst, :] = jnp.where(lane_lo, qf, 0.0).astype(bf16)
        qhi_ref[q_dst, :] = jnp.where(lane_lo, 0.0, qf).astype(bf16)
        kpad_ref[p_dst, :] = kf.astype(bf16)
        vlo_ref[p_dst, :] = jnp.where(lane_lo, vf, 0.0).astype(bf16)
        vhi_ref[p_dst, :] = jnp.where(lane_lo, 0.0, vf).astype(bf16)

    def chunk(pat, out_rows, q_row, pad_row, cap):
        q2 = jnp.concatenate([qlo_ref[pl.ds(q_row, Q_CHUNK), :], qhi_ref[pl.ds(q_row, Q_CHUNK), :]],
                             axis=0)
        s = lax.dot_general(q2, kpad_ref[pl.ds(pad_row, K_WIN), :], (((1,), (1,)), ((), ())),
                            preferred_element_type=f32)
        caps = cap_ref[cap]
        s = jnp.minimum(s, jnp.concatenate([caps, caps], axis=0))
        if exact_max:
            m = jnp.max(s, axis=1, keepdims=True)
            p = jnp.exp(s - m)
        else:
            p = jnp.exp(s)
        l = jnp.sum(p, axis=1, keepdims=True)
        pb = p.astype(bf16)
        num = (jnp.dot(pb[:Q_CHUNK], vlo_ref[pl.ds(pad_row, K_WIN), :], preferred_element_type=f32)
               + jnp.dot(pb[Q_CHUNK:], vhi_ref[pl.ds(pad_row, K_WIN), :], preferred_element_type=f32))
        num_ref[pat, out_rows, :] = num
        if exact_max:
            m_ref[pat, out_rows, :] = jnp.where(lane_lo, m[:Q_CHUNK], m[Q_CHUNK:])
        l_ref[pat, out_rows, :] = jnp.where(lane_lo, l[:Q_CHUNK], l[Q_CHUNK:])

    def run_pattern(pat, sources, classes, length, keep_f32=False):
        n_chunks = length // Q_CHUNK
        region = length + 2 * BAND_HALF

        def rows(first, stride, i):
            if stride == 1:
                return pl.ds(first + i * Q_CHUNK, Q_CHUNK)
            return pl.ds(first + stride * Q_CHUNK * i, Q_CHUNK, stride=stride)

        for u in range(len(classes)):
            zero_rows(u * region)
            zero_rows(u * region + BAND_HALF + length)
        for u, (src0, src_stride, _, _) in enumerate(classes):
            for i in range(n_chunks):
                keep = pl.ds(u * length + i * Q_CHUNK, Q_CHUNK) if keep_f32 else None
                stage(sources, rows(src0, src_stride, i), u * length + i * Q_CHUNK,
                      u * region + BAND_HALF + i * Q_CHUNK, keep)
        for u, (_, _, out0, out_stride) in enumerate(classes):
            for j in range(n_chunks):
                if n_chunks == 1:
                    cap = CAP_SINGLE
                else:
                    cap = CAP_FIRST if j == 0 else CAP_LAST if j == n_chunks - 1 else CAP_INTERIOR
                chunk(pat, rows(out0, out_stride, j), u * length + j * Q_CHUNK,
                      u * region + j * Q_CHUNK, cap)

    inputs = (q_ref, k_ref, v_ref)
    run_pattern(0, inputs, [(0, 1, 0, 1)], SEQ)
    run_pattern(1, inputs, [(r, MID_DIL, r * MID_LEN, 1) for r in range(MID_DIL)], MID_LEN,
                keep_f32=True)
    run_pattern(2, (q4_ref, k4_ref, v4_ref),
                [(r * MID_LEN + a, MID_DIL, r * MID_LEN + a, MID_DIL)
                 for r in range(MID_DIL) for a in range(MID_DIL)], MID_LEN // MID_DIL)

    def mix(blk, c):
        mid_rows = pl.ds(pl.multiple_of(blk * Q_CHUNK, Q_CHUNK), Q_CHUNK)
        per_class = MID_LEN // Q_CHUNK
        pos_rows = pl.ds(blk // per_class + MID_DIL * Q_CHUNK * (blk % per_class), Q_CHUNK,
                         stride=MID_DIL)
        rows = (pos_rows, mid_rows, mid_rows)
        num = jnp.zeros((Q_CHUNK, LANES), f32)
        den = jnp.zeros((Q_CHUNK, LANES), f32)
        if exact_max:
            ms = [m_ref[p, rows[p], :] for p in range(N_PATTERNS)]
            m_all = functools.reduce(jnp.maximum, ms)
        for p in range(N_PATTERNS):
            a = jnp.exp(ms[p] - m_all) if exact_max else 1.0
            den = den + a * l_ref[p, rows[p], :]
            num = num + a * num_ref[p, rows[p], :]
        o_ref[pos_rows, :] = num / den
        return c
    lax.fori_loop(0, SEQ // Q_CHUNK, mix, 0)


def _attention(qkv3, exact_max):
    n_hp = N_HEADS // HEADS_PER_STEP
    blk = lambda off: pl.BlockSpec((None, SEQ, LANES), lambda b, h, off=off: (b, 0, off + h))
    pad_rows = max(SEQ + 2 * BAND_HALF * d for _, d in DILATED_PATTERNS)
    stats = pltpu.VMEM((N_PATTERNS, SEQ, LANES), f32)
    rows_f32 = pltpu.VMEM((SEQ, LANES), f32)
    return pl.pallas_call(
        functools.partial(_attn_kernel, exact_max=exact_max),
        grid=(BATCH, n_hp),
        in_specs=[blk(0), blk(n_hp), blk(2 * n_hp)],
        out_specs=pl.BlockSpec((None, SEQ, LANES), lambda b, h: (b, 0, h)),
        out_shape=jax.ShapeDtypeStruct((BATCH, SEQ, ATTN_WIDTH), f32),
        scratch_shapes=[
            pltpu.VMEM((4, Q_CHUNK, K_WIN), f32),
            rows_f32, rows_f32, rows_f32,
            pltpu.VMEM((SEQ, LANES), bf16), pltpu.VMEM((SEQ, LANES), bf16),
            pltpu.VMEM((pad_rows, LANES), bf16), pltpu.VMEM((pad_rows, LANES), bf16),
            pltpu.VMEM((pad_rows, LANES), bf16),
            stats, stats, stats,
        ],
        compiler_params=_params(("arbitrary", "arbitrary"), 48),
        name="attention_exact" if exact_max else "attention",
    )(qkv3, qkv3, qkv3)


def _store_token_major(ref, x):
    for j in range(SLAB_ROWS):
        ref[pl.ds(j, x.shape[0], stride=SLAB_ROWS), :] = x[:, j * LANES:(j + 1) * LANES]


def _dot_nt(a, b):
    return lax.dot_general(a, b, (((1,), (1,)), ((), ())), preferred_element_type=f32)


def _out_proj_kernel(yc_ref, edge_ref, edge_prev_ref, edge_next_ref, ya_ref, x_ref, w_ref,
                     gc_ref, ga_ref, gf_ref, wr_ref, x1_ref, h2_ref, lg_ref):
    tile = pl.program_id(0) % (SEQ // ROW_TILE)
    from_prev = (edge_ref[pl.ds(EDGE_B_FIRST, 1), :]
                 * edge_prev_ref[pl.ds(EDGE_U_LAST + SUBLANES - 1, 1), :])
    from_next = (edge_ref[pl.ds(EDGE_B_LAST + SUBLANES - 1, 1), :]
                 * edge_next_ref[pl.ds(EDGE_U_FIRST, 1), :])
    from_prev = jnp.where(tile == 0, 0.0, from_prev)
    from_next = jnp.where(tile == SEQ // ROW_TILE - 1, 0.0, from_next)
    row = lax.broadcasted_iota(jnp.int32, (ROW_TILE, CONV_WIDTH), 0)
    y_conv = (yc_ref[...] + jnp.where(row == 0, from_prev, 0.0)
              + jnp.where(row == ROW_TILE - 1, from_next, 0.0))
    yc = _rms(y_conv, gc_ref[...]).astype(bf16)
    ya = _rms(ya_ref[...], ga_ref[...]).astype(bf16)
    mix = (jnp.dot(yc, w_ref[pl.ds(0, CONV_WIDTH), :], preferred_element_type=f32)
           + jnp.dot(ya, w_ref[pl.ds(CONV_WIDTH, ATTN_WIDTH), :], preferred_element_type=f32))
    x1 = x_ref[...] + mix
    _store_token_major(x1_ref, x1)
    h2 = _rms(x1, gf_ref[...])
    _store_token_major(h2_ref, h2)
    wr = wr_ref[...]
    wr_hi = wr.astype(bf16)
    wr_lo = (wr - wr_hi.astype(f32)).astype(bf16)
    h2_hi = h2.astype(bf16)
    h2_lo = (h2 - h2_hi.astype(f32)).astype(bf16)
    both = _dot_nt(jnp.concatenate([wr_hi, wr_lo], axis=0), h2_hi)
    lg_ref[...] = both[:N_EXPERTS] + both[N_EXPERTS:] + _dot_nt(wr_hi, h2_lo)


def _out_proj(yc, edge, ya, x2, w_out_bf, g_conv, g_attn, g_ffn, w_router_t):
    n = TOKENS // ROW_TILE
    slab = pl.BlockSpec((ROW_TILE * SLAB_ROWS, LANES), lambda i: (i, 0))
    slab_shape = jax.ShapeDtypeStruct((TOKENS * SLAB_ROWS, LANES), f32)
    return pl.pallas_call(
        _out_proj_kernel,
        grid=(n,),
        in_specs=[
            pl.BlockSpec((ROW_TILE, CONV_WIDTH), lambda i: (i, 0)),
            pl.BlockSpec((EDGE_ROWS, CONV_WIDTH), lambda i: (i, 0)),
            pl.BlockSpec((EDGE_ROWS, CONV_WIDTH), lambda i: (jnp.maximum(i - 1, 0), 0)),
            pl.BlockSpec((EDGE_ROWS, CONV_WIDTH), lambda i: (jnp.minimum(i + 1, n - 1), 0)),
            pl.BlockSpec((ROW_TILE, ATTN_WIDTH), lambda i: (i, 0)),
            pl.BlockSpec((ROW_TILE, D_MODEL), lambda i: (i, 0)),
            pl.BlockSpec((D_MODEL, D_MODEL), lambda i: (0, 0)),
            pl.BlockSpec((1, CONV_WIDTH), lambda i: (0, 0)),
            pl.BlockSpec((1, ATTN_WIDTH), lambda i: (0, 0)),
            pl.BlockSpec((1, D_MODEL), lambda i: (0, 0)),
            pl.BlockSpec((N_EXPERTS, D_MODEL), lambda i: (0, 0)),
        ],
        out_specs=[slab, slab, pl.BlockSpec((N_EXPERTS, ROW_TILE), lambda i: (0, i))],
        out_shape=[slab_shape, slab_shape, jax.ShapeDtypeStruct((N_EXPERTS, TOKENS), f32)],
        compiler_params=_params(("arbitrary",), 48),
        name="out_proj",
    )(yc, edge, edge, edge, ya, x2, w_out_bf, g_conv, g_attn, g_ffn, w_router_t)


N_LANE_BLOCKS = SEQ // LANES


def _excl_cumsum_tokens(x, tri):
    n = x.shape[0]
    stacked = jnp.concatenate([x[:, j * LANES:(j + 1) * LANES] for j in range(N_LANE_BLOCKS)], axis=0)
    within = jnp.dot(stacked.astype(bf16), tri, preferred_element_type=f32)
    totals = jnp.sum(stacked, axis=1, keepdims=True)
    out, offset = [], jnp.zeros((n, 1), f32)
    for j in range(N_LANE_BLOCKS):
        rows = slice(j * n, (j + 1) * n)
        out.append(within[rows] + offset)
        offset = offset + totals[rows]
    return jnp.concatenate(out, axis=1)


def _select_kernel(lg_ref, rank_ref, aff_ref):
    affs = []
    for b in range(BATCH):
        x = lg_ref[:, b * SEQ:(b + 1) * SEQ]
        e = jnp.exp(x - jnp.max(x, axis=0, keepdims=True))
        affs.append(e / jnp.sum(e, axis=0, keepdims=True))
    aff = jnp.concatenate(affs, axis=0)
    aff_ref[...] = aff
    thr = jnp.zeros((BATCH * N_EXPERTS, 1), jnp.int32)
    for bit in range(30, -1, -1):
        cand = thr | (1 << bit)
        hit = aff >= pltpu.bitcast(cand, f32)
        cnt = jnp.sum(jnp.where(hit, 1.0, 0.0), axis=1, keepdims=True)
        thr = jnp.where(cnt >= float(CAPACITY), cand, thr)
    above = aff >= pltpu.bitcast(thr + 1, f32)
    tie = (aff >= pltpu.bitcast(thr, f32)) & jnp.logical_not(above)
    need = float(CAPACITY) - jnp.sum(jnp.where(above, 1.0, 0.0), axis=1, keepdims=True)
    tri = jnp.where(lax.broadcasted_iota(jnp.int32, (LANES, LANES), 0)
                    < lax.broadcasted_iota(jnp.int32, (LANES, LANES), 1), 1.0, 0.0).astype(bf16)
    tie_rank = _excl_cumsum_tokens(jnp.where(tie, 1.0, 0.0), tri)
    sel = above | (tie & (tie_rank < need))
    rank_ref[...] = jnp.where(sel, _excl_cumsum_tokens(jnp.where(sel, 1.0, 0.0), tri), -1.0)


def _select(logits_t):
    shape = jax.ShapeDtypeStruct((BATCH * N_EXPERTS, SEQ), f32)
    blk = pl.BlockSpec((BATCH * N_EXPERTS, SEQ), lambda i: (0, 0))
    return pl.pallas_call(
        _select_kernel,
        grid=(1,),
        in_specs=[pl.BlockSpec((N_EXPERTS, TOKENS), lambda i: (0, 0))],
        out_specs=[blk, blk],
        out_shape=[shape, shape],
        compiler_params=_params(("arbitrary",), 40),
        name="select",
    )(logits_t)


def _route_kernel(rank_ref, aff_ref, idx_ref, gate_ref):
    rank = rank_ref[...]
    aff = aff_ref[...]
    high = jnp.floor(rank * (1.0 / SLOT_SPLIT))
    low = rank - high * SLOT_SPLIT
    digit = lax.broadcasted_iota(jnp.int32, (SLOT_SPLIT, SEQ), 0).astype(f32)
    tok = lax.broadcasted_iota(jnp.int32, (1, SEQ), 1)
    g1 = aff.astype(bf16).astype(f32)
    g2 = (aff - g1).astype(bf16).astype(f32)
    g3 = aff - g1 - g2
    per_expert = lambda v: jnp.broadcast_to(v.astype(f32), (N_EXPERTS, SEQ))
    payloads = [per_expert(tok // LANES), per_expert(tok % LANES), g1, g2, g3]
    high_rows = jnp.concatenate(
        [jnp.where(high[e:e + 1] == digit, 1.0, 0.0) for e in range(N_EXPERTS)], axis=0).astype(bf16)
    low_hit = [low[e:e + 1] == digit for e in range(N_EXPERTS)]
    n_rows = N_EXPERTS * SLOT_SPLIT
    same_expert = (lax.broadcasted_iota(jnp.int32, (n_rows, n_rows), 0) // SLOT_SPLIT
                   == lax.broadcasted_iota(jnp.int32, (n_rows, n_rows), 1) // SLOT_SPLIT)
    fold = jnp.where(lax.broadcasted_iota(jnp.int32, (n_rows, SLOT_SPLIT), 0) % SLOT_SPLIT
                     == lax.broadcasted_iota(jnp.int32, (n_rows, SLOT_SPLIT), 1), 1.0, 0.0).astype(bf16)
    folded = []
    for val in payloads:
        rows = jnp.concatenate(
            [jnp.where(low_hit[e], val[e:e + 1], 0.0) for e in range(N_EXPERTS)], axis=0).astype(bf16)
        picked = jnp.where(same_expert, _dot_nt(high_rows, rows), 0.0).astype(bf16)
        folded.append(jnp.dot(picked, fold, preferred_element_type=f32))
    idx_ref[...] = ((folded[0] * float(LANES) + folded[1]) * float(SLAB_ROWS)).astype(jnp.int32)
    gate_ref[...] = folded[2] + folded[3] + folded[4]


def _route(rank, aff):
    n_rows = N_EXPERTS * SLOT_SPLIT
    blk = pl.BlockSpec((None, n_rows, CAPACITY // SLOT_SPLIT), lambda b: (b, 0, 0))
    shape = (BATCH, n_rows, CAPACITY // SLOT_SPLIT)
    per_seq = pl.BlockSpec((N_EXPERTS, SEQ), lambda b: (b, 0))
    return pl.pallas_call(
        _route_kernel,
        grid=(BATCH,),
        in_specs=[per_seq, per_seq],
        out_specs=[blk, blk],
        out_shape=[jax.ShapeDtypeStruct(shape, jnp.int32), jax.ShapeDtypeStruct(shape, f32)],
        compiler_params=_params(("arbitrary",), 40),
        name="route",
    )(rank, aff)


def _slot_list_spec():
    return pl.BlockSpec((None, 1, EXPERTS_PER_STEP * CAPACITY), lambda b, s: (b, 0, s),
                        memory_space=pltpu.SMEM)


def _gather_kernel(row_ref, h2_ref, xe_ref, tile_ref):
    for k in range(EXPERTS_PER_STEP):
        for slot in range(CAPACITY):
            row = row_ref[0, k * CAPACITY + slot]
            slab = h2_ref[pl.ds(pl.multiple_of(row, SLAB_ROWS), SLAB_ROWS), :]
            tile_ref[k, pl.ds(slot, SLAB_ROWS, stride=TILE_PITCH), :] = slab
        for j in range(SLAB_ROWS):
            xe_ref[k, :, j * LANES:(j + 1) * LANES] = (
                tile_ref[k, pl.ds(j * TILE_PITCH, CAPACITY), :].astype(bf16))


def _gather(rows, h2_slab):
    return pl.pallas_call(
        _gather_kernel,
        grid=(BATCH, N_EXPERTS // EXPERTS_PER_STEP),
        in_specs=[
            _slot_list_spec(),
            pl.BlockSpec((SEQ * SLAB_ROWS, LANES), lambda b, s: (b, 0)),
        ],
        out_specs=pl.BlockSpec((EXPERTS_PER_STEP, None, CAPACITY, D_MODEL),
                               lambda b, s: (s, b, 0, 0)),
        out_shape=jax.ShapeDtypeStruct((N_EXPERTS, BATCH, CAPACITY, D_MODEL), bf16),
        scratch_shapes=[pltpu.VMEM((EXPERTS_PER_STEP, SLAB_ROWS * TILE_PITCH, LANES), f32)],
        compiler_params=_params(("arbitrary", "arbitrary"), 48),
        name="gather",
    )(rows, h2_slab)


def _ffn_kernel(xe_ref, wg_ref, wu_ref, wd_ref, y_ref, acc_ref):
    f = pl.program_id(1)

    last = D_FF // FF_TILE - 1

    def partial_out():
        xe = xe_ref[...]
        a = jnp.dot(xe, wg_ref[...].astype(bf16), preferred_element_type=f32)
        u = jnp.dot(xe, wu_ref[...].astype(bf16), preferred_element_type=f32)
        hidden = (a * (1.0 / (1.0 + jnp.exp(-a))) * u).astype(bf16)
        return jnp.dot(hidden, wd_ref[...].astype(bf16), preferred_element_type=f32)

    @pl.when(f == 0)
    def _():
        acc_ref[...] = partial_out()

    @pl.when((f > 0) & (f < last))
    def _():
        acc_ref[...] += partial_out()

    @pl.when(f == last)
    def _():
        y_ref[...] = (acc_ref[...] + partial_out()).astype(bf16)


def _ffn(xe, w_gate, w_up, w_down):
    rows = BATCH * CAPACITY
    return pl.pallas_call(
        _ffn_kernel,
        grid=(N_EXPERTS, D_FF // FF_TILE),
        in_specs=[
            pl.BlockSpec((None, rows, D_MODEL), lambda e, f: (e, 0, 0)),
            pl.BlockSpec((None, D_MODEL, FF_TILE), lambda e, f: (e, 0, f)),
            pl.BlockSpec((None, D_MODEL, FF_TILE), lambda e, f: (e, 0, f)),
            pl.BlockSpec((None, FF_TILE, D_MODEL), lambda e, f: (e, f, 0)),
        ],
        out_specs=pl.BlockSpec((None, rows, D_MODEL), lambda e, f: (e, 0, 0)),
        out_shape=jax.ShapeDtypeStruct((N_EXPERTS, rows, D_MODEL), bf16),
        scratch_shapes=[pltpu.VMEM((rows, D_MODEL), f32)],
        compiler_params=_params(("arbitrary", "arbitrary"), 56),
        name="ffn",
    )(xe, w_gate, w_up, w_down)


SCATTER_UNROLL = 8
COPY_ROWS = 2048
NORM_TOKENS = 256


def _combine_kernel(row_ref, gate_ref, y_ref, x1_ref, g_ref, o_ref, acc_ref, tile_ref):
    step = pl.program_id(1)

    @pl.when(step == 0)
    def _():
        def copy(i, c):
            rows = pl.ds(pl.multiple_of(i * COPY_ROWS, COPY_ROWS), COPY_ROWS)
            acc_ref[rows, :] = x1_ref[rows, :]
            return c
        lax.fori_loop(0, SEQ * SLAB_ROWS // COPY_ROWS, copy, 0)

    for k in range(EXPERTS_PER_STEP):
        for j in range(SLAB_ROWS):
            tile_ref[k, pl.ds(j * TILE_PITCH, CAPACITY), :] = (
                y_ref[k, :, j * LANES:(j + 1) * LANES].astype(f32))
        for first in range(0, CAPACITY, SCATTER_UNROLL):
            updates = []
            for slot in range(first, first + SCATTER_UNROLL):
                row = row_ref[0, k * CAPACITY + slot]
                gate = gate_ref[0, k * CAPACITY + slot]
                rows = pl.ds(pl.multiple_of(row, SLAB_ROWS), SLAB_ROWS)
                contrib = tile_ref[k, pl.ds(slot, SLAB_ROWS, stride=TILE_PITCH), :]
                updates.append((rows, acc_ref[rows, :] + gate * contrib))
            for rows, value in updates:
                acc_ref[rows, :] = value

    @pl.when(step == N_EXPERTS // EXPERTS_PER_STEP - 1)
    def _():
        def norm(i, c):
            first = i * (NORM_TOKENS * SLAB_ROWS)
            chunks = [acc_ref[pl.ds(first + j, NORM_TOKENS, stride=SLAB_ROWS), :]
                      for j in range(SLAB_ROWS)]
            squares = functools.reduce(lambda a, b: a + b, [ch * ch for ch in chunks])
            inv = lax.rsqrt(jnp.sum(squares, axis=1, keepdims=True) * (1.0 / D_MODEL) + RMS_EPS)
            rows = pl.ds(pl.multiple_of(i * NORM_TOKENS, NORM_TOKENS), NORM_TOKENS)
            for j in range(SLAB_ROWS):
                cols = slice(j * LANES, (j + 1) * LANES)
                o_ref[rows, cols] = chunks[j] * inv * g_ref[:, cols]
            return c
        lax.fori_loop(0, SEQ // NORM_TOKENS, norm, 0)


def _combine(rows, gate, y, x1_slab, g_final):
    return pl.pallas_call(
        _combine_kernel,
        grid=(BATCH, N_EXPERTS // EXPERTS_PER_STEP),
        in_specs=[
            _slot_list_spec(), _slot_list_spec(),
            pl.BlockSpec((EXPERTS_PER_STEP, None, CAPACITY, D_MODEL), lambda b, s: (s, b, 0, 0)),
            pl.BlockSpec((SEQ * SLAB_ROWS, LANES), lambda b, s: (b, 0)),
            pl.BlockSpec((1, D_MODEL), lambda b, s: (0, 0)),
        ],
        out_specs=pl.BlockSpec((SEQ, D_MODEL), lambda b, s: (b, 0)),
        out_shape=jax.ShapeDtypeStruct((TOKENS, D_MODEL), f32),
        scratch_shapes=[pltpu.VMEM((SEQ * SLAB_ROWS, LANES), f32),
                        pltpu.VMEM((EXPERTS_PER_STEP, SLAB_ROWS * TILE_PITCH, LANES), f32)],
        compiler_params=_params(("arbitrary", "arbitrary"), 56),
        name="combine",
    )(rows, gate, y, x1_slab, g_final)


def _rope_freqs():
    inv_freq = ROPE_THETA ** (-jnp.arange(ROT_HALF, dtype=f32) / ROT_HALF)
    return inv_freq.reshape(ROT_HALF, 1)


def kernel(x, positions, g_mix, w_in, conv_w, g_conv_out, g_attn_out, w_out, g_ffn, w_router,
           w_gate, w_up, w_down, g_final):
    x2 = x.reshape(TOKENS, D_MODEL)
    rope = _rope_tables(positions.reshape(BATCH, 1, SEQ), _rope_freqs())
    yc, edge, qkv, norms = _in_proj(x2, g_mix[0].reshape(1, D_MODEL), w_in[0].astype(bf16),
                                    conv_w[0], rope)
    norms = norms.reshape(BATCH, SEQ // ROW_TILE, 2, ATTN_WIDTH // LANES, LANES)
    norms = norms[..., :HEADS_PER_STEP].max(axis=1)
    scores_bounded = jnp.all(norms[:, 0] * norms[:, 1] * NORM_SLACK <= SAFE_SCORE ** 2)
    qkv3 = qkv.reshape(BATCH, SEQ, 3 * ATTN_WIDTH)
    ya = lax.cond(scores_bounded, functools.partial(_attention, exact_max=False),
                  functools.partial(_attention, exact_max=True), qkv3)
    x1, h2, logits_t = _out_proj(yc, edge, ya.reshape(TOKENS, ATTN_WIDTH), x2, w_out[0].astype(bf16),
                                 g_conv_out[0].reshape(1, CONV_WIDTH),
                                 g_attn_out[0].reshape(1, ATTN_WIDTH), g_ffn[0].reshape(1, D_MODEL),
                                 w_router[0].T)
    idx, gate = _route(*_select(logits_t))
    idx = idx.reshape(BATCH, 1, N_EXPERTS * CAPACITY)
    gate = gate.reshape(BATCH, 1, N_EXPERTS * CAPACITY)
    xe = _gather(idx, h2)
    y = _ffn(xe.reshape(N_EXPERTS, BATCH * CAPACITY, D_MODEL), w_gate[0], w_up[0], w_down[0])
    out = _combine(idx, gate, y.reshape(N_EXPERTS, BATCH, CAPACITY, D_MODEL), x1,
                   g_final.reshape(1, D_MODEL))
    return out.reshape(BATCH, SEQ, D_MODEL)
```

```python
import functools

import jax
import jax.numpy as jnp
from jax import lax
from jax.experimental import pallas as pl
from jax.experimental.pallas import tpu as pltpu

D_MODEL = 1024
BATCH = 8
SEQ = 2048
TOKENS = BATCH * SEQ
CONV_WIDTH = 512
ATTN_WIDTH = 512
HEAD_DIM = 64
N_HEADS = 8
DILATED_PATTERNS = ((128, 1), (512, 4), (2048, 16))
N_PATTERNS = len(DILATED_PATTERNS)
ROPE_THETA = 500000.0
ROT_DIM = 16
ROT_HALF = ROT_DIM // 2
N_EXPERTS = 16
CAPACITY = 2 * SEQ // N_EXPERTS
D_FF = 2 * D_MODEL
PROJ_WIDTH = 3 * CONV_WIDTH + 3 * ATTN_WIDTH
RMS_EPS = 1e-6
NEG_INF = -1e30

LANES = 128
ROW_TILE = 1024
FF_TILE = 512
BAND_HALF = 64
Q_CHUNK = 2 * BAND_HALF
K_WIN = 4 * BAND_HALF
HEADS_PER_STEP = LANES // HEAD_DIM
SUBLANES = 8
SLAB_ROWS = D_MODEL // LANES
TILE_PITCH = CAPACITY + SUBLANES
SLOT_SPLIT = 16
EXPERTS_PER_STEP = 4
MIB = 1024 * 1024

f32 = jnp.float32
bf16 = jnp.bfloat16


def _params(semantics, vmem_mib):
    return pltpu.CompilerParams(dimension_semantics=semantics, vmem_limit_bytes=vmem_mib * MIB)


def _rms(x, g):
    return x * lax.rsqrt(jnp.mean(x * x, axis=-1, keepdims=True) + RMS_EPS) * g


def _rope_table_kernel(pos_ref, invf_ref, cos_ref, s1_ref, s2_ref):
    ang = invf_ref[...] * pos_ref[...].astype(f32)
    trig = jnp.concatenate([jnp.cos(ang), jnp.sin(ang)], axis=0)
    lane = lax.broadcasted_iota(jnp.int32, (ROT_DIM, 3 * LANES), 1)
    row = lax.broadcasted_iota(jnp.int32, (ROT_DIM, 3 * LANES), 0)
    freq = row % ROT_HALF
    is_sin = row >= ROT_HALF
    at_lo = lane % HEAD_DIM == freq
    at_hi = lane % HEAD_DIM == freq + ROT_HALF
    table = lane // LANES
    place = (jnp.where((table == 0) & jnp.logical_not(is_sin) & (at_lo | at_hi), 1.0, 0.0)
             + jnp.where((table == 1) & is_sin & at_hi, 1.0, 0.0)
             - jnp.where((table == 2) & is_sin & at_lo, 1.0, 0.0)).astype(bf16)
    spread = jnp.zeros((SEQ, 3 * LANES), f32)
    rest = trig
    for _ in range(3):
        piece = rest.astype(bf16)
        rest = rest - piece.astype(f32)
        spread = spread + lax.dot_general(piece, place, (((0,), (0,)), ((), ())),
                                          preferred_element_type=f32)
    unrotated = jnp.where(
        lax.broadcasted_iota(jnp.int32, (1, LANES), 1) % HEAD_DIM >= ROT_DIM, 1.0, 0.0)
    cos_ref[...] = spread[:, :LANES] + unrotated
    s1_ref[...] = spread[:, LANES:2 * LANES]
    s2_ref[...] = spread[:, 2 * LANES:]


def _rope_tables(pos_row, invf_col):
    blk = pl.BlockSpec((SEQ, LANES), lambda b: (b, 0))
    return pl.pallas_call(
        _rope_table_kernel,
        grid=(BATCH,),
        in_specs=[
            pl.BlockSpec((None, 1, SEQ), lambda b: (b, 0, 0)),
            pl.BlockSpec((ROT_HALF, 1), lambda b: (0, 0)),
        ],
        out_specs=[blk, blk, blk],
        out_shape=[jax.ShapeDtypeStruct((TOKENS, LANES), f32)] * 3,
        compiler_params=_params(("arbitrary",), 32),
        name="rope_tables",
    )(pos_row, invf_col)


EDGE_U_FIRST, EDGE_U_LAST, EDGE_B_FIRST, EDGE_B_LAST = (k * SUBLANES for k in range(4))
EDGE_ROWS = 4 * SUBLANES


def _in_proj_kernel(x_ref, g_ref, w_ref, cw_ref, cos_ref, s1_ref, s2_ref,
                    yc_ref, edge_ref, qkv_ref, norm_ref):
    h = _rms(x_ref[...], g_ref[...]).astype(bf16)
    p = jnp.dot(h, w_ref[...], preferred_element_type=f32)
    gate_b = p[:, :CONV_WIDTH]
    u = p[:, CONV_WIDTH:2 * CONV_WIDTH] * p[:, 2 * CONV_WIDTH:3 * CONV_WIDTH]
    cw = cw_ref[...]
    row = lax.broadcasted_iota(jnp.int32, (ROW_TILE, CONV_WIDTH), 0)
    before = jnp.where(row == 0, 0.0, pltpu.roll(u, 1, 0))
    after = jnp.where(row == ROW_TILE - 1, 0.0, pltpu.roll(u, ROW_TILE - 1, 0))
    yc_ref[...] = gate_b * (cw[0:1] * before + cw[1:2] * u + cw[2:3] * after)
    last = ROW_TILE - SUBLANES
    edge_ref[pl.ds(EDGE_U_FIRST, SUBLANES), :] = u[:SUBLANES]
    edge_ref[pl.ds(EDGE_U_LAST, SUBLANES), :] = u[last:]
    edge_ref[pl.ds(EDGE_B_FIRST, SUBLANES), :] = gate_b[:SUBLANES] * cw[0:1]
    edge_ref[pl.ds(EDGE_B_LAST, SUBLANES), :] = gate_b[last:] * cw[2:3]
    cs, s1, s2 = cos_ref[...], s1_ref[...], s2_ref[...]
    head_of = jnp.where(lax.broadcasted_iota(jnp.int32, (LANES, LANES), 0) // HEAD_DIM
                        == lax.broadcasted_iota(jnp.int32, (LANES, LANES), 1), 1.0, 0.0).astype(bf16)
    assert 2 * ATTN_WIDTH // LANES == SUBLANES
    for blk in range(2 * ATTN_WIDTH // LANES):
        x = p[:, 3 * CONV_WIDTH + blk * LANES:3 * CONV_WIDTH + (blk + 1) * LANES]
        y = x * cs + pltpu.roll(x, ROT_HALF, 1) * s1 + pltpu.roll(x, LANES - ROT_HALF, 1) * s2
        if blk < ATTN_WIDTH // LANES:
            y = y * HEAD_DIM ** -0.5
        qkv_ref[:, blk * LANES:(blk + 1) * LANES] = y
        row_norms = jnp.dot((y * y).astype(bf16), head_of, preferred_element_type=f32)
        norm_ref[pl.ds(blk, 1), :] = jnp.max(row_norms, axis=0, keepdims=True)
    qkv_ref[:, 2 * ATTN_WIDTH:] = p[:, 3 * CONV_WIDTH + 2 * ATTN_WIDTH:]


def _in_proj(x2, g_mix, w_in_bf, conv_w, rope):
    n = TOKENS // ROW_TILE
    table = pl.BlockSpec((ROW_TILE, LANES), lambda i: (i, 0))
    return pl.pallas_call(
        _in_proj_kernel,
        grid=(n,),
        in_specs=[
            pl.BlockSpec((ROW_TILE, D_MODEL), lambda i: (i, 0)),
            pl.BlockSpec((1, D_MODEL), lambda i: (0, 0)),
            pl.BlockSpec((D_MODEL, PROJ_WIDTH), lambda i: (0, 0)),
            pl.BlockSpec((3, CONV_WIDTH), lambda i: (0, 0)),
            table, table, table,
        ],
        out_specs=[
            pl.BlockSpec((ROW_TILE, CONV_WIDTH), lambda i: (i, 0)),
            pl.BlockSpec((EDGE_ROWS, CONV_WIDTH), lambda i: (i, 0)),
            pl.BlockSpec((ROW_TILE, 3 * ATTN_WIDTH), lambda i: (i, 0)),
            pl.BlockSpec((SUBLANES, LANES), lambda i: (i, 0)),
        ],
        out_shape=[
            jax.ShapeDtypeStruct((TOKENS, CONV_WIDTH), f32),
            jax.ShapeDtypeStruct((n * EDGE_ROWS, CONV_WIDTH), f32),
            jax.ShapeDtypeStruct((TOKENS, 3 * ATTN_WIDTH), f32),
            jax.ShapeDtypeStruct((n * SUBLANES, LANES), f32),
        ],
        compiler_params=_params(("arbitrary",), 48),
        name="in_proj",
    )(x2, g_mix, w_in_bf, conv_w, *rope)


CAP_INTERIOR, CAP_FIRST, CAP_LAST, CAP_SINGLE = range(4)
MASK_OPEN = float(jnp.finfo(jnp.float32).max)
MID_DIL = DILATED_PATTERNS[1][1]
MID_LEN = SEQ // MID_DIL
assert [d for _, d in DILATED_PATTERNS] == [1, MID_DIL, MID_DIL * MID_DIL]
assert all(w // 2 // d == BAND_HALF for w, d in DILATED_PATTERNS)


SAFE_SCORE = 40.0
NORM_SLACK = 1.1


def _attn_kernel(q_ref, k_ref, v_ref, o_ref, cap_ref, q4_ref, k4_ref, v4_ref,
                 qlo_ref, qhi_ref, kpad_ref, vlo_ref, vhi_ref,
                 num_ref, m_ref, l_ref, *, exact_max):
    @pl.when(pl.program_id(1) == 0)
    def _():
        row = lax.broadcasted_iota(jnp.int32, (Q_CHUNK, K_WIN), 0)
        col = lax.broadcasted_iota(jnp.int32, (Q_CHUNK, K_WIN), 1)
        band = (col - row >= 0) & (col - row <= 2 * BAND_HALF)
        not_before = col >= BAND_HALF
        not_after = col < K_WIN - BAND_HALF
        for idx, ok in ((CAP_INTERIOR, band), (CAP_FIRST, band & not_before),
                        (CAP_LAST, band & not_after), (CAP_SINGLE, band & not_before & not_after)):
            cap_ref[idx] = jnp.where(ok, MASK_OPEN, NEG_INF)

    lane_lo = lax.broadcasted_iota(jnp.int32, (Q_CHUNK, LANES), 1) < HEAD_DIM
    zero_pad = jnp.zeros((BAND_HALF, LANES), bf16)

    def zero_rows(start):
        for ref in (kpad_ref, vlo_ref, vhi_ref):
            ref[pl.ds(start, BAND_HALF), :] = zero_pad

    def stage(sources, src, q_row, pad_row, keep=None):
        qf, kf, vf = (ref[src, :] for ref in sources)
        if keep is not None:
            for ref, val in zip((q4_ref, k4_ref, v4_ref), (qf, kf, vf)):
                ref[keep, :] = val
        q_dst, p_dst = pl.ds(q_row, Q_CHUNK), pl.ds(pad_row, Q_CHUNK)
        qlo_ref[q_dst, :] = jnp.where(lane_lo, qf, 0.0).astype(bf16)
        qhi_ref[q_dst, :] = jnp.where(lane_lo, 0.0, qf).astype(bf16)
        kpad_ref[p_dst, :] = kf.astype(bf16)
        vlo_ref[p_dst, :] = jnp.where(lane_lo, vf, 0.0).astype(bf16)
        vhi_ref[p_dst, :] = jnp.where(lane_lo, 0.0, vf).astype(bf16)

    def chunk(pat, out_rows, q_row, pad_row, cap):
        q2 = jnp.concatenate([qlo_ref[pl.ds(q_row, Q_CHUNK), :], qhi_ref[pl.ds(q_row, Q_CHUNK), :]],
                             axis=0)
        s = lax.dot_general(q2, kpad_ref[pl.ds(pad_row, K_WIN), :], (((1,), (1,)), ((), ())),
                            preferred_element_type=f32)
        caps = cap_ref[cap]
        s = jnp.minimum(s, jnp.concatenate([caps, caps], axis=0))
        if exact_max:
            m = jnp.max(s, axis=1, keepdims=True)
            p = jnp.exp(s - m)
        else:
            p = jnp.exp(s)
        l = jnp.sum(p, axis=1, keepdims=True)
        pb = p.astype(bf16)
        num = (jnp.dot(pb[:Q_CHUNK], vlo_ref[pl.ds(pad_row, K_WIN), :], preferred_element_type=f32)
               + jnp.dot(pb[Q_CHUNK:], vhi_ref[pl.ds(pad_row, K_WIN), :], preferred_element_type=f32))
        num_ref[pat, out_rows, :] = num
        if exact_max:
            m_ref[pat, out_rows, :] = jnp.where(lane_lo, m[:Q_CHUNK], m[Q_CHUNK:])
        l_ref[pat, out_rows, :] = jnp.where(lane_lo, l[:Q_CHUNK], l[Q_CHUNK:])

    def run_pattern(pat, sources, classes, length, keep_f32=False):
        n_chunks = length // Q_CHUNK
        region = length + 2 * BAND_HALF

        def rows(first, stride, i):
            if stride == 1:
                return pl.ds(first + i * Q_CHUNK, Q_CHUNK)
            return pl.ds(first + stride * Q_CHUNK * i, Q_CHUNK, stride=stride)

        for u in range(len(classes)):
            zero_rows(u * region)
            zero_rows(u * region + BAND_HALF + length)
        for u, (src0, src_stride, _, _) in enumerate(classes):
            for i in range(n_chunks):
                keep = pl.ds(u * length + i * Q_CHUNK, Q_CHUNK) if keep_f32 else None
                stage(sources, rows(src0, src_stride, i), u * length + i * Q_CHUNK,
                      u * region + BAND_HALF + i * Q_CHUNK, keep)
        for u, (_, _, out0, out_stride) in enumerate(classes):
            for j in range(n_chunks):
                if n_chunks == 1:
                    cap = CAP_SINGLE
                else:
                    cap = CAP_FIRST if j == 0 else CAP_LAST if j == n_chunks - 1 else CAP_INTERIOR
                chunk(pat, rows(out0, out_stride, j), u * length + j * Q_CHUNK,
                      u * region + j * Q_CHUNK, cap)

    inputs = (q_ref, k_ref, v_ref)
    run_pattern(0, inputs, [(0, 1, 0, 1)], SEQ)
    run_pattern(1, inputs, [(r, MID_DIL, r * MID_LEN, 1) for r in range(MID_DIL)], MID_LEN,
                keep_f32=True)
    run_pattern(2, (q4_ref, k4_ref, v4_ref),
                [(r * MID_LEN + a, MID_DIL, r * MID_LEN + a, MID_DIL)
                 for r in range(MID_DIL) for a in range(MID_DIL)], MID_LEN // MID_DIL)

    def mix(blk, c):
        mid_rows = pl.ds(pl.multiple_of(blk * Q_CHUNK, Q_CHUNK), Q_CHUNK)
        per_class = MID_LEN // Q_CHUNK
        pos_rows = pl.ds(blk // per_class + MID_DIL * Q_CHUNK * (blk % per_class), Q_CHUNK,
                         stride=MID_DIL)
        rows = (pos_rows, mid_rows, mid_rows)
        num = jnp.zeros((Q_CHUNK, LANES), f32)
        den = jnp.zeros((Q_CHUNK, LANES), f32)
        if exact_max:
            ms = [m_ref[p, rows[p], :] for p in range(N_PATTERNS)]
            m_all = functools.reduce(jnp.maximum, ms)
        for p in range(N_PATTERNS):
            a = jnp.exp(ms[p] - m_all) if exact_max else 1.0
            den = den + a * l_ref[p, rows[p], :]
            num = num + a * num_ref[p, rows[p], :]
        o_ref[pos_rows, :] = num / den
        return c
    lax.fori_loop(0, SEQ // Q_CHUNK, mix, 0)


def _attention(qkv3, exact_max):
    n_hp = N_HEADS // HEADS_PER_STEP
    blk = lambda off: pl.BlockSpec((None, SEQ, LANES), lambda b, h, off=off: (b, 0, off + h))
    pad_rows = max(SEQ + 2 * BAND_HALF * d for _, d in DILATED_PATTERNS)
    stats = pltpu.VMEM((N_PATTERNS, SEQ, LANES), f32)
    rows_f32 = pltpu.VMEM((SEQ, LANES), f32)
    return pl.pallas_call(
        functools.partial(_attn_kernel, exact_max=exact_max),
        grid=(BATCH, n_hp),
        in_specs=[blk(0), blk(n_hp), blk(2 * n_hp)],
        out_specs=pl.BlockSpec((None, SEQ, LANES), lambda b, h: (b, 0, h)),
        out_shape=jax.ShapeDtypeStruct((BATCH, SEQ, ATTN_WIDTH), f32),
        scratch_shapes=[
            pltpu.VMEM((4, Q_CHUNK, K_WIN), f32),
            rows_f32, rows_f32, rows_f32,
            pltpu.VMEM((SEQ, LANES), bf16), pltpu.VMEM((SEQ, LANES), bf16),
            pltpu.VMEM((pad_rows, LANES), bf16), pltpu.VMEM((pad_rows, LANES), bf16),
            pltpu.VMEM((pad_rows, LANES), bf16),
            stats, stats, stats,
        ],
        compiler_params=_params(("arbitrary", "arbitrary"), 48),
        name="attention_exact" if exact_max else "attention",
    )(qkv3, qkv3, qkv3)


def _store_token_major(ref, x):
    for j in range(SLAB_ROWS):
        ref[pl.ds(j, x.shape[0], stride=SLAB_ROWS), :] = x[:, j * LANES:(j + 1) * LANES]


def _dot_nt(a, b):
    return lax.dot_general(a, b, (((1,), (1,)), ((), ())), preferred_element_type=f32)


def _out_proj_kernel(yc_ref, edge_ref, edge_prev_ref, edge_next_ref, ya_ref, x_ref, w_ref,
                     gc_ref, ga_ref, gf_ref, wr_ref, x1_ref, h2_ref, lg_ref):
    tile = pl.program_id(0) % (SEQ // ROW_TILE)
    from_prev = (edge_ref[pl.ds(EDGE_B_FIRST, 1), :]
                 * edge_prev_ref[pl.ds(EDGE_U_LAST + SUBLANES - 1, 1), :])
    from_next = (edge_ref[pl.ds(EDGE_B_LAST + SUBLANES - 1, 1), :]
                 * edge_next_ref[pl.ds(EDGE_U_FIRST, 1), :])
    from_prev = jnp.where(tile == 0, 0.0, from_prev)
    from_next = jnp.where(tile == SEQ // ROW_TILE - 1, 0.0, from_next)
    row = lax.broadcasted_iota(jnp.int32, (ROW_TILE, CONV_WIDTH), 0)
    y_conv = (yc_ref[...] + jnp.where(row == 0, from_prev, 0.0)
              + jnp.where(row == ROW_TILE - 1, from_next, 0.0))
    yc = _rms(y_conv, gc_ref[...]).astype(bf16)
    ya = _rms(ya_ref[...], ga_ref[...]).astype(bf16)
    mix = (jnp.dot(yc, w_ref[pl.ds(0, CONV_WIDTH), :], preferred_element_type=f32)
           + jnp.dot(ya, w_ref[pl.ds(CONV_WIDTH, ATTN_WIDTH), :], preferred_element_type=f32))
    x1 = x_ref[...] + mix
    _store_token_major(x1_ref, x1)
    h2 = _rms(x1, gf_ref[...])
    _store_token_major(h2_ref, h2)
    wr = wr_ref[...]
    wr_hi = wr.astype(bf16)
    wr_lo = (wr - wr_hi.astype(f32)).astype(bf16)
    h2_hi = h2.astype(bf16)
    h2_lo = (h2 - h2_hi.astype(f32)).astype(bf16)
    both = _dot_nt(jnp.concatenate([wr_hi, wr_lo], axis=0), h2_hi)
    lg_ref[...] = both[:N_EXPERTS] + both[N_EXPERTS:] + _dot_nt(wr_hi, h2_lo)


def _out_proj(yc, edge, ya, x2, w_out_bf, g_conv, g_attn, g_ffn, w_router_t):
    n = TOKENS // ROW_TILE
    slab = pl.BlockSpec((ROW_TILE * SLAB_ROWS, LANES), lambda i: (i, 0))
    slab_shape = jax.ShapeDtypeStruct((TOKENS * SLAB_ROWS, LANES), f32)
    return pl.pallas_call(
        _out_proj_kernel,
        grid=(n,),
        in_specs=[
            pl.BlockSpec((ROW_TILE, CONV_WIDTH), lambda i: (i, 0)),
            pl.BlockSpec((EDGE_ROWS, CONV_WIDTH), lambda i: (i, 0)),
            pl.BlockSpec((EDGE_ROWS, CONV_WIDTH), lambda i: (jnp.maximum(i - 1, 0), 0)),
            pl.BlockSpec((EDGE_ROWS, CONV_WIDTH), lambda i: (jnp.minimum(i + 1, n - 1), 0)),
            pl.BlockSpec((ROW_TILE, ATTN_WIDTH), lambda i: (i, 0)),
            pl.BlockSpec((ROW_TILE, D_MODEL), lambda i: (i, 0)),
            pl.BlockSpec((D_MODEL, D_MODEL), lambda i: (0, 0)),
            pl.BlockSpec((1, CONV_WIDTH), lambda i: (0, 0)),
            pl.BlockSpec((1, ATTN_WIDTH), lambda i: (0, 0)),
            pl.BlockSpec((1, D_MODEL), lambda i: (0, 0)),
            pl.BlockSpec((N_EXPERTS, D_MODEL), lambda i: (0, 0)),
        ],
        out_specs=[slab, slab, pl.BlockSpec((N_EXPERTS, ROW_TILE), lambda i: (0, i))],
        out_shape=[slab_shape, slab_shape, jax.ShapeDtypeStruct((N_EXPERTS, TOKENS), f32)],
        compiler_params=_params(("arbitrary",), 48),
        name="out_proj",
    )(yc, edge, edge, edge, ya, x2, w_out_bf, g_conv, g_attn, g_ffn, w_router_t)


N_LANE_BLOCKS = SEQ // LANES


def _excl_cumsum_tokens(x, tri):
    n = x.shape[0]
    stacked = jnp.concatenate([x[:, j * LANES:(j + 1) * LANES] for j in range(N_LANE_BLOCKS)], axis=0)
    within = jnp.dot(stacked.astype(bf16), tri, preferred_element_type=f32)
    totals = jnp.sum(stacked, axis=1, keepdims=True)
    out, offset = [], jnp.zeros((n, 1), f32)
    for j in range(N_LANE_BLOCKS):
        rows = slice(j * n, (j + 1) * n)
        out.append(within[rows] + offset)
        offset = offset + totals[rows]
    return jnp.concatenate(out, axis=1)


def _select_kernel(lg_ref, rank_ref, aff_ref):
    affs = []
    for b in range(BATCH):
        x = lg_ref[:, b * SEQ:(b + 1) * SEQ]
        e = jnp.exp(x - jnp.max(x, axis=0, keepdims=True))
        affs.append(e / jnp.sum(e, axis=0, keepdims=True))
    aff = jnp.concatenate(affs, axis=0)
    aff_ref[...] = aff
    thr = jnp.zeros((BATCH * N_EXPERTS, 1), jnp.int32)
    for bit in range(30, -1, -1):
        cand = thr | (1 << bit)
        hit = aff >= pltpu.bitcast(cand, f32)
        cnt = jnp.sum(jnp.where(hit, 1.0, 0.0), axis=1, keepdims=True)
        thr = jnp.where(cnt >= float(CAPACITY), cand, thr)
    above = aff >= pltpu.bitcast(thr + 1, f32)
    tie = (aff >= pltpu.bitcast(thr, f32)) & jnp.logical_not(above)
    need = float(CAPACITY) - jnp.sum(jnp.where(above, 1.0, 0.0), axis=1, keepdims=True)
    tri = jnp.where(lax.broadcasted_iota(jnp.int32, (LANES, LANES), 0)
                    < lax.broadcasted_iota(jnp.int32, (LANES, LANES), 1), 1.0, 0.0).astype(bf16)
    tie_rank = _excl_cumsum_tokens(jnp.where(tie, 1.0, 0.0), tri)
    sel = above | (tie & (tie_rank < need))
    rank_ref[...] = jnp.where(sel, _excl_cumsum_tokens(jnp.where(sel, 1.0, 0.0), tri), -1.0)


def _select(logits_t):
    shape = jax.ShapeDtypeStruct((BATCH * N_EXPERTS, SEQ), f32)
    blk = pl.BlockSpec((BATCH * N_EXPERTS, SEQ), lambda i: (0, 0))
    return pl.pallas_call(
        _select_kernel,
        grid=(1,),
        in_specs=[pl.BlockSpec((N_EXPERTS, TOKENS), lambda i: (0, 0))],
        out_specs=[blk, blk],
        out_shape=[shape, shape],
        compiler_params=_params(("arbitrary",), 40),
        name="select",
    )(logits_t)


def _route_kernel(rank_ref, aff_ref, idx_ref, gate_ref):
    rank = rank_ref[...]
    aff = aff_ref[...]
    high = jnp.floor(rank * (1.0 / SLOT_SPLIT))
    low = rank - high * SLOT_SPLIT
    digit = lax.broadcasted_iota(jnp.int32, (SLOT_SPLIT, SEQ), 0).astype(f32)
    tok = lax.broadcasted_iota(jnp.int32, (1, SEQ), 1)
    g1 = aff.astype(bf16).astype(f32)
    g2 = (aff - g1).astype(bf16).astype(f32)
    g3 = aff - g1 - g2
    per_expert = lambda v: jnp.broadcast_to(v.astype(f32), (N_EXPERTS, SEQ))
    payloads = [per_expert(tok // LANES), per_expert(tok % LANES), g1, g2, g3]
    high_rows = jnp.concatenate(
        [jnp.where(high[e:e + 1] == digit, 1.0, 0.0) for e in range(N_EXPERTS)], axis=0).astype(bf16)
    low_hit = [low[e:e + 1] == digit for e in range(N_EXPERTS)]
    n_rows = N_EXPERTS * SLOT_SPLIT
    same_expert = (lax.broadcasted_iota(jnp.int32, (n_rows, n_rows), 0) // SLOT_SPLIT
                   == lax.broadcasted_iota(jnp.int32, (n_rows, n_rows), 1) // SLOT_SPLIT)
    fold = jnp.where(lax.broadcasted_iota(jnp.int32, (n_rows, SLOT_SPLIT), 0) % SLOT_SPLIT
                     == lax.broadcasted_iota(jnp.int32, (n_rows, SLOT_SPLIT), 1), 1.0, 0.0).astype(bf16)
    folded = []
    for val in payloads:
        rows = jnp.concatenate(
            [jnp.where(low_hit[e], val[e:e + 1], 0.0) for e in range(N_EXPERTS)], axis=0).astype(bf16)
        picked = jnp.where(same_expert, _dot_nt(high_rows, rows), 0.0).astype(bf16)
        folded.append(jnp.dot(picked, fold, preferred_element_type=f32))
    idx_ref[...] = ((folded[0] * float(LANES) + folded[1]) * float(SLAB_ROWS)).astype(jnp.int32)
    gate_ref[...] = folded[2] + folded[3] + folded[4]


def _route(rank, aff):
    n_rows = N_EXPERTS * SLOT_SPLIT
    blk = pl.BlockSpec((None, n_rows, CAPACITY // SLOT_SPLIT), lambda b: (b, 0, 0))
    shape = (BATCH, n_rows, CAPACITY // SLOT_SPLIT)
    per_seq = pl.BlockSpec((N_EXPERTS, SEQ), lambda b: (b, 0))
    return pl.pallas_call(
        _route_kernel,
        grid=(BATCH,),
        in_specs=[per_seq, per_seq],
        out_specs=[blk, blk],
        out_shape=[jax.ShapeDtypeStruct(shape, jnp.int32), jax.ShapeDtypeStruct(shape, f32)],
        compiler_params=_params(("arbitrary",), 40),
        name="route",
    )(rank, aff)


def _slot_list_spec():
    return pl.BlockSpec((None, 1, EXPERTS_PER_STEP * CAPACITY), lambda b, s: (b, 0, s),
                        memory_space=pltpu.SMEM)


def _gather_kernel(row_ref, h2_ref, xe_ref, tile_ref):
    def expert(e, c):
        base = e * CAPACITY
        for slot in range(CAPACITY):
            row = row_ref[0, base + slot]
            slab = h2_ref[pl.ds(pl.multiple_of(row, SLAB_ROWS), SLAB_ROWS), :]
            tile_ref[pl.ds(slot, SLAB_ROWS, stride=TILE_PITCH), :] = slab
        for j in range(SLAB_ROWS):
            xe_ref[e, :, j * LANES:(j + 1) * LANES] = (
                tile_ref[pl.ds(j * TILE_PITCH, CAPACITY), :].astype(bf16))
        return c
    lax.fori_loop(0, N_EXPERTS, expert, 0)


def _gather(rows, h2_slab):
    return pl.pallas_call(
        _gather_kernel,
        grid=(BATCH,),
        in_specs=[
            pl.BlockSpec((None, 1, N_EXPERTS * CAPACITY), lambda b: (b, 0, 0),
                         memory_space=pltpu.SMEM),
            pl.BlockSpec((SEQ * SLAB_ROWS, LANES), lambda b: (b, 0)),
        ],
        out_specs=pl.BlockSpec((N_EXPERTS, None, CAPACITY, D_MODEL), lambda b: (0, b, 0, 0)),
        out_shape=jax.ShapeDtypeStruct((N_EXPERTS, BATCH, CAPACITY, D_MODEL), bf16),
        scratch_shapes=[pltpu.VMEM((SLAB_ROWS * TILE_PITCH, LANES), f32)],
        compiler_params=_params(("arbitrary",), 48),
        name="gather",
    )(rows, h2_slab)


def _ffn_kernel(xe_ref, wg_ref, wu_ref, wd_ref, y_ref, acc_ref):
    f = pl.program_id(1)

    last = D_FF // FF_TILE - 1

    def partial_out():
        xe = xe_ref[...]
        a = jnp.dot(xe, wg_ref[...].astype(bf16), preferred_element_type=f32)
        u = jnp.dot(xe, wu_ref[...].astype(bf16), preferred_element_type=f32)
        hidden = (a * (1.0 / (1.0 + jnp.exp(-a))) * u).astype(bf16)
        return jnp.dot(hidden, wd_ref[...].astype(bf16), preferred_element_type=f32)

    @pl.when(f == 0)
    def _():
        acc_ref[...] = partial_out()

    @pl.when((f > 0) & (f < last))
    def _():
        acc_ref[...] += partial_out()

    @pl.when(f == last)
    def _():
        y_ref[...] = (acc_ref[...] + partial_out()).astype(bf16)


def _ffn(xe, w_gate, w_up, w_down):
    rows = BATCH * CAPACITY
    return pl.pallas_call(
        _ffn_kernel,
        grid=(N_EXPERTS, D_FF // FF_TILE),
        in_specs=[
            pl.BlockSpec((None, rows, D_MODEL), lambda e, f: (e, 0, 0)),
            pl.BlockSpec((None, D_MODEL, FF_TILE), lambda e, f: (e, 0, f)),
            pl.BlockSpec((None, D_MODEL, FF_TILE), lambda e, f: (e, 0, f)),
            pl.BlockSpec((None, FF_TILE, D_MODEL), lambda e, f: (e, f, 0)),
        ],
        out_specs=pl.BlockSpec((None, rows, D_MODEL), lambda e, f: (e, 0, 0)),
        out_shape=jax.ShapeDtypeStruct((N_EXPERTS, rows, D_MODEL), bf16),
        scratch_shapes=[pltpu.VMEM((rows, D_MODEL), f32)],
        compiler_params=_params(("arbitrary", "arbitrary"), 56),
        name="ffn",
    )(xe, w_gate, w_up, w_down)


SCATTER_UNROLL = 8
COPY_ROWS = 2048
NORM_TOKENS = 256


def _combine_kernel(row_ref, gate_ref, y_ref, x1_ref, g_ref, o_ref, acc_ref, tile_ref):
    step = pl.program_id(1)

    @pl.when(step == 0)
    def _():
        def copy(i, c):
            rows = pl.ds(pl.multiple_of(i * COPY_ROWS, COPY_ROWS), COPY_ROWS)
            acc_ref[rows, :] = x1_ref[rows, :]
            return c
        lax.fori_loop(0, SEQ * SLAB_ROWS // COPY_ROWS, copy, 0)

    for k in range(EXPERTS_PER_STEP):
        for j in range(SLAB_ROWS):
            tile_ref[k, pl.ds(j * TILE_PITCH, CAPACITY), :] = (
                y_ref[k, :, j * LANES:(j + 1) * LANES].astype(f32))
        for first in range(0, CAPACITY, SCATTER_UNROLL):
            updates = []
            for slot in range(first, first + SCATTER_UNROLL):
                row = row_ref[0, k * CAPACITY + slot]
                gate = gate_ref[0, k * CAPACITY + slot]
                rows = pl.ds(pl.multiple_of(row, SLAB_ROWS), SLAB_ROWS)
                contrib = tile_ref[k, pl.ds(slot, SLAB_ROWS, stride=TILE_PITCH), :]
                updates.append((rows, acc_ref[rows, :] + gate * contrib))
            for rows, value in updates:
                acc_ref[rows, :] = value

    @pl.when(step == N_EXPERTS // EXPERTS_PER_STEP - 1)
    def _():
        def norm(i, c):
            first = i * (NORM_TOKENS * SLAB_ROWS)
            chunks = [acc_ref[pl.ds(first + j, NORM_TOKENS, stride=SLAB_ROWS), :]
                      for j in range(SLAB_ROWS)]
            squares = functools.reduce(lambda a, b: a + b, [ch * ch for ch in chunks])
            inv = lax.rsqrt(jnp.sum(squares, axis=1, keepdims=True) * (1.0 / D_MODEL) + RMS_EPS)
            rows = pl.ds(pl.multiple_of(i * NORM_TOKENS, NORM_TOKENS), NORM_TOKENS)
            for j in range(SLAB_ROWS):
                cols = slice(j * LANES, (j + 1) * LANES)
                o_ref[rows, cols] = chunks[j] * inv * g_ref[:, cols]
            return c
        lax.fori_loop(0, SEQ // NORM_TOKENS, norm, 0)


def _combine(rows, gate, y, x1_slab, g_final):
    return pl.pallas_call(
        _combine_kernel,
        grid=(BATCH, N_EXPERTS // EXPERTS_PER_STEP),
        in_specs=[
            _slot_list_spec(), _slot_list_spec(),
            pl.BlockSpec((EXPERTS_PER_STEP, None, CAPACITY, D_MODEL), lambda b, s: (s, b, 0, 0)),
            pl.BlockSpec((SEQ * SLAB_ROWS, LANES), lambda b, s: (b, 0)),
            pl.BlockSpec((1, D_MODEL), lambda b, s: (0, 0)),
        ],
        out_specs=pl.BlockSpec((SEQ, D_MODEL), lambda b, s: (b, 0)),
        out_shape=jax.ShapeDtypeStruct((TOKENS, D_MODEL), f32),
        scratch_shapes=[pltpu.VMEM((SEQ * SLAB_ROWS, LANES), f32),
                        pltpu.VMEM((EXPERTS_PER_STEP, SLAB_ROWS * TILE_PITCH, LANES), f32)],
        compiler_params=_params(("arbitrary", "arbitrary"), 56),
        name="combine",
    )(rows, gate, y, x1_slab, g_final)


def _rope_freqs():
    inv_freq = ROPE_THETA ** (-jnp.arange(ROT_HALF, dtype=f32) / ROT_HALF)
    return inv_freq.reshape(ROT_HALF, 1)


def kernel(x, positions, g_mix, w_in, conv_w, g_conv_out, g_attn_out, w_out, g_ffn, w_router,
           w_gate, w_up, w_down, g_final):
    x2 = x.reshape(TOKENS, D_MODEL)
    rope = _rope_tables(positions.reshape(BATCH, 1, SEQ), _rope_freqs())
    yc, edge, qkv, norms = _in_proj(x2, g_mix[0].reshape(1, D_MODEL), w_in[0].astype(bf16),
                                    conv_w[0], rope)
    norms = norms.reshape(BATCH, SEQ // ROW_TILE, 2, ATTN_WIDTH // LANES, LANES)
    norms = norms[..., :HEADS_PER_STEP].max(axis=1)
    scores_bounded = jnp.all(norms[:, 0] * norms[:, 1] * NORM_SLACK <= SAFE_SCORE ** 2)
    qkv3 = qkv.reshape(BATCH, SEQ, 3 * ATTN_WIDTH)
    ya = lax.cond(scores_bounded, functools.partial(_attention, exact_max=False),
                  functools.partial(_attention, exact_max=True), qkv3)
    x1, h2, logits_t = _out_proj(yc, edge, ya.reshape(TOKENS, ATTN_WIDTH), x2, w_out[0].astype(bf16),
                                 g_conv_out[0].reshape(1, CONV_WIDTH),
                                 g_attn_out[0].reshape(1, ATTN_WIDTH), g_ffn[0].reshape(1, D_MODEL),
                                 w_router[0].T)
    idx, gate = _route(*_select(logits_t))
    idx = idx.reshape(BATCH, 1, N_EXPERTS * CAPACITY)
    gate = gate.reshape(BATCH, 1, N_EXPERTS * CAPACITY)
    xe = _gather(idx, h2)
    y = _ffn(xe.reshape(N_EXPERTS, BATCH * CAPACITY, D_MODEL), w_gate[0], w_up[0], w_down[0])
    out = _combine(idx, gate, y.reshape(N_EXPERTS, BATCH, CAPACITY, D_MODEL), x1,
                   g_final.reshape(1, D_MODEL))
    return out.reshape(BATCH, SEQ, D_MODEL)
```

```python
import functools

import jax
import jax.numpy as jnp
from jax import lax
from jax.experimental import pallas as pl
from jax.experimental.pallas import tpu as pltpu

D_MODEL = 1024
BATCH = 8
SEQ = 2048
TOKENS = BATCH * SEQ
CONV_WIDTH = 512
ATTN_WIDTH = 512
HEAD_DIM = 64
N_HEADS = 8
DILATED_PATTERNS = ((128, 1), (512, 4), (2048, 16))
N_PATTERNS = len(DILATED_PATTERNS)
ROPE_THETA = 500000.0
ROT_DIM = 16
ROT_HALF = ROT_DIM // 2
N_EXPERTS = 16
CAPACITY = 2 * SEQ // N_EXPERTS
D_FF = 2 * D_MODEL
PROJ_WIDTH = 3 * CONV_WIDTH + 3 * ATTN_WIDTH
RMS_EPS = 1e-6
NEG_INF = -1e30

LANES = 128
ROW_TILE = 1024
FF_TILE = 512
BAND_HALF = 64
Q_CHUNK = 2 * BAND_HALF
K_WIN = 4 * BAND_HALF
HEADS_PER_STEP = LANES // HEAD_DIM
SUBLANES = 8
SLAB_ROWS = D_MODEL // LANES
TILE_PITCH = CAPACITY + SUBLANES
SLOT_SPLIT = 16
EXPERTS_PER_STEP = 4
MIB = 1024 * 1024

f32 = jnp.float32
bf16 = jnp.bfloat16


def _params(semantics, vmem_mib):
    return pltpu.CompilerParams(dimension_semantics=semantics, vmem_limit_bytes=vmem_mib * MIB)


def _rms(x, g):
    return x * lax.rsqrt(jnp.mean(x * x, axis=-1, keepdims=True) + RMS_EPS) * g


def _rope_table_kernel(pos_ref, invf_ref, cos_ref, sin_ref):
    ang = invf_ref[...] * pos_ref[...].astype(f32)
    trig = jnp.concatenate([jnp.cos(ang), jnp.sin(ang)], axis=0)
    lane = lax.broadcasted_iota(jnp.int32, (ROT_DIM, 2 * LANES), 1)
    row = lax.broadcasted_iota(jnp.int32, (ROT_DIM, 2 * LANES), 0)
    freq = row % ROT_HALF
    is_sin = row >= ROT_HALF
    at_lo = lane % HEAD_DIM == freq
    at_hi = lane % HEAD_DIM == freq + ROT_HALF
    table = lane // LANES
    place = (jnp.where((table == 0) & jnp.logical_not(is_sin) & (at_lo | at_hi), 1.0, 0.0)
             + jnp.where((table == 1) & is_sin & at_hi, 1.0, 0.0)
             - jnp.where((table == 1) & is_sin & at_lo, 1.0, 0.0)).astype(bf16)
    spread = jnp.zeros((SEQ, 2 * LANES), f32)
    rest = trig
    for _ in range(3):
        piece = rest.astype(bf16)
        rest = rest - piece.astype(f32)
        spread = spread + lax.dot_general(piece, place, (((0,), (0,)), ((), ())),
                                          preferred_element_type=f32)
    unrotated = jnp.where(
        lax.broadcasted_iota(jnp.int32, (1, LANES), 1) % HEAD_DIM >= ROT_DIM, 1.0, 0.0)
    cos_ref[...] = spread[:, :LANES] + unrotated
    sin_ref[...] = spread[:, LANES:]


def _rope_tables(pos_row, invf_col):
    blk = pl.BlockSpec((SEQ, LANES), lambda b: (b, 0))
    return pl.pallas_call(
        _rope_table_kernel,
        grid=(BATCH,),
        in_specs=[
            pl.BlockSpec((None, 1, SEQ), lambda b: (b, 0, 0)),
            pl.BlockSpec((ROT_HALF, 1), lambda b: (0, 0)),
        ],
        out_specs=[blk, blk],
        out_shape=[jax.ShapeDtypeStruct((TOKENS, LANES), f32)] * 2,
        compiler_params=_params(("arbitrary",), 32),
        name="rope_tables",
    )(pos_row, invf_col)


EDGE_U_FIRST, EDGE_U_LAST, EDGE_B_FIRST, EDGE_B_LAST = (k * SUBLANES for k in range(4))
EDGE_ROWS = 4 * SUBLANES


def _in_proj_kernel(x_ref, g_ref, w_ref, cw_ref, cos_ref, sin_ref,
                    yc_ref, edge_ref, qkv_ref, norm_ref):
    h = _rms(x_ref[...], g_ref[...]).astype(bf16)
    p = jnp.dot(h, w_ref[...], preferred_element_type=f32)
    gate_b = p[:, :CONV_WIDTH]
    u = p[:, CONV_WIDTH:2 * CONV_WIDTH] * p[:, 2 * CONV_WIDTH:3 * CONV_WIDTH]
    cw = cw_ref[...]
    row = lax.broadcasted_iota(jnp.int32, (ROW_TILE, CONV_WIDTH), 0)
    before = jnp.where(row == 0, 0.0, pltpu.roll(u, 1, 0))
    after = jnp.where(row == ROW_TILE - 1, 0.0, pltpu.roll(u, ROW_TILE - 1, 0))
    yc_ref[...] = gate_b * (cw[0:1] * before + cw[1:2] * u + cw[2:3] * after)
    last = ROW_TILE - SUBLANES
    edge_ref[pl.ds(EDGE_U_FIRST, SUBLANES), :] = u[:SUBLANES]
    edge_ref[pl.ds(EDGE_U_LAST, SUBLANES), :] = u[last:]
    edge_ref[pl.ds(EDGE_B_FIRST, SUBLANES), :] = gate_b[:SUBLANES] * cw[0:1]
    edge_ref[pl.ds(EDGE_B_LAST, SUBLANES), :] = gate_b[last:] * cw[2:3]
    cs, sn = cos_ref[...], sin_ref[...]
    upper = lax.broadcasted_iota(jnp.int32, (ROW_TILE, LANES), 1) % HEAD_DIM >= ROT_HALF
    s1 = jnp.where(upper, sn, 0.0)
    s2 = jnp.where(upper, 0.0, sn)
    squares = []
    for blk in range(2 * ATTN_WIDTH // LANES):
        x = p[:, 3 * CONV_WIDTH + blk * LANES:3 * CONV_WIDTH + (blk + 1) * LANES]
        y = x * cs + pltpu.roll(x, ROT_HALF, 1) * s1 + pltpu.roll(x, LANES - ROT_HALF, 1) * s2
        if blk < ATTN_WIDTH // LANES:
            y = y * HEAD_DIM ** -0.5
        qkv_ref[:, blk * LANES:(blk + 1) * LANES] = y
        squares.append((y * y).astype(bf16))
    qkv_ref[:, 2 * ATTN_WIDTH:] = p[:, 3 * CONV_WIDTH + 2 * ATTN_WIDTH:]
    head_of = jnp.where(
        lax.broadcasted_iota(jnp.int32, (2 * ATTN_WIDTH, LANES), 0) // HEAD_DIM
        == lax.broadcasted_iota(jnp.int32, (2 * ATTN_WIDTH, LANES), 1), 1.0, 0.0).astype(bf16)
    row_norms = jnp.dot(jnp.concatenate(squares, axis=1), head_of, preferred_element_type=f32)
    norm_ref[...] = jnp.broadcast_to(jnp.max(row_norms, axis=0, keepdims=True), (SUBLANES, LANES))


def _in_proj(x2, g_mix, w_in_bf, conv_w, rope):
    n = TOKENS // ROW_TILE
    table = pl.BlockSpec((ROW_TILE, LANES), lambda i: (i, 0))
    return pl.pallas_call(
        _in_proj_kernel,
        grid=(n,),
        in_specs=[
            pl.BlockSpec((ROW_TILE, D_MODEL), lambda i: (i, 0)),
            pl.BlockSpec((1, D_MODEL), lambda i: (0, 0)),
            pl.BlockSpec((D_MODEL, PROJ_WIDTH), lambda i: (0, 0)),
            pl.BlockSpec((3, CONV_WIDTH), lambda i: (0, 0)),
            table, table,
        ],
        out_specs=[
            pl.BlockSpec((ROW_TILE, CONV_WIDTH), lambda i: (i, 0)),
            pl.BlockSpec((EDGE_ROWS, CONV_WIDTH), lambda i: (i, 0)),
            pl.BlockSpec((ROW_TILE, 3 * ATTN_WIDTH), lambda i: (i, 0)),
            pl.BlockSpec((SUBLANES, LANES), lambda i: (i, 0)),
        ],
        out_shape=[
            jax.ShapeDtypeStruct((TOKENS, CONV_WIDTH), f32),
            jax.ShapeDtypeStruct((n * EDGE_ROWS, CONV_WIDTH), f32),
            jax.ShapeDtypeStruct((TOKENS, 3 * ATTN_WIDTH), f32),
            jax.ShapeDtypeStruct((n * SUBLANES, LANES), f32),
        ],
        compiler_params=_params(("arbitrary",), 48),
        name="in_proj",
    )(x2, g_mix, w_in_bf, conv_w, *rope)


CAP_INTERIOR, CAP_FIRST, CAP_LAST, CAP_SINGLE = range(4)
MASK_OPEN = float(jnp.finfo(jnp.float32).max)
MID_DIL = DILATED_PATTERNS[1][1]
MID_LEN = SEQ // MID_DIL
assert [d for _, d in DILATED_PATTERNS] == [1, MID_DIL, MID_DIL * MID_DIL]
assert all(w // 2 // d == BAND_HALF for w, d in DILATED_PATTERNS)


SAFE_SCORE = 40.0
NORM_SLACK = 1.1


def _attn_kernel(q_ref, k_ref, v_ref, o_ref, cap_ref, q4_ref, k4_ref, v4_ref,
                 qlo_ref, qhi_ref, kpad_ref, vlo_ref, vhi_ref,
                 num_ref, m_ref, l_ref, *, exact_max):
    @pl.when(pl.program_id(1) == 0)
    def _():
        row = lax.broadcasted_iota(jnp.int32, (Q_CHUNK, K_WIN), 0)
        col = lax.broadcasted_iota(jnp.int32, (Q_CHUNK, K_WIN), 1)
        band = (col - row >= 0) & (col - row <= 2 * BAND_HALF)
        not_before = col >= BAND_HALF
        not_after = col < K_WIN - BAND_HALF
        for idx, ok in ((CAP_INTERIOR, band), (CAP_FIRST, band & not_before),
                        (CAP_LAST, band & not_after), (CAP_SINGLE, band & not_before & not_after)):
            cap_ref[idx] = jnp.where(ok, MASK_OPEN, NEG_INF)

    lane_lo = lax.broadcasted_iota(jnp.int32, (Q_CHUNK, LANES), 1) < HEAD_DIM
    zero_pad = jnp.zeros((BAND_HALF, LANES), bf16)

    def zero_rows(start):
        for ref in (kpad_ref, vlo_ref, vhi_ref):
            ref[pl.ds(start, BAND_HALF), :] = zero_pad

    def stage(sources, src, q_row, pad_row, keep=None):
        qf, kf, vf = (ref[src, :] for ref in sources)
        if keep is not None:
            for ref, val in zip((q4_ref, k4_ref, v4_ref), (qf, kf, vf)):
                ref[keep, :] = val
        q_dst, p_dst = pl.ds(q_row, Q_CHUNK), pl.ds(pad_row, Q_CHUNK)
        qlo_ref[q_dst, :] = jnp.where(lane_lo, qf, 0.0).astype(bf16)
        qhi_ref[q_dst, :] = jnp.where(lane_lo, 0.0, qf).astype(bf16)
        kpad_ref[p_dst, :] = kf.astype(bf16)
        vlo_ref[p_dst, :] = jnp.where(lane_lo, vf, 0.0).astype(bf16)
        vhi_ref[p_dst, :] = jnp.where(lane_lo, 0.0, vf).astype(bf16)

    def chunk(pat, out_rows, q_row, pad_row, cap):
        q2 = jnp.concatenate([qlo_ref[pl.ds(q_row, Q_CHUNK), :], qhi_ref[pl.ds(q_row, Q_CHUNK), :]],
                             axis=0)
        s = lax.dot_general(q2, kpad_ref[pl.ds(pad_row, K_WIN), :], (((1,), (1,)), ((), ())),
                            preferred_element_type=f32)
        caps = cap_ref[cap]
        s = jnp.minimum(s, jnp.concatenate([caps, caps], axis=0))
        if exact_max:
            m = jnp.max(s, axis=1, keepdims=True)
            p = jnp.exp(s - m)
        else:
            p = jnp.exp(s)
        l = jnp.sum(p, axis=1, keepdims=True)
        pb = p.astype(bf16)
        num = (jnp.dot(pb[:Q_CHUNK], vlo_ref[pl.ds(pad_row, K_WIN), :], preferred_element_type=f32)
               + jnp.dot(pb[Q_CHUNK:], vhi_ref[pl.ds(pad_row, K_WIN), :], preferred_element_type=f32))
        num_ref[pat, out_rows, :] = num
        if exact_max:
            m_ref[pat, out_rows, :] = jnp.where(lane_lo, m[:Q_CHUNK], m[Q_CHUNK:])
        l_ref[pat, out_rows, :] = jnp.where(lane_lo, l[:Q_CHUNK], l[Q_CHUNK:])

    def run_pattern(pat, sources, classes, length, keep_f32=False):
        n_chunks = length // Q_CHUNK
        region = length + 2 * BAND_HALF

        def rows(first, stride, i):
            if stride == 1:
                return pl.ds(first + i * Q_CHUNK, Q_CHUNK)
            return pl.ds(first + stride * Q_CHUNK * i, Q_CHUNK, stride=stride)

        for u in range(len(classes)):
            zero_rows(u * region)
            zero_rows(u * region + BAND_HALF + length)
        for u, (src0, src_stride, _, _) in enumerate(classes):
            for i in range(n_chunks):
                keep = pl.ds(u * length + i * Q_CHUNK, Q_CHUNK) if keep_f32 else None
                stage(sources, rows(src0, src_stride, i), u * length + i * Q_CHUNK,
                      u * region + BAND_HALF + i * Q_CHUNK, keep)
        for u, (_, _, out0, out_stride) in enumerate(classes):
            for j in range(n_chunks):
                if n_chunks == 1:
                    cap = CAP_SINGLE
                else:
                    cap = CAP_FIRST if j == 0 else CAP_LAST if j == n_chunks - 1 else CAP_INTERIOR
                chunk(pat, rows(out0, out_stride, j), u * length + j * Q_CHUNK,
                      u * region + j * Q_CHUNK, cap)

    inputs = (q_ref, k_ref, v_ref)
    run_pattern(0, inputs, [(0, 1, 0, 1)], SEQ)
    run_pattern(1, inputs, [(r, MID_DIL, r * MID_LEN, 1) for r in range(MID_DIL)], MID_LEN,
                keep_f32=True)
    run_pattern(2, (q4_ref, k4_ref, v4_ref),
                [(r * MID_LEN + a, MID_DIL, r * MID_LEN + a, MID_DIL)
                 for r in range(MID_DIL) for a in range(MID_DIL)], MID_LEN // MID_DIL)

    def mix(blk, c):
        mid_rows = pl.ds(pl.multiple_of(blk * Q_CHUNK, Q_CHUNK), Q_CHUNK)
        per_class = MID_LEN // Q_CHUNK
        pos_rows = pl.ds(blk // per_class + MID_DIL * Q_CHUNK * (blk % per_class), Q_CHUNK,
                         stride=MID_DIL)
        rows = (pos_rows, mid_rows, mid_rows)
        num = jnp.zeros((Q_CHUNK, LANES), f32)
        den = jnp.zeros((Q_CHUNK, LANES), f32)
        if exact_max:
            ms = [m_ref[p, rows[p], :] for p in range(N_PATTERNS)]
            m_all = functools.reduce(jnp.maximum, ms)
        for p in range(N_PATTERNS):
            a = jnp.exp(ms[p] - m_all) if exact_max else 1.0
            den = den + a * l_ref[p, rows[p], :]
            num = num + a * num_ref[p, rows[p], :]
        o_ref[pos_rows, :] = num / den
        return c
    lax.fori_loop(0, SEQ // Q_CHUNK, mix, 0)


def _attention(qkv3, exact_max):
    n_hp = N_HEADS // HEADS_PER_STEP
    blk = lambda off: pl.BlockSpec((None, SEQ, LANES), lambda b, h, off=off: (b, 0, off + h))
    pad_rows = max(SEQ + 2 * BAND_HALF * d for _, d in DILATED_PATTERNS)
    stats = pltpu.VMEM((N_PATTERNS, SEQ, LANES), f32)
    rows_f32 = pltpu.VMEM((SEQ, LANES), f32)
    return pl.pallas_call(
        functools.partial(_attn_kernel, exact_max=exact_max),
        grid=(BATCH, n_hp),
        in_specs=[blk(0), blk(n_hp), blk(2 * n_hp)],
        out_specs=pl.BlockSpec((None, SEQ, LANES), lambda b, h: (b, 0, h)),
        out_shape=jax.ShapeDtypeStruct((BATCH, SEQ, ATTN_WIDTH), f32),
        scratch_shapes=[
            pltpu.VMEM((4, Q_CHUNK, K_WIN), f32),
            rows_f32, rows_f32, rows_f32,
            pltpu.VMEM((SEQ, LANES), bf16), pltpu.VMEM((SEQ, LANES), bf16),
            pltpu.VMEM((pad_rows, LANES), bf16), pltpu.VMEM((pad_rows, LANES), bf16),
            pltpu.VMEM((pad_rows, LANES), bf16),
            stats, stats, stats,
        ],
        compiler_params=_params(("arbitrary", "arbitrary"), 48),
        name="attention_exact" if exact_max else "attention",
    )(qkv3, qkv3, qkv3)


def _store_token_major(ref, x):
    for j in range(SLAB_ROWS):
        ref[pl.ds(j, x.shape[0], stride=SLAB_ROWS), :] = x[:, j * LANES:(j + 1) * LANES]


def _dot_nt(a, b):
    return lax.dot_general(a, b, (((1,), (1,)), ((), ())), preferred_element_type=f32)


def _out_proj_kernel(yc_ref, edge_ref, edge_prev_ref, edge_next_ref, ya_ref, x_ref, w_ref,
                     gc_ref, ga_ref, gf_ref, wr_ref, x1_ref, h2_ref, lg_ref):
    tile = pl.program_id(0) % (SEQ // ROW_TILE)
    from_prev = (edge_ref[pl.ds(EDGE_B_FIRST, 1), :]
                 * edge_prev_ref[pl.ds(EDGE_U_LAST + SUBLANES - 1, 1), :])
    from_next = (edge_ref[pl.ds(EDGE_B_LAST + SUBLANES - 1, 1), :]
                 * edge_next_ref[pl.ds(EDGE_U_FIRST, 1), :])
    from_prev = jnp.where(tile == 0, 0.0, from_prev)
    from_next = jnp.where(tile == SEQ // ROW_TILE - 1, 0.0, from_next)
    row = lax.broadcasted_iota(jnp.int32, (ROW_TILE, CONV_WIDTH), 0)
    y_conv = (yc_ref[...] + jnp.where(row == 0, from_prev, 0.0)
              + jnp.where(row == ROW_TILE - 1, from_next, 0.0))
    yc = _rms(y_conv, gc_ref[...]).astype(bf16)
    ya = _rms(ya_ref[...], ga_ref[...]).astype(bf16)
    mix = (jnp.dot(yc, w_ref[pl.ds(0, CONV_WIDTH), :], preferred_element_type=f32)
           + jnp.dot(ya, w_ref[pl.ds(CONV_WIDTH, ATTN_WIDTH), :], preferred_element_type=f32))
    x1 = x_ref[...] + mix
    _store_token_major(x1_ref, x1)
    h2 = _rms(x1, gf_ref[...])
    _store_token_major(h2_ref, h2)
    wr = wr_ref[...]
    wr_hi = wr.astype(bf16)
    wr_lo = (wr - wr_hi.astype(f32)).astype(bf16)
    h2_hi = h2.astype(bf16)
    h2_lo = (h2 - h2_hi.astype(f32)).astype(bf16)
    both = _dot_nt(jnp.concatenate([wr_hi, wr_lo], axis=0), h2_hi)
    lg_ref[...] = both[:N_EXPERTS] + both[N_EXPERTS:] + _dot_nt(wr_hi, h2_lo)


def _out_proj(yc, edge, ya, x2, w_out_bf, g_conv, g_attn, g_ffn, w_router_t):
    n = TOKENS // ROW_TILE
    slab = pl.BlockSpec((ROW_TILE * SLAB_ROWS, LANES), lambda i: (i, 0))
    slab_shape = jax.ShapeDtypeStruct((TOKENS * SLAB_ROWS, LANES), f32)
    return pl.pallas_call(
        _out_proj_kernel,
        grid=(n,),
        in_specs=[
            pl.BlockSpec((ROW_TILE, CONV_WIDTH), lambda i: (i, 0)),
            pl.BlockSpec((EDGE_ROWS, CONV_WIDTH), lambda i: (i, 0)),
            pl.BlockSpec((EDGE_ROWS, CONV_WIDTH), lambda i: (jnp.maximum(i - 1, 0), 0)),
            pl.BlockSpec((EDGE_ROWS, CONV_WIDTH), lambda i: (jnp.minimum(i + 1, n - 1), 0)),
            pl.BlockSpec((ROW_TILE, ATTN_WIDTH), lambda i: (i, 0)),
            pl.BlockSpec((ROW_TILE, D_MODEL), lambda i: (i, 0)),
            pl.BlockSpec((D_MODEL, D_MODEL), lambda i: (0, 0)),
            pl.BlockSpec((1, CONV_WIDTH), lambda i: (0, 0)),
            pl.BlockSpec((1, ATTN_WIDTH), lambda i: (0, 0)),
            pl.BlockSpec((1, D_MODEL), lambda i: (0, 0)),
            pl.BlockSpec((N_EXPERTS, D_MODEL), lambda i: (0, 0)),
        ],
        out_specs=[slab, slab, pl.BlockSpec((N_EXPERTS, ROW_TILE), lambda i: (0, i))],
        out_shape=[slab_shape, slab_shape, jax.ShapeDtypeStruct((N_EXPERTS, TOKENS), f32)],
        compiler_params=_params(("arbitrary",), 48),
        name="out_proj",
    )(yc, edge, edge, edge, ya, x2, w_out_bf, g_conv, g_attn, g_ffn, w_router_t)


N_LANE_BLOCKS = SEQ // LANES


def _excl_cumsum_tokens(x, tri):
    n = x.shape[0]
    stacked = jnp.concatenate([x[:, j * LANES:(j + 1) * LANES] for j in range(N_LANE_BLOCKS)], axis=0)
    within = jnp.dot(stacked.astype(bf16), tri, preferred_element_type=f32)
    totals = jnp.sum(stacked, axis=1, keepdims=True)
    out, offset = [], jnp.zeros((n, 1), f32)
    for j in range(N_LANE_BLOCKS):
        rows = slice(j * n, (j + 1) * n)
        out.append(within[rows] + offset)
        offset = offset + totals[rows]
    return jnp.concatenate(out, axis=1)


def _select_kernel(lg_ref, rank_ref, aff_ref):
    affs = []
    for b in range(BATCH):
        x = lg_ref[:, b * SEQ:(b + 1) * SEQ]
        e = jnp.exp(x - jnp.max(x, axis=0, keepdims=True))
        affs.append(e / jnp.sum(e, axis=0, keepdims=True))
    aff = jnp.concatenate(affs, axis=0)
    aff_ref[...] = aff
    thr = jnp.zeros((BATCH * N_EXPERTS, 1), jnp.int32)
    for bit in range(30, -1, -1):
        cand = thr | (1 << bit)
        hit = aff >= pltpu.bitcast(cand, f32)
        cnt = jnp.sum(jnp.where(hit, 1.0, 0.0), axis=1, keepdims=True)
        thr = jnp.where(cnt >= float(CAPACITY), cand, thr)
    above = aff >= pltpu.bitcast(thr + 1, f32)
    tie = (aff >= pltpu.bitcast(thr, f32)) & jnp.logical_not(above)
    need = float(CAPACITY) - jnp.sum(jnp.where(above, 1.0, 0.0), axis=1, keepdims=True)
    tri = jnp.where(lax.broadcasted_iota(jnp.int32, (LANES, LANES), 0)
                    < lax.broadcasted_iota(jnp.int32, (LANES, LANES), 1), 1.0, 0.0).astype(bf16)
    tie_rank = _excl_cumsum_tokens(jnp.where(tie, 1.0, 0.0), tri)
    sel = above | (tie & (tie_rank < need))
    rank_ref[...] = jnp.where(sel, _excl_cumsum_tokens(jnp.where(sel, 1.0, 0.0), tri), -1.0)


def _select(logits_t):
    shape = jax.ShapeDtypeStruct((BATCH * N_EXPERTS, SEQ), f32)
    blk = pl.BlockSpec((BATCH * N_EXPERTS, SEQ), lambda i: (0, 0))
    return pl.pallas_call(
        _select_kernel,
        grid=(1,),
        in_specs=[pl.BlockSpec((N_EXPERTS, TOKENS), lambda i: (0, 0))],
        out_specs=[blk, blk],
        out_shape=[shape, shape],
        compiler_params=_params(("arbitrary",), 40),
        name="select",
    )(logits_t)


def _route_kernel(rank_ref, aff_ref, idx_ref, gate_ref):
    rank = rank_ref[...]
    aff = aff_ref[...]
    high = jnp.floor(rank * (1.0 / SLOT_SPLIT))
    low = rank - high * SLOT_SPLIT
    digit = lax.broadcasted_iota(jnp.int32, (SLOT_SPLIT, SEQ), 0).astype(f32)
    tok = lax.broadcasted_iota(jnp.int32, (1, SEQ), 1)
    g1 = aff.astype(bf16).astype(f32)
    g2 = (aff - g1).astype(bf16).astype(f32)
    g3 = aff - g1 - g2
    per_expert = lambda v: jnp.broadcast_to(v.astype(f32), (N_EXPERTS, SEQ))
    payloads = [per_expert(tok // LANES), per_expert(tok % LANES), g1, g2, g3]
    high_rows = jnp.concatenate(
        [jnp.where(high[e:e + 1] == digit, 1.0, 0.0) for e in range(N_EXPERTS)], axis=0).astype(bf16)
    low_hit = [low[e:e + 1] == digit for e in range(N_EXPERTS)]
    n_rows = N_EXPERTS * SLOT_SPLIT
    same_expert = (lax.broadcasted_iota(jnp.int32, (n_rows, n_rows), 0) // SLOT_SPLIT
                   == lax.broadcasted_iota(jnp.int32, (n_rows, n_rows), 1) // SLOT_SPLIT)
    fold = jnp.where(lax.broadcasted_iota(jnp.int32, (n_rows, SLOT_SPLIT), 0) % SLOT_SPLIT
                     == lax.broadcasted_iota(jnp.int32, (n_rows, SLOT_SPLIT), 1), 1.0, 0.0).astype(bf16)
    folded = []
    for val in payloads:
        rows = jnp.concatenate(
            [jnp.where(low_hit[e], val[e:e + 1], 0.0) for e in range(N_EXPERTS)], axis=0).astype(bf16)
        picked = jnp.where(same_expert, _dot_nt(high_rows, rows), 0.0).astype(bf16)
        folded.append(jnp.dot(picked, fold, preferred_element_type=f32))
    idx_ref[...] = ((folded[0] * float(LANES) + folded[1]) * float(SLAB_ROWS)).astype(jnp.int32)
    gate_ref[...] = folded[2] + folded[3] + folded[4]


def _route(rank, aff):
    n_rows = N_EXPERTS * SLOT_SPLIT
    blk = pl.BlockSpec((None, n_rows, CAPACITY // SLOT_SPLIT), lambda b: (b, 0, 0))
    shape = (BATCH, n_rows, CAPACITY // SLOT_SPLIT)
    per_seq = pl.BlockSpec((N_EXPERTS, SEQ), lambda b: (b, 0))
    return pl.pallas_call(
        _route_kernel,
        grid=(BATCH,),
        in_specs=[per_seq, per_seq],
        out_specs=[blk, blk],
        out_shape=[jax.ShapeDtypeStruct(shape, jnp.int32), jax.ShapeDtypeStruct(shape, f32)],
        compiler_params=_params(("arbitrary",), 40),
        name="route",
    )(rank, aff)


def _slot_list_spec():
    return pl.BlockSpec((None, 1, EXPERTS_PER_STEP * CAPACITY), lambda b, s: (b, 0, s),
                        memory_space=pltpu.SMEM)


def _gather_kernel(row_ref, h2_ref, xe_ref, tile_ref):
    def expert(e, c):
        base = e * CAPACITY
        for slot in range(CAPACITY):
            row = row_ref[0, base + slot]
            slab = h2_ref[pl.ds(pl.multiple_of(row, SLAB_ROWS), SLAB_ROWS), :]
            tile_ref[pl.ds(slot, SLAB_ROWS, stride=TILE_PITCH), :] = slab
        for j in range(SLAB_ROWS):
            xe_ref[e, :, j * LANES:(j + 1) * LANES] = (
                tile_ref[pl.ds(j * TILE_PITCH, CAPACITY), :].astype(bf16))
        return c
    lax.fori_loop(0, N_EXPERTS, expert, 0)


def _gather(rows, h2_slab):
    return pl.pallas_call(
        _gather_kernel,
        grid=(BATCH,),
        in_specs=[
            pl.BlockSpec((None, 1, N_EXPERTS * CAPACITY), lambda b: (b, 0, 0),
                         memory_space=pltpu.SMEM),
            pl.BlockSpec((SEQ * SLAB_ROWS, LANES), lambda b: (b, 0)),
        ],
        out_specs=pl.BlockSpec((N_EXPERTS, None, CAPACITY, D_MODEL), lambda b: (0, b, 0, 0)),
        out_shape=jax.ShapeDtypeStruct((N_EXPERTS, BATCH, CAPACITY, D_MODEL), bf16),
        scratch_shapes=[pltpu.VMEM((SLAB_ROWS * TILE_PITCH, LANES), f32)],
        compiler_params=_params(("arbitrary",), 48),
        name="gather",
    )(rows, h2_slab)


def _ffn_kernel(xe_ref, wg_ref, wu_ref, wd_ref, y_ref, acc_ref):
    f = pl.program_id(1)

    last = D_FF // FF_TILE - 1

    def partial_out():
        xe = xe_ref[...]
        a = jnp.dot(xe, wg_ref[...].astype(bf16), preferred_element_type=f32)
        u = jnp.dot(xe, wu_ref[...].astype(bf16), preferred_element_type=f32)
        hidden = (a * (1.0 / (1.0 + jnp.exp(-a))) * u).astype(bf16)
        return jnp.dot(hidden, wd_ref[...].astype(bf16), preferred_element_type=f32)

    @pl.when(f == 0)
    def _():
        acc_ref[...] = partial_out()

    @pl.when((f > 0) & (f < last))
    def _():
        acc_ref[...] += partial_out()

    @pl.when(f == last)
    def _():
        y_ref[...] = (acc_ref[...] + partial_out()).astype(bf16)


def _ffn(xe, w_gate, w_up, w_down):
    rows = BATCH * CAPACITY
    return pl.pallas_call(
        _ffn_kernel,
        grid=(N_EXPERTS, D_FF // FF_TILE),
        in_specs=[
            pl.BlockSpec((None, rows, D_MODEL), lambda e, f: (e, 0, 0)),
            pl.BlockSpec((None, D_MODEL, FF_TILE), lambda e, f: (e, 0, f)),
            pl.BlockSpec((None, D_MODEL, FF_TILE), lambda e, f: (e, 0, f)),
            pl.BlockSpec((None, FF_TILE, D_MODEL), lambda e, f: (e, f, 0)),
        ],
        out_specs=pl.BlockSpec((None, rows, D_MODEL), lambda e, f: (e, 0, 0)),
        out_shape=jax.ShapeDtypeStruct((N_EXPERTS, rows, D_MODEL), bf16),
        scratch_shapes=[pltpu.VMEM((rows, D_MODEL), f32)],
        compiler_params=_params(("arbitrary", "arbitrary"), 56),
        name="ffn",
    )(xe, w_gate, w_up, w_down)


SCATTER_UNROLL = 8
COPY_ROWS = 2048
NORM_TOKENS = 256


def _combine_kernel(row_ref, gate_ref, y_ref, x1_ref, g_ref, o_ref, acc_ref, tile_ref):
    step = pl.program_id(1)

    @pl.when(step == 0)
    def _():
        def copy(i, c):
            rows = pl.ds(pl.multiple_of(i * COPY_ROWS, COPY_ROWS), COPY_ROWS)
            acc_ref[rows, :] = x1_ref[rows, :]
            return c
        lax.fori_loop(0, SEQ * SLAB_ROWS // COPY_ROWS, copy, 0)

    for k in range(EXPERTS_PER_STEP):
        for j in range(SLAB_ROWS):
            tile_ref[k, pl.ds(j * TILE_PITCH, CAPACITY), :] = (
                y_ref[k, :, j * LANES:(j + 1) * LANES].astype(f32))
        for first in range(0, CAPACITY, SCATTER_UNROLL):
            updates = []
            for slot in range(first, first + SCATTER_UNROLL):
                row = row_ref[0, k * CAPACITY + slot]
                gate = gate_ref[0, k * CAPACITY + slot]
                rows = pl.ds(pl.multiple_of(row, SLAB_ROWS), SLAB_ROWS)
                contrib = tile_ref[k, pl.ds(slot, SLAB_ROWS, stride=TILE_PITCH), :]
                updates.append((rows, acc_ref[rows, :] + gate * contrib))
            for rows, value in updates:
                acc_ref[rows, :] = value

    @pl.when(step == N_EXPERTS // EXPERTS_PER_STEP - 1)
    def _():
        def norm(i, c):
            first = i * (NORM_TOKENS * SLAB_ROWS)
            chunks = [acc_ref[pl.ds(first + j, NORM_TOKENS, stride=SLAB_ROWS), :]
                      for j in range(SLAB_ROWS)]
            squares = functools.reduce(lambda a, b: a + b, [ch * ch for ch in chunks])
            inv = lax.rsqrt(jnp.sum(squares, axis=1, keepdims=True) * (1.0 / D_MODEL) + RMS_EPS)
            rows = pl.ds(pl.multiple_of(i * NORM_TOKENS, NORM_TOKENS), NORM_TOKENS)
            for j in range(SLAB_ROWS):
                cols = slice(j * LANES, (j + 1) * LANES)
                o_ref[rows, cols] = chunks[j] * inv * g_ref[:, cols]
            return c
        lax.fori_loop(0, SEQ // NORM_TOKENS, norm, 0)


def _combine(rows, gate, y, x1_slab, g_final):
    return pl.pallas_call(
        _combine_kernel,
        grid=(BATCH, N_EXPERTS // EXPERTS_PER_STEP),
        in_specs=[
            _slot_list_spec(), _slot_list_spec(),
            pl.BlockSpec((EXPERTS_PER_STEP, None, CAPACITY, D_MODEL), lambda b, s: (s, b, 0, 0)),
            pl.BlockSpec((SEQ * SLAB_ROWS, LANES), lambda b, s: (b, 0)),
            pl.BlockSpec((1, D_MODEL), lambda b, s: (0, 0)),
        ],
        out_specs=pl.BlockSpec((SEQ, D_MODEL), lambda b, s: (b, 0)),
        out_shape=jax.ShapeDtypeStruct((TOKENS, D_MODEL), f32),
        scratch_shapes=[pltpu.VMEM((SEQ * SLAB_ROWS, LANES), f32),
                        pltpu.VMEM((EXPERTS_PER_STEP, SLAB_ROWS * TILE_PITCH, LANES), f32)],
        compiler_params=_params(("arbitrary", "arbitrary"), 56),
        name="combine",
    )(rows, gate, y, x1_slab, g_final)


def _rope_freqs():
    inv_freq = ROPE_THETA ** (-jnp.arange(ROT_HALF, dtype=f32) / ROT_HALF)
    return inv_freq.reshape(ROT_HALF, 1)


def kernel(x, positions, g_mix, w_in, conv_w, g_conv_out, g_attn_out, w_out, g_ffn, w_router,
           w_gate, w_up, w_down, g_final):
    x2 = x.reshape(TOKENS, D_MODEL)
    rope = _rope_tables(positions.reshape(BATCH, 1, SEQ), _rope_freqs())
    yc, edge, qkv, norms = _in_proj(x2, g_mix[0].reshape(1, D_MODEL), w_in[0].astype(bf16),
                                    conv_w[0], rope)
    norms = norms.reshape(BATCH, SEQ // ROW_TILE, SUBLANES, LANES)[:, :, 0, :2 * N_HEADS]
    norms = norms.max(axis=1).reshape(BATCH, 2, N_HEADS)
    scores_bounded = jnp.all(norms[:, 0] * norms[:, 1] * NORM_SLACK <= SAFE_SCORE ** 2)
    qkv3 = qkv.reshape(BATCH, SEQ, 3 * ATTN_WIDTH)
    ya = lax.cond(scores_bounded, functools.partial(_attention, exact_max=False),
                  functools.partial(_attention, exact_max=True), qkv3)
    x1, h2, logits_t = _out_proj(yc, edge, ya.reshape(TOKENS, ATTN_WIDTH), x2, w_out[0].astype(bf16),
                                 g_conv_out[0].reshape(1, CONV_WIDTH),
                                 g_attn_out[0].reshape(1, ATTN_WIDTH), g_ffn[0].reshape(1, D_MODEL),
                                 w_router[0].T)
    idx, gate = _route(*_select(logits_t))
    idx = idx.reshape(BATCH, 1, N_EXPERTS * CAPACITY)
    gate = gate.reshape(BATCH, 1, N_EXPERTS * CAPACITY)
    xe = _gather(idx, h2)
    y = _ffn(xe.reshape(N_EXPERTS, BATCH * CAPACITY, D_MODEL), w_gate[0], w_up[0], w_down[0])
    out = _combine(idx, gate, y.reshape(N_EXPERTS, BATCH, CAPACITY, D_MODEL), x1,
                   g_final.reshape(1, D_MODEL))
    return out.reshape(BATCH, SEQ, D_MODEL)
```

```python
import functools

import jax
import jax.numpy as jnp
from jax import lax
from jax.experimental import pallas as pl
from jax.experimental.pallas import tpu as pltpu

D_MODEL = 1024
BATCH = 8
SEQ = 2048
TOKENS = BATCH * SEQ
CONV_WIDTH = 512
ATTN_WIDTH = 512
HEAD_DIM = 64
N_HEADS = 8
DILATED_PATTERNS = ((128, 1), (512, 4), (2048, 16))
N_PATTERNS = len(DILATED_PATTERNS)
ROPE_THETA = 500000.0
ROT_DIM = 16
ROT_HALF = ROT_DIM // 2
N_EXPERTS = 16
CAPACITY = 2 * SEQ // N_EXPERTS
D_FF = 2 * D_MODEL
PROJ_WIDTH = 3 * CONV_WIDTH + 3 * ATTN_WIDTH
RMS_EPS = 1e-6
NEG_INF = -1e30

LANES = 128
ROW_TILE = 1024
FF_TILE = 512
FF_GROUP = 256
BAND_HALF = 64
Q_CHUNK = 2 * BAND_HALF
K_WIN = 4 * BAND_HALF
HEADS_PER_STEP = LANES // HEAD_DIM
SUBLANES = 8
SLAB_ROWS = D_MODEL // LANES
TILE_PITCH = CAPACITY + SUBLANES
SLOT_SPLIT = 16
EXPERTS_PER_STEP = 4
MIB = 1024 * 1024

f32 = jnp.float32
bf16 = jnp.bfloat16


def _params(semantics, vmem_mib):
    return pltpu.CompilerParams(dimension_semantics=semantics, vmem_limit_bytes=vmem_mib * MIB)


def _rms(x, g):
    return x * lax.rsqrt(jnp.mean(x * x, axis=-1, keepdims=True) + RMS_EPS) * g


def _rope_table_kernel(pos_ref, invf_ref, cos_ref, sin_ref):
    ang = invf_ref[...] * pos_ref[...].astype(f32)
    trig = jnp.concatenate([jnp.cos(ang), jnp.sin(ang)], axis=0)
    lane = lax.broadcasted_iota(jnp.int32, (ROT_DIM, 2 * LANES), 1)
    row = lax.broadcasted_iota(jnp.int32, (ROT_DIM, 2 * LANES), 0)
    freq = row % ROT_HALF
    is_sin = row >= ROT_HALF
    at_lo = lane % HEAD_DIM == freq
    at_hi = lane % HEAD_DIM == freq + ROT_HALF
    table = lane // LANES
    place = (jnp.where((table == 0) & jnp.logical_not(is_sin) & (at_lo | at_hi), 1.0, 0.0)
             + jnp.where((table == 1) & is_sin & at_hi, 1.0, 0.0)
             - jnp.where((table == 1) & is_sin & at_lo, 1.0, 0.0)).astype(bf16)
    spread = jnp.zeros((SEQ, 2 * LANES), f32)
    rest = trig
    for _ in range(3):
        piece = rest.astype(bf16)
        rest = rest - piece.astype(f32)
        spread = spread + lax.dot_general(piece, place, (((0,), (0,)), ((), ())),
                                          preferred_element_type=f32)
    unrotated = jnp.where(
        lax.broadcasted_iota(jnp.int32, (1, LANES), 1) % HEAD_DIM >= ROT_DIM, 1.0, 0.0)
    cos_ref[...] = spread[:, :LANES] + unrotated
    sin_ref[...] = spread[:, LANES:]


def _rope_tables(pos_row, invf_col):
    blk = pl.BlockSpec((SEQ, LANES), lambda b: (b, 0))
    return pl.pallas_call(
        _rope_table_kernel,
        grid=(BATCH,),
        in_specs=[
            pl.BlockSpec((None, 1, SEQ), lambda b: (b, 0, 0)),
            pl.BlockSpec((ROT_HALF, 1), lambda b: (0, 0)),
        ],
        out_specs=[blk, blk],
        out_shape=[jax.ShapeDtypeStruct((TOKENS, LANES), f32)] * 2,
        compiler_params=_params(("arbitrary",), 32),
        name="rope_tables",
    )(pos_row, invf_col)


EDGE_U_FIRST, EDGE_U_LAST, EDGE_B_FIRST, EDGE_B_LAST = (k * SUBLANES for k in range(4))
EDGE_ROWS = 4 * SUBLANES


def _in_proj_kernel(x_ref, g_ref, w_ref, cw_ref, cos_ref, sin_ref,
                    yc_ref, edge_ref, qkv_ref, norm_ref):
    h = _rms(x_ref[...], g_ref[...]).astype(bf16)
    p = jnp.dot(h, w_ref[...], preferred_element_type=f32)
    gate_b = p[:, :CONV_WIDTH]
    u = p[:, CONV_WIDTH:2 * CONV_WIDTH] * p[:, 2 * CONV_WIDTH:3 * CONV_WIDTH]
    cw = cw_ref[...]
    row = lax.broadcasted_iota(jnp.int32, (ROW_TILE, CONV_WIDTH), 0)
    before = jnp.where(row == 0, 0.0, pltpu.roll(u, 1, 0))
    after = jnp.where(row == ROW_TILE - 1, 0.0, pltpu.roll(u, ROW_TILE - 1, 0))
    yc_ref[...] = gate_b * (cw[0:1] * before + cw[1:2] * u + cw[2:3] * after)
    last = ROW_TILE - SUBLANES
    edge_ref[pl.ds(EDGE_U_FIRST, SUBLANES), :] = u[:SUBLANES]
    edge_ref[pl.ds(EDGE_U_LAST, SUBLANES), :] = u[last:]
    edge_ref[pl.ds(EDGE_B_FIRST, SUBLANES), :] = gate_b[:SUBLANES] * cw[0:1]
    edge_ref[pl.ds(EDGE_B_LAST, SUBLANES), :] = gate_b[last:] * cw[2:3]
    cs, sn = cos_ref[...], sin_ref[...]
    upper = lax.broadcasted_iota(jnp.int32, (ROW_TILE, LANES), 1) % HEAD_DIM >= ROT_HALF
    s1 = jnp.where(upper, sn, 0.0)
    s2 = jnp.where(upper, 0.0, sn)
    squares = []
    for blk in range(2 * ATTN_WIDTH // LANES):
        x = p[:, 3 * CONV_WIDTH + blk * LANES:3 * CONV_WIDTH + (blk + 1) * LANES]
        y = x * cs + pltpu.roll(x, ROT_HALF, 1) * s1 + pltpu.roll(x, LANES - ROT_HALF, 1) * s2
        if blk < ATTN_WIDTH // LANES:
            y = y * HEAD_DIM ** -0.5
        qkv_ref[:, blk * LANES:(blk + 1) * LANES] = y
        squares.append((y * y).astype(bf16))
    qkv_ref[:, 2 * ATTN_WIDTH:] = p[:, 3 * CONV_WIDTH + 2 * ATTN_WIDTH:]
    head_of = jnp.where(
        lax.broadcasted_iota(jnp.int32, (2 * ATTN_WIDTH, LANES), 0) // HEAD_DIM
        == lax.broadcasted_iota(jnp.int32, (2 * ATTN_WIDTH, LANES), 1), 1.0, 0.0).astype(bf16)
    row_norms = jnp.dot(jnp.concatenate(squares, axis=1), head_of, preferred_element_type=f32)
    norm_ref[...] = jnp.broadcast_to(jnp.max(row_norms, axis=0, keepdims=True), (SUBLANES, LANES))


def _in_proj(x2, g_mix, w_in_bf, conv_w, rope):
    n = TOKENS // ROW_TILE
    table = pl.BlockSpec((ROW_TILE, LANES), lambda i: (i, 0))
    return pl.pallas_call(
        _in_proj_kernel,
        grid=(n,),
        in_specs=[
            pl.BlockSpec((ROW_TILE, D_MODEL), lambda i: (i, 0)),
            pl.BlockSpec((1, D_MODEL), lambda i: (0, 0)),
            pl.BlockSpec((D_MODEL, PROJ_WIDTH), lambda i: (0, 0)),
            pl.BlockSpec((3, CONV_WIDTH), lambda i: (0, 0)),
            table, table,
        ],
        out_specs=[
            pl.BlockSpec((ROW_TILE, CONV_WIDTH), lambda i: (i, 0)),
            pl.BlockSpec((EDGE_ROWS, CONV_WIDTH), lambda i: (i, 0)),
            pl.BlockSpec((ROW_TILE, 3 * ATTN_WIDTH), lambda i: (i, 0)),
            pl.BlockSpec((SUBLANES, LANES), lambda i: (i, 0)),
        ],
        out_shape=[
            jax.ShapeDtypeStruct((TOKENS, CONV_WIDTH), f32),
            jax.ShapeDtypeStruct((n * EDGE_ROWS, CONV_WIDTH), f32),
            jax.ShapeDtypeStruct((TOKENS, 3 * ATTN_WIDTH), f32),
            jax.ShapeDtypeStruct((n * SUBLANES, LANES), f32),
        ],
        compiler_params=_params(("arbitrary",), 48),
        name="in_proj",
    )(x2, g_mix, w_in_bf, conv_w, *rope)


CAP_INTERIOR, CAP_FIRST, CAP_LAST, CAP_SINGLE = range(4)
MASK_OPEN = float(jnp.finfo(jnp.float32).max)
MID_DIL = DILATED_PATTERNS[1][1]
MID_LEN = SEQ // MID_DIL
assert [d for _, d in DILATED_PATTERNS] == [1, MID_DIL, MID_DIL * MID_DIL]
assert all(w // 2 // d == BAND_HALF for w, d in DILATED_PATTERNS)


SAFE_SCORE = 40.0
NORM_SLACK = 1.1


def _attn_kernel(q_ref, k_ref, v_ref, o_ref, cap_ref, q4_ref, k4_ref, v4_ref,
                 qlo_ref, qhi_ref, kpad_ref, vlo_ref, vhi_ref,
                 num_ref, m_ref, l_ref, *, exact_max):
    @pl.when(pl.program_id(1) == 0)
    def _():
        row = lax.broadcasted_iota(jnp.int32, (Q_CHUNK, K_WIN), 0)
        col = lax.broadcasted_iota(jnp.int32, (Q_CHUNK, K_WIN), 1)
        band = (col - row >= 0) & (col - row <= 2 * BAND_HALF)
        not_before = col >= BAND_HALF
        not_after = col < K_WIN - BAND_HALF
        for idx, ok in ((CAP_INTERIOR, band), (CAP_FIRST, band & not_before),
                        (CAP_LAST, band & not_after), (CAP_SINGLE, band & not_before & not_after)):
            cap_ref[idx] = jnp.where(ok, MASK_OPEN, NEG_INF)

    lane_lo = lax.broadcasted_iota(jnp.int32, (Q_CHUNK, LANES), 1) < HEAD_DIM
    zero_pad = jnp.zeros((BAND_HALF, LANES), bf16)

    def zero_rows(start):
        for ref in (kpad_ref, vlo_ref, vhi_ref):
            ref[pl.ds(start, BAND_HALF), :] = zero_pad

    def stage(sources, src, q_row, pad_row, keep=None):
        qf, kf, vf = (ref[src, :] for ref in sources)
        if keep is not None:
            for ref, val in zip((q4_ref, k4_ref, v4_ref), (qf, kf, vf)):
                ref[keep, :] = val
        q_dst, p_dst = pl.ds(q_row, Q_CHUNK), pl.ds(pad_row, Q_CHUNK)
        qlo_ref[q_dst, :] = jnp.where(lane_lo, qf, 0.0).astype(bf16)
        qhi_ref[q_dst, :] = jnp.where(lane_lo, 0.0, qf).astype(bf16)
        kpad_ref[p_dst, :] = kf.astype(bf16)
        vlo_ref[p_dst, :] = jnp.where(lane_lo, vf, 0.0).astype(bf16)
        vhi_ref[p_dst, :] = jnp.where(lane_lo, 0.0, vf).astype(bf16)

    def chunk(pat, out_rows, q_row, pad_row, cap):
        q2 = jnp.concatenate([qlo_ref[pl.ds(q_row, Q_CHUNK), :], qhi_ref[pl.ds(q_row, Q_CHUNK), :]],
                             axis=0)
        s = lax.dot_general(q2, kpad_ref[pl.ds(pad_row, K_WIN), :], (((1,), (1,)), ((), ())),
                            preferred_element_type=f32)
        caps = cap_ref[cap]
        s = jnp.minimum(s, jnp.concatenate([caps, caps], axis=0))
        if exact_max:
            m = jnp.max(s, axis=1, keepdims=True)
            p = jnp.exp(s - m)
        else:
            p = jnp.exp(s)
        l = jnp.sum(p, axis=1, keepdims=True)
        pb = p.astype(bf16)
        num = (jnp.dot(pb[:Q_CHUNK], vlo_ref[pl.ds(pad_row, K_WIN), :], preferred_element_type=f32)
               + jnp.dot(pb[Q_CHUNK:], vhi_ref[pl.ds(pad_row, K_WIN), :], preferred_element_type=f32))
        num_ref[pat, out_rows, :] = num
        if exact_max:
            m_ref[pat, out_rows, :] = jnp.where(lane_lo, m[:Q_CHUNK], m[Q_CHUNK:])
        l_ref[pat, out_rows, :] = jnp.where(lane_lo, l[:Q_CHUNK], l[Q_CHUNK:])

    def run_pattern(pat, sources, classes, length, keep_f32=False):
        n_chunks = length // Q_CHUNK
        region = length + 2 * BAND_HALF

        def rows(first, stride, i):
            if stride == 1:
                return pl.ds(first + i * Q_CHUNK, Q_CHUNK)
            return pl.ds(first + stride * Q_CHUNK * i, Q_CHUNK, stride=stride)

        for u in range(len(classes)):
            zero_rows(u * region)
            zero_rows(u * region + BAND_HALF + length)
        for u, (src0, src_stride, _, _) in enumerate(classes):
            for i in range(n_chunks):
                keep = pl.ds(u * length + i * Q_CHUNK, Q_CHUNK) if keep_f32 else None
                stage(sources, rows(src0, src_stride, i), u * length + i * Q_CHUNK,
                      u * region + BAND_HALF + i * Q_CHUNK, keep)
        for u, (_, _, out0, out_stride) in enumerate(classes):
            for j in range(n_chunks):
                if n_chunks == 1:
                    cap = CAP_SINGLE
                else:
                    cap = CAP_FIRST if j == 0 else CAP_LAST if j == n_chunks - 1 else CAP_INTERIOR
                chunk(pat, rows(out0, out_stride, j), u * length + j * Q_CHUNK,
                      u * region + j * Q_CHUNK, cap)

    inputs = (q_ref, k_ref, v_ref)
    run_pattern(0, inputs, [(0, 1, 0, 1)], SEQ)
    run_pattern(1, inputs, [(r, MID_DIL, r * MID_LEN, 1) for r in range(MID_DIL)], MID_LEN,
                keep_f32=True)
    run_pattern(2, (q4_ref, k4_ref, v4_ref),
                [(r * MID_LEN + a, MID_DIL, r * MID_LEN + a, MID_DIL)
                 for r in range(MID_DIL) for a in range(MID_DIL)], MID_LEN // MID_DIL)

    def mix(blk, c):
        mid_rows = pl.ds(pl.multiple_of(blk * Q_CHUNK, Q_CHUNK), Q_CHUNK)
        per_class = MID_LEN // Q_CHUNK
        pos_rows = pl.ds(blk // per_class + MID_DIL * Q_CHUNK * (blk % per_class), Q_CHUNK,
                         stride=MID_DIL)
        rows = (pos_rows, mid_rows, mid_rows)
        num = jnp.zeros((Q_CHUNK, LANES), f32)
        den = jnp.zeros((Q_CHUNK, LANES), f32)
        if exact_max:
            ms = [m_ref[p, rows[p], :] for p in range(N_PATTERNS)]
            m_all = functools.reduce(jnp.maximum, ms)
        for p in range(N_PATTERNS):
            a = jnp.exp(ms[p] - m_all) if exact_max else 1.0
            den = den + a * l_ref[p, rows[p], :]
            num = num + a * num_ref[p, rows[p], :]
        o_ref[pos_rows, :] = num / den
        return c
    lax.fori_loop(0, SEQ // Q_CHUNK, mix, 0)


def _attention(qkv3, exact_max):
    n_hp = N_HEADS // HEADS_PER_STEP
    blk = lambda off: pl.BlockSpec((None, SEQ, LANES), lambda b, h, off=off: (b, 0, off + h))
    pad_rows = max(SEQ + 2 * BAND_HALF * d for _, d in DILATED_PATTERNS)
    stats = pltpu.VMEM((N_PATTERNS, SEQ, LANES), f32)
    rows_f32 = pltpu.VMEM((SEQ, LANES), f32)
    return pl.pallas_call(
        functools.partial(_attn_kernel, exact_max=exact_max),
        grid=(BATCH, n_hp),
        in_specs=[blk(0), blk(n_hp), blk(2 * n_hp)],
        out_specs=pl.BlockSpec((None, SEQ, LANES), lambda b, h: (b, 0, h)),
        out_shape=jax.ShapeDtypeStruct((BATCH, SEQ, ATTN_WIDTH), f32),
        scratch_shapes=[
            pltpu.VMEM((4, Q_CHUNK, K_WIN), f32),
            rows_f32, rows_f32, rows_f32,
            pltpu.VMEM((SEQ, LANES), bf16), pltpu.VMEM((SEQ, LANES), bf16),
            pltpu.VMEM((pad_rows, LANES), bf16), pltpu.VMEM((pad_rows, LANES), bf16),
            pltpu.VMEM((pad_rows, LANES), bf16),
            stats, stats, stats,
        ],
        compiler_params=_params(("arbitrary", "arbitrary"), 48),
        name="attention_exact" if exact_max else "attention",
    )(qkv3, qkv3, qkv3)


def _store_token_major(ref, x):
    for j in range(SLAB_ROWS):
        ref[pl.ds(j, x.shape[0], stride=SLAB_ROWS), :] = x[:, j * LANES:(j + 1) * LANES]


def _dot_nt(a, b):
    return lax.dot_general(a, b, (((1,), (1,)), ((), ())), preferred_element_type=f32)


def _out_proj_kernel(yc_ref, edge_ref, edge_prev_ref, edge_next_ref, ya_ref, x_ref, w_ref,
                     gc_ref, ga_ref, gf_ref, wr_ref, x1_ref, h2_ref, lg_ref):
    tile = pl.program_id(0) % (SEQ // ROW_TILE)
    from_prev = (edge_ref[pl.ds(EDGE_B_FIRST, 1), :]
                 * edge_prev_ref[pl.ds(EDGE_U_LAST + SUBLANES - 1, 1), :])
    from_next = (edge_ref[pl.ds(EDGE_B_LAST + SUBLANES - 1, 1), :]
                 * edge_next_ref[pl.ds(EDGE_U_FIRST, 1), :])
    from_prev = jnp.where(tile == 0, 0.0, from_prev)
    from_next = jnp.where(tile == SEQ // ROW_TILE - 1, 0.0, from_next)
    row = lax.broadcasted_iota(jnp.int32, (ROW_TILE, CONV_WIDTH), 0)
    y_conv = (yc_ref[...] + jnp.where(row == 0, from_prev, 0.0)
              + jnp.where(row == ROW_TILE - 1, from_next, 0.0))
    yc = _rms(y_conv, gc_ref[...]).astype(bf16)
    ya = _rms(ya_ref[...], ga_ref[...]).astype(bf16)
    mix = (jnp.dot(yc, w_ref[pl.ds(0, CONV_WIDTH), :], preferred_element_type=f32)
           + jnp.dot(ya, w_ref[pl.ds(CONV_WIDTH, ATTN_WIDTH), :], preferred_element_type=f32))
    x1 = x_ref[...] + mix
    x1_ref[...] = x1
    h2 = _rms(x1, gf_ref[...])
    _store_token_major(h2_ref, h2)
    wr = wr_ref[...]
    wr_hi = wr.astype(bf16)
    wr_lo = (wr - wr_hi.astype(f32)).astype(bf16)
    h2_hi = h2.astype(bf16)
    h2_lo = (h2 - h2_hi.astype(f32)).astype(bf16)
    both = _dot_nt(jnp.concatenate([wr_hi, wr_lo], axis=0), h2_hi)
    lg_ref[...] = both[:N_EXPERTS] + both[N_EXPERTS:] + _dot_nt(wr_hi, h2_lo)


def _out_proj(yc, edge, ya, x2, w_out_bf, g_conv, g_attn, g_ffn, w_router_t):
    n = TOKENS // ROW_TILE
    slab = pl.BlockSpec((ROW_TILE * SLAB_ROWS, LANES), lambda i: (i, 0))
    slab_shape = jax.ShapeDtypeStruct((TOKENS * SLAB_ROWS, LANES), f32)
    return pl.pallas_call(
        _out_proj_kernel,
        grid=(n,),
        in_specs=[
            pl.BlockSpec((ROW_TILE, CONV_WIDTH), lambda i: (i, 0)),
            pl.BlockSpec((EDGE_ROWS, CONV_WIDTH), lambda i: (i, 0)),
            pl.BlockSpec((EDGE_ROWS, CONV_WIDTH), lambda i: (jnp.maximum(i - 1, 0), 0)),
            pl.BlockSpec((EDGE_ROWS, CONV_WIDTH), lambda i: (jnp.minimum(i + 1, n - 1), 0)),
            pl.BlockSpec((ROW_TILE, ATTN_WIDTH), lambda i: (i, 0)),
            pl.BlockSpec((ROW_TILE, D_MODEL), lambda i: (i, 0)),
            pl.BlockSpec((D_MODEL, D_MODEL), lambda i: (0, 0)),
            pl.BlockSpec((1, CONV_WIDTH), lambda i: (0, 0)),
            pl.BlockSpec((1, ATTN_WIDTH), lambda i: (0, 0)),
            pl.BlockSpec((1, D_MODEL), lambda i: (0, 0)),
            pl.BlockSpec((N_EXPERTS, D_MODEL), lambda i: (0, 0)),
        ],
        out_specs=[pl.BlockSpec((ROW_TILE, D_MODEL), lambda i: (i, 0)), slab,
                   pl.BlockSpec((N_EXPERTS, ROW_TILE), lambda i: (0, i))],
        out_shape=[jax.ShapeDtypeStruct((TOKENS, D_MODEL), f32), slab_shape,
                   jax.ShapeDtypeStruct((N_EXPERTS, TOKENS), f32)],
        compiler_params=_params(("arbitrary",), 48),
        name="out_proj",
    )(yc, edge, edge, edge, ya, x2, w_out_bf, g_conv, g_attn, g_ffn, w_router_t)


N_LANE_BLOCKS = SEQ // LANES


def _excl_cumsum_tokens(x, tri):
    n = x.shape[0]
    stacked = jnp.concatenate([x[:, j * LANES:(j + 1) * LANES] for j in range(N_LANE_BLOCKS)], axis=0)
    within = jnp.dot(stacked.astype(bf16), tri, preferred_element_type=f32)
    totals = jnp.sum(stacked, axis=1, keepdims=True)
    out, offset = [], jnp.zeros((n, 1), f32)
    for j in range(N_LANE_BLOCKS):
        rows = slice(j * n, (j + 1) * n)
        out.append(within[rows] + offset)
        offset = offset + totals[rows]
    return jnp.concatenate(out, axis=1)


def _select_kernel(lg_ref, rank_ref, aff_ref):
    affs = []
    for b in range(BATCH):
        x = lg_ref[:, b * SEQ:(b + 1) * SEQ]
        e = jnp.exp(x - jnp.max(x, axis=0, keepdims=True))
        affs.append(e / jnp.sum(e, axis=0, keepdims=True))
    aff = jnp.concatenate(affs, axis=0)
    aff_ref[...] = aff
    thr = jnp.zeros((BATCH * N_EXPERTS, 1), jnp.int32)
    for bit in range(30, -1, -1):
        cand = thr | (1 << bit)
        hit = aff >= pltpu.bitcast(cand, f32)
        cnt = jnp.sum(jnp.where(hit, 1.0, 0.0), axis=1, keepdims=True)
        thr = jnp.where(cnt >= float(CAPACITY), cand, thr)
    above = aff >= pltpu.bitcast(thr + 1, f32)
    tie = (aff >= pltpu.bitcast(thr, f32)) & jnp.logical_not(above)
    need = float(CAPACITY) - jnp.sum(jnp.where(above, 1.0, 0.0), axis=1, keepdims=True)
    tri = jnp.where(lax.broadcasted_iota(jnp.int32, (LANES, LANES), 0)
                    < lax.broadcasted_iota(jnp.int32, (LANES, LANES), 1), 1.0, 0.0).astype(bf16)
    tie_rank = _excl_cumsum_tokens(jnp.where(tie, 1.0, 0.0), tri)
    sel = above | (tie & (tie_rank < need))
    rank_ref[...] = jnp.where(sel, _excl_cumsum_tokens(jnp.where(sel, 1.0, 0.0), tri), -1.0)


def _select(logits_t):
    shape = jax.ShapeDtypeStruct((BATCH * N_EXPERTS, SEQ), f32)
    blk = pl.BlockSpec((BATCH * N_EXPERTS, SEQ), lambda i: (0, 0))
    return pl.pallas_call(
        _select_kernel,
        grid=(1,),
        in_specs=[pl.BlockSpec((N_EXPERTS, TOKENS), lambda i: (0, 0))],
        out_specs=[blk, blk],
        out_shape=[shape, shape],
        compiler_params=_params(("arbitrary",), 40),
        name="select",
    )(logits_t)


def _route_kernel(rank_ref, aff_ref, idx_ref, gate_ref):
    rank = rank_ref[...]
    aff = aff_ref[...]
    high = jnp.floor(rank * (1.0 / SLOT_SPLIT))
    low = rank - high * SLOT_SPLIT
    digit = lax.broadcasted_iota(jnp.int32, (SLOT_SPLIT, SEQ), 0).astype(f32)
    tok = lax.broadcasted_iota(jnp.int32, (1, SEQ), 1)
    g1 = aff.astype(bf16).astype(f32)
    g2 = (aff - g1).astype(bf16).astype(f32)
    g3 = aff - g1 - g2
    per_expert = lambda v: jnp.broadcast_to(v.astype(f32), (N_EXPERTS, SEQ))
    payloads = [per_expert(tok // LANES), per_expert(tok % LANES), g1, g2, g3]
    high_rows = jnp.concatenate(
        [jnp.where(high[e:e + 1] == digit, 1.0, 0.0) for e in range(N_EXPERTS)], axis=0).astype(bf16)
    low_hit = [low[e:e + 1] == digit for e in range(N_EXPERTS)]
    n_rows = N_EXPERTS * SLOT_SPLIT
    same_expert = (lax.broadcasted_iota(jnp.int32, (n_rows, n_rows), 0) // SLOT_SPLIT
                   == lax.broadcasted_iota(jnp.int32, (n_rows, n_rows), 1) // SLOT_SPLIT)
    fold = jnp.where(lax.broadcasted_iota(jnp.int32, (n_rows, SLOT_SPLIT), 0) % SLOT_SPLIT
                     == lax.broadcasted_iota(jnp.int32, (n_rows, SLOT_SPLIT), 1), 1.0, 0.0).astype(bf16)
    folded = []
    for val in payloads:
        rows = jnp.concatenate(
            [jnp.where(low_hit[e], val[e:e + 1], 0.0) for e in range(N_EXPERTS)], axis=0).astype(bf16)
        picked = jnp.where(same_expert, _dot_nt(high_rows, rows), 0.0).astype(bf16)
        folded.append(jnp.dot(picked, fold, preferred_element_type=f32))
    idx_ref[...] = ((folded[0] * float(LANES) + folded[1]) * float(SLAB_ROWS)).astype(jnp.int32)
    gate_ref[...] = folded[2] + folded[3] + folded[4]


def _route(rank, aff):
    n_rows = N_EXPERTS * SLOT_SPLIT
    blk = pl.BlockSpec((None, n_rows, CAPACITY // SLOT_SPLIT), lambda b: (b, 0, 0))
    shape = (BATCH, n_rows, CAPACITY // SLOT_SPLIT)
    per_seq = pl.BlockSpec((N_EXPERTS, SEQ), lambda b: (b, 0))
    return pl.pallas_call(
        _route_kernel,
        grid=(BATCH,),
        in_specs=[per_seq, per_seq],
        out_specs=[blk, blk],
        out_shape=[jax.ShapeDtypeStruct(shape, jnp.int32), jax.ShapeDtypeStruct(shape, f32)],
        compiler_params=_params(("arbitrary",), 40),
        name="route",
    )(rank, aff)


def _slot_list_spec():
    return pl.BlockSpec((None, 1, EXPERTS_PER_STEP * CAPACITY), lambda b, s: (b, 0, s),
                        memory_space=pltpu.SMEM)


def _gather_kernel(row_ref, h2_ref, xe_ref, tile_ref):
    def expert(e, c):
        base = e * CAPACITY
        for slot in range(CAPACITY):
            row = row_ref[0, base + slot]
            slab = h2_ref[pl.ds(pl.multiple_of(row, SLAB_ROWS), SLAB_ROWS), :]
            tile_ref[pl.ds(slot, SLAB_ROWS, stride=TILE_PITCH), :] = slab
        for j in range(SLAB_ROWS):
            xe_ref[e, :, j * LANES:(j + 1) * LANES] = (
                tile_ref[pl.ds(j * TILE_PITCH, CAPACITY), :].astype(bf16))
        return c
    lax.fori_loop(0, N_EXPERTS, expert, 0)


def _gather(rows, h2_slab):
    return pl.pallas_call(
        _gather_kernel,
        grid=(BATCH,),
        in_specs=[
            pl.BlockSpec((None, 1, N_EXPERTS * CAPACITY), lambda b: (b, 0, 0),
                         memory_space=pltpu.SMEM),
            pl.BlockSpec((SEQ * SLAB_ROWS, LANES), lambda b: (b, 0)),
        ],
        out_specs=pl.BlockSpec((N_EXPERTS, None, CAPACITY, D_MODEL), lambda b: (0, b, 0, 0)),
        out_shape=jax.ShapeDtypeStruct((N_EXPERTS, BATCH, CAPACITY, D_MODEL), bf16),
        scratch_shapes=[pltpu.VMEM((SLAB_ROWS * TILE_PITCH, LANES), f32)],
        compiler_params=_params(("arbitrary",), 48),
        name="gather",
    )(rows, h2_slab)


def _ffn_kernel(xe_ref, wg_ref, wu_ref, wd_ref, y_ref, acc_ref):
    f = pl.program_id(1)

    last = D_FF // FF_TILE - 1

    def partial_out():
        xe = xe_ref[...]
        out = None
        for c0 in range(0, FF_TILE, FF_GROUP):
            a = jnp.dot(xe, wg_ref[:, c0:c0 + FF_GROUP].astype(bf16), preferred_element_type=f32)
            u = jnp.dot(xe, wu_ref[:, c0:c0 + FF_GROUP].astype(bf16), preferred_element_type=f32)
            hidden = (a * (1.0 / (1.0 + jnp.exp(-a))) * u).astype(bf16)
            part = jnp.dot(hidden, wd_ref[pl.ds(c0, FF_GROUP), :].astype(bf16),
                           preferred_element_type=f32)
            out = part if out is None else out + part
        return out

    @pl.when(f == 0)
    def _():
        acc_ref[...] = partial_out()

    @pl.when((f > 0) & (f < last))
    def _():
        acc_ref[...] += partial_out()

    @pl.when(f == last)
    def _():
        y_ref[...] = (acc_ref[...] + partial_out()).astype(bf16)


def _ffn(xe, w_gate, w_up, w_down):
    rows = BATCH * CAPACITY
    return pl.pallas_call(
        _ffn_kernel,
        grid=(N_EXPERTS, D_FF // FF_TILE),
        in_specs=[
            pl.BlockSpec((None, rows, D_MODEL), lambda e, f: (e, 0, 0)),
            pl.BlockSpec((None, D_MODEL, FF_TILE), lambda e, f: (e, 0, f)),
            pl.BlockSpec((None, D_MODEL, FF_TILE), lambda e, f: (e, 0, f)),
            pl.BlockSpec((None, FF_TILE, D_MODEL), lambda e, f: (e, f, 0)),
        ],
        out_specs=pl.BlockSpec((None, rows, D_MODEL), lambda e, f: (e, 0, 0)),
        out_shape=jax.ShapeDtypeStruct((N_EXPERTS, rows, D_MODEL), bf16),
        scratch_shapes=[pltpu.VMEM((rows, D_MODEL), f32)],
        compiler_params=_params(("arbitrary", "arbitrary"), 56),
        name="ffn",
    )(xe, w_gate, w_up, w_down)


SCATTER_UNROLL = 8
COPY_ROWS = 2048
NORM_TOKENS = 256


def _combine_kernel(row_ref, gate_ref, y_ref, x1_ref, g_ref, o_ref, acc_ref, tile_ref):
    step = pl.program_id(1)

    @pl.when(step == 0)
    def _():
        def clear(i, c):
            rows = pl.ds(pl.multiple_of(i * COPY_ROWS, COPY_ROWS), COPY_ROWS)
            acc_ref[rows, :] = jnp.zeros((COPY_ROWS, LANES), f32)
            return c
        lax.fori_loop(0, SEQ * SLAB_ROWS // COPY_ROWS, clear, 0)

    for k in range(EXPERTS_PER_STEP):
        for j in range(SLAB_ROWS):
            tile_ref[k, pl.ds(j * TILE_PITCH, CAPACITY), :] = (
                y_ref[k, :, j * LANES:(j + 1) * LANES].astype(f32))
        for first in range(0, CAPACITY, SCATTER_UNROLL):
            updates = []
            for slot in range(first, first + SCATTER_UNROLL):
                row = row_ref[0, k * CAPACITY + slot]
                gate = gate_ref[0, k * CAPACITY + slot]
                rows = pl.ds(pl.multiple_of(row, SLAB_ROWS), SLAB_ROWS)
                contrib = tile_ref[k, pl.ds(slot, SLAB_ROWS, stride=TILE_PITCH), :]
                updates.append((rows, acc_ref[rows, :] + gate * contrib))
            for rows, value in updates:
                acc_ref[rows, :] = value

    @pl.when(step == N_EXPERTS // EXPERTS_PER_STEP - 1)
    def _():
        def norm(i, c):
            first = i * (NORM_TOKENS * SLAB_ROWS)
            rows = pl.ds(pl.multiple_of(i * NORM_TOKENS, NORM_TOKENS), NORM_TOKENS)
            chunks = [x1_ref[rows, j * LANES:(j + 1) * LANES]
                      + acc_ref[pl.ds(first + j, NORM_TOKENS, stride=SLAB_ROWS), :]
                      for j in range(SLAB_ROWS)]
            squares = functools.reduce(lambda a, b: a + b, [ch * ch for ch in chunks])
            inv = lax.rsqrt(jnp.sum(squares, axis=1, keepdims=True) * (1.0 / D_MODEL) + RMS_EPS)
            for j in range(SLAB_ROWS):
                cols = slice(j * LANES, (j + 1) * LANES)
                o_ref[rows, cols] = chunks[j] * inv * g_ref[:, cols]
            return c
        lax.fori_loop(0, SEQ // NORM_TOKENS, norm, 0)


def _combine(rows, gate, y, x1, g_final):
    return pl.pallas_call(
        _combine_kernel,
        grid=(BATCH, N_EXPERTS // EXPERTS_PER_STEP),
        in_specs=[
            _slot_list_spec(), _slot_list_spec(),
            pl.BlockSpec((EXPERTS_PER_STEP, None, CAPACITY, D_MODEL), lambda b, s: (s, b, 0, 0)),
            pl.BlockSpec((SEQ, D_MODEL), lambda b, s: (b, 0)),
            pl.BlockSpec((1, D_MODEL), lambda b, s: (0, 0)),
        ],
        out_specs=pl.BlockSpec((SEQ, D_MODEL), lambda b, s: (b, 0)),
        out_shape=jax.ShapeDtypeStruct((TOKENS, D_MODEL), f32),
        scratch_shapes=[pltpu.VMEM((SEQ * SLAB_ROWS, LANES), f32),
                        pltpu.VMEM((EXPERTS_PER_STEP, SLAB_ROWS * TILE_PITCH, LANES), f32)],
        compiler_params=_params(("arbitrary", "arbitrary"), 56),
        name="combine",
    )(rows, gate, y, x1, g_final)


def _rope_freqs():
    inv_freq = ROPE_THETA ** (-jnp.arange(ROT_HALF, dtype=f32) / ROT_HALF)
    return inv_freq.reshape(ROT_HALF, 1)


def kernel(x, positions, g_mix, w_in, conv_w, g_conv_out, g_attn_out, w_out, g_ffn, w_router,
           w_gate, w_up, w_down, g_final):
    x2 = x.reshape(TOKENS, D_MODEL)
    rope = _rope_tables(positions.reshape(BATCH, 1, SEQ), _rope_freqs())
    yc, edge, qkv, norms = _in_proj(x2, g_mix[0].reshape(1, D_MODEL), w_in[0].astype(bf16),
                                    conv_w[0], rope)
    norms = norms.reshape(BATCH, SEQ // ROW_TILE, SUBLANES, LANES)[:, :, 0, :2 * N_HEADS]
    norms = norms.max(axis=1).reshape(BATCH, 2, N_HEADS)
    scores_bounded = jnp.all(norms[:, 0] * norms[:, 1] * NORM_SLACK <= SAFE_SCORE ** 2)
    qkv3 = qkv.reshape(BATCH, SEQ, 3 * ATTN_WIDTH)
    ya = lax.cond(scores_bounded, functools.partial(_attention, exact_max=False),
                  functools.partial(_attention, exact_max=True), qkv3)
    x1, h2, logits_t = _out_proj(yc, edge, ya.reshape(TOKENS, ATTN_WIDTH), x2, w_out[0].astype(bf16),
                                 g_conv_out[0].reshape(1, CONV_WIDTH),
                                 g_attn_out[0].reshape(1, ATTN_WIDTH), g_ffn[0].reshape(1, D_MODEL),
                                 w_router[0].T)
    idx, gate = _route(*_select(logits_t))
    idx = idx.reshape(BATCH, 1, N_EXPERTS * CAPACITY)
    gate = gate.reshape(BATCH, 1, N_EXPERTS * CAPACITY)
    xe = _gather(idx, h2)
    y = _ffn(xe.reshape(N_EXPERTS, BATCH * CAPACITY, D_MODEL), w_gate[0], w_up[0], w_down[0])
    out = _combine(idx, gate, y.reshape(N_EXPERTS, BATCH, CAPACITY, D_MODEL), x1,
                   g_final.reshape(1, D_MODEL))
    return out.reshape(BATCH, SEQ, D_MODEL)
```

```python
import functools

import jax
import jax.numpy as jnp
from jax import lax
from jax.experimental import pallas as pl
from jax.experimental.pallas import tpu as pltpu

D_MODEL = 1024
BATCH = 8
SEQ = 2048
TOKENS = BATCH * SEQ
CONV_WIDTH = 512
ATTN_WIDTH = 512
HEAD_DIM = 64
N_HEADS = 8
DILATED_PATTERNS = ((128, 1), (512, 4), (2048, 16))
N_PATTERNS = len(DILATED_PATTERNS)
ROPE_THETA = 500000.0
ROT_DIM = 16
ROT_HALF = ROT_DIM // 2
N_EXPERTS = 16
CAPACITY = 2 * SEQ // N_EXPERTS
D_FF = 2 * D_MODEL
PROJ_WIDTH = 3 * CONV_WIDTH + 3 * ATTN_WIDTH
RMS_EPS = 1e-6
NEG_INF = -1e30

LANES = 128
ROW_TILE = 1024
FF_TILE = 512
FF_GROUP = 256
BAND_HALF = 64
Q_CHUNK = 2 * BAND_HALF
K_WIN = 4 * BAND_HALF
HEADS_PER_STEP = LANES // HEAD_DIM
SUBLANES = 8
SLAB_ROWS = D_MODEL // LANES
TILE_PITCH = CAPACITY + SUBLANES
SLOT_SPLIT = 16
EXPERTS_PER_STEP = 4
MIB = 1024 * 1024

f32 = jnp.float32
bf16 = jnp.bfloat16


def _params(semantics, vmem_mib):
    return pltpu.CompilerParams(dimension_semantics=semantics, vmem_limit_bytes=vmem_mib * MIB)


def _rms(x, g):
    return x * lax.rsqrt(jnp.mean(x * x, axis=-1, keepdims=True) + RMS_EPS) * g


def _rope_table_kernel(pos_ref, invf_ref, cos_ref, sin_ref):
    ang = invf_ref[...] * pos_ref[...].astype(f32)
    trig = jnp.concatenate([jnp.cos(ang), jnp.sin(ang)], axis=0)
    lane = lax.broadcasted_iota(jnp.int32, (ROT_DIM, 2 * LANES), 1)
    row = lax.broadcasted_iota(jnp.int32, (ROT_DIM, 2 * LANES), 0)
    freq = row % ROT_HALF
    is_sin = row >= ROT_HALF
    at_lo = lane % HEAD_DIM == freq
    at_hi = lane % HEAD_DIM == freq + ROT_HALF
    table = lane // LANES
    place = (jnp.where((table == 0) & jnp.logical_not(is_sin) & (at_lo | at_hi), 1.0, 0.0)
             + jnp.where((table == 1) & is_sin & at_hi, 1.0, 0.0)
             - jnp.where((table == 1) & is_sin & at_lo, 1.0, 0.0)).astype(bf16)
    spread = jnp.zeros((SEQ, 2 * LANES), f32)
    rest = trig
    for _ in range(3):
        piece = rest.astype(bf16)
        rest = rest - piece.astype(f32)
        spread = spread + lax.dot_general(piece, place, (((0,), (0,)), ((), ())),
                                          preferred_element_type=f32)
    unrotated = jnp.where(
        lax.broadcasted_iota(jnp.int32, (1, LANES), 1) % HEAD_DIM >= ROT_DIM, 1.0, 0.0)
    cos_ref[...] = spread[:, :LANES] + unrotated
    sin_ref[...] = spread[:, LANES:]


def _rope_tables(pos_row, invf_col):
    blk = pl.BlockSpec((SEQ, LANES), lambda b: (b, 0))
    return pl.pallas_call(
        _rope_table_kernel,
        grid=(BATCH,),
        in_specs=[
            pl.BlockSpec((None, 1, SEQ), lambda b: (b, 0, 0)),
            pl.BlockSpec((ROT_HALF, 1), lambda b: (0, 0)),
        ],
        out_specs=[blk, blk],
        out_shape=[jax.ShapeDtypeStruct((TOKENS, LANES), f32)] * 2,
        compiler_params=_params(("arbitrary",), 32),
        name="rope_tables",
    )(pos_row, invf_col)


EDGE_U_FIRST, EDGE_U_LAST, EDGE_B_FIRST, EDGE_B_LAST = (k * SUBLANES for k in range(4))
EDGE_ROWS = 4 * SUBLANES


def _in_proj_kernel(x_ref, g_ref, w_ref, cw_ref, cos_ref, sin_ref,
                    yc_ref, edge_ref, qkv_ref, norm_ref):
    h = _rms(x_ref[...], g_ref[...]).astype(bf16)
    p = jnp.dot(h, w_ref[...], preferred_element_type=f32)
    gate_b = p[:, :CONV_WIDTH]
    u = p[:, CONV_WIDTH:2 * CONV_WIDTH] * p[:, 2 * CONV_WIDTH:3 * CONV_WIDTH]
    cw = cw_ref[...]
    row = lax.broadcasted_iota(jnp.int32, (ROW_TILE, CONV_WIDTH), 0)
    before = jnp.where(row == 0, 0.0, pltpu.roll(u, 1, 0))
    after = jnp.where(row == ROW_TILE - 1, 0.0, pltpu.roll(u, ROW_TILE - 1, 0))
    yc_ref[...] = gate_b * (cw[0:1] * before + cw[1:2] * u + cw[2:3] * after)
    last = ROW_TILE - SUBLANES
    edge_ref[pl.ds(EDGE_U_FIRST, SUBLANES), :] = u[:SUBLANES]
    edge_ref[pl.ds(EDGE_U_LAST, SUBLANES), :] = u[last:]
    edge_ref[pl.ds(EDGE_B_FIRST, SUBLANES), :] = gate_b[:SUBLANES] * cw[0:1]
    edge_ref[pl.ds(EDGE_B_LAST, SUBLANES), :] = gate_b[last:] * cw[2:3]
    cs, sn = cos_ref[...], sin_ref[...]
    upper = lax.broadcasted_iota(jnp.int32, (ROW_TILE, LANES), 1) % HEAD_DIM >= ROT_HALF
    s1 = jnp.where(upper, sn, 0.0)
    s2 = jnp.where(upper, 0.0, sn)
    squares = []
    for blk in range(2 * ATTN_WIDTH // LANES):
        x = p[:, 3 * CONV_WIDTH + blk * LANES:3 * CONV_WIDTH + (blk + 1) * LANES]
        y = x * cs + pltpu.roll(x, ROT_HALF, 1) * s1 + pltpu.roll(x, LANES - ROT_HALF, 1) * s2
        if blk < ATTN_WIDTH // LANES:
            y = y * HEAD_DIM ** -0.5
        qkv_ref[:, blk * LANES:(blk + 1) * LANES] = y
        squares.append((y * y).astype(bf16))
    qkv_ref[:, 2 * ATTN_WIDTH:] = p[:, 3 * CONV_WIDTH + 2 * ATTN_WIDTH:]
    head_of = jnp.where(
        lax.broadcasted_iota(jnp.int32, (2 * ATTN_WIDTH, LANES), 0) // HEAD_DIM
        == lax.broadcasted_iota(jnp.int32, (2 * ATTN_WIDTH, LANES), 1), 1.0, 0.0).astype(bf16)
    row_norms = jnp.dot(jnp.concatenate(squares, axis=1), head_of, preferred_element_type=f32)
    norm_ref[...] = jnp.broadcast_to(jnp.max(row_norms, axis=0, keepdims=True), (SUBLANES, LANES))


def _in_proj(x2, g_mix, w_in_bf, conv_w, rope):
    n = TOKENS // ROW_TILE
    table = pl.BlockSpec((ROW_TILE, LANES), lambda i: (i, 0))
    return pl.pallas_call(
        _in_proj_kernel,
        grid=(n,),
        in_specs=[
            pl.BlockSpec((ROW_TILE, D_MODEL), lambda i: (i, 0)),
            pl.BlockSpec((1, D_MODEL), lambda i: (0, 0)),
            pl.BlockSpec((D_MODEL, PROJ_WIDTH), lambda i: (0, 0)),
            pl.BlockSpec((3, CONV_WIDTH), lambda i: (0, 0)),
            table, table,
        ],
        out_specs=[
            pl.BlockSpec((ROW_TILE, CONV_WIDTH), lambda i: (i, 0)),
            pl.BlockSpec((EDGE_ROWS, CONV_WIDTH), lambda i: (i, 0)),
            pl.BlockSpec((ROW_TILE, 3 * ATTN_WIDTH), lambda i: (i, 0)),
            pl.BlockSpec((SUBLANES, LANES), lambda i: (i, 0)),
        ],
        out_shape=[
            jax.ShapeDtypeStruct((TOKENS, CONV_WIDTH), f32),
            jax.ShapeDtypeStruct((n * EDGE_ROWS, CONV_WIDTH), f32),
            jax.ShapeDtypeStruct((TOKENS, 3 * ATTN_WIDTH), f32),
            jax.ShapeDtypeStruct((n * SUBLANES, LANES), f32),
        ],
        compiler_params=_params(("arbitrary",), 48),
        name="in_proj",
    )(x2, g_mix, w_in_bf, conv_w, *rope)


CAP_INTERIOR, CAP_FIRST, CAP_LAST, CAP_SINGLE = range(4)
MASK_OPEN = float(jnp.finfo(jnp.float32).max)
MID_DIL = DILATED_PATTERNS[1][1]
MID_LEN = SEQ // MID_DIL
assert [d for _, d in DILATED_PATTERNS] == [1, MID_DIL, MID_DIL * MID_DIL]
assert all(w // 2 // d == BAND_HALF for w, d in DILATED_PATTERNS)


SAFE_SCORE = 40.0
NORM_SLACK = 1.1


def _attn_kernel(q_ref, k_ref, v_ref, o_ref, cap_ref, q4_ref, k4_ref, v4_ref,
                 qlo_ref, qhi_ref, kpad_ref, vlo_ref, vhi_ref,
                 num_ref, m_ref, l_ref, *, exact_max):
    @pl.when(pl.program_id(1) == 0)
    def _():
        row = lax.broadcasted_iota(jnp.int32, (Q_CHUNK, K_WIN), 0)
        col = lax.broadcasted_iota(jnp.int32, (Q_CHUNK, K_WIN), 1)
        band = (col - row >= 0) & (col - row <= 2 * BAND_HALF)
        not_before = col >= BAND_HALF
        not_after = col < K_WIN - BAND_HALF
        for idx, ok in ((CAP_INTERIOR, band), (CAP_FIRST, band & not_before),
                        (CAP_LAST, band & not_after), (CAP_SINGLE, band & not_before & not_after)):
            cap_ref[idx] = jnp.where(ok, MASK_OPEN, NEG_INF)

    lane_lo = lax.broadcasted_iota(jnp.int32, (Q_CHUNK, LANES), 1) < HEAD_DIM
    zero_pad = jnp.zeros((BAND_HALF, LANES), bf16)

    def zero_rows(start):
        for ref in (kpad_ref, vlo_ref, vhi_ref):
            ref[pl.ds(start, BAND_HALF), :] = zero_pad

    def stage(sources, src, q_row, pad_row, keep=None):
        qf, kf, vf = (ref[src, :] for ref in sources)
        if keep is not None:
            for ref, val in zip((q4_ref, k4_ref, v4_ref), (qf, kf, vf)):
                ref[keep, :] = val
        q_dst, p_dst = pl.ds(q_row, Q_CHUNK), pl.ds(pad_row, Q_CHUNK)
        qlo_ref[q_dst, :] = jnp.where(lane_lo, qf, 0.0).astype(bf16)
        qhi_ref[q_dst, :] = jnp.where(lane_lo, 0.0, qf).astype(bf16)
        kpad_ref[p_dst, :] = kf.astype(bf16)
        vlo_ref[p_dst, :] = jnp.where(lane_lo, vf, 0.0).astype(bf16)
        vhi_ref[p_dst, :] = jnp.where(lane_lo, 0.0, vf).astype(bf16)

    def chunk(pat, out_rows, q_row, pad_row, cap):
        q2 = jnp.concatenate([qlo_ref[pl.ds(q_row, Q_CHUNK), :], qhi_ref[pl.ds(q_row, Q_CHUNK), :]],
                             axis=0)
        s = lax.dot_general(q2, kpad_ref[pl.ds(pad_row, K_WIN), :], (((1,), (1,)), ((), ())),
                            preferred_element_type=f32)
        caps = cap_ref[cap]
        s = jnp.minimum(s, jnp.concatenate([caps, caps], axis=0))
        if exact_max:
            m = jnp.max(s, axis=1, keepdims=True)
            p = jnp.exp(s - m)
        else:
            p = jnp.exp(s)
        l = jnp.sum(p, axis=1, keepdims=True)
        pb = p.astype(bf16)
        num = (jnp.dot(pb[:Q_CHUNK], vlo_ref[pl.ds(pad_row, K_WIN), :], preferred_element_type=f32)
               + jnp.dot(pb[Q_CHUNK:], vhi_ref[pl.ds(pad_row, K_WIN), :], preferred_element_type=f32))
        num_ref[pat, out_rows, :] = num
        if exact_max:
            m_ref[pat, out_rows, :] = jnp.where(lane_lo, m[:Q_CHUNK], m[Q_CHUNK:])
        l_ref[pat, out_rows, :] = jnp.where(lane_lo, l[:Q_CHUNK], l[Q_CHUNK:])

    def run_pattern(pat, sources, classes, length, keep_f32=False, first_slot=0):
        n_chunks = length // Q_CHUNK
        region = length + 2 * BAND_HALF

        def rows(first, stride, i):
            if stride == 1:
                return pl.ds(first + i * Q_CHUNK, Q_CHUNK)
            return pl.ds(first + stride * Q_CHUNK * i, Q_CHUNK, stride=stride)

        slots = range(first_slot, first_slot + len(classes))
        for u in slots:
            zero_rows(u * region)
            zero_rows(u * region + BAND_HALF + length)
        for u, (src0, src_stride, _, _) in zip(slots, classes):
            for i in range(n_chunks):
                keep = pl.ds(u * length + i * Q_CHUNK, Q_CHUNK) if keep_f32 else None
                stage(sources, rows(src0, src_stride, i), u * length + i * Q_CHUNK,
                      u * region + BAND_HALF + i * Q_CHUNK, keep)
        for u, (_, _, out0, out_stride) in zip(slots, classes):
            for j in range(n_chunks):
                if n_chunks == 1:
                    cap = CAP_SINGLE
                else:
                    cap = CAP_FIRST if j == 0 else CAP_LAST if j == n_chunks - 1 else CAP_INTERIOR
                chunk(pat, rows(out0, out_stride, j), u * length + j * Q_CHUNK,
                      u * region + j * Q_CHUNK, cap)

    def mix(residue, part):
        mid_rows = pl.ds(residue * MID_LEN + part * Q_CHUNK, Q_CHUNK)
        pos_rows = pl.ds(residue + MID_DIL * Q_CHUNK * part, Q_CHUNK, stride=MID_DIL)
        rows = (pos_rows, mid_rows, mid_rows)
        num = jnp.zeros((Q_CHUNK, LANES), f32)
        den = jnp.zeros((Q_CHUNK, LANES), f32)
        if exact_max:
            ms = [m_ref[p, rows[p], :] for p in range(N_PATTERNS)]
            m_all = functools.reduce(jnp.maximum, ms)
        for p in range(N_PATTERNS):
            a = jnp.exp(ms[p] - m_all) if exact_max else 1.0
            den = den + a * l_ref[p, rows[p], :]
            num = num + a * num_ref[p, rows[p], :]
        o_ref[pos_rows, :] = num / den

    inputs = (q_ref, k_ref, v_ref)
    run_pattern(0, inputs, [(0, 1, 0, 1)], SEQ)
    run_pattern(1, inputs, [(r, MID_DIL, r * MID_LEN, 1) for r in range(MID_DIL)], MID_LEN,
                keep_f32=True)
    for r in range(MID_DIL):
        run_pattern(2, (q4_ref, k4_ref, v4_ref),
                    [(r * MID_LEN + a, MID_DIL, r * MID_LEN + a, MID_DIL) for a in range(MID_DIL)],
                    MID_LEN // MID_DIL, first_slot=r * MID_DIL)
        for part in range(MID_LEN // Q_CHUNK):
            mix(r, part)


def _attention(qkv3, exact_max):
    n_hp = N_HEADS // HEADS_PER_STEP
    blk = lambda off: pl.BlockSpec((None, SEQ, LANES), lambda b, h, off=off: (b, 0, off + h))
    pad_rows = max(SEQ + 2 * BAND_HALF * d for _, d in DILATED_PATTERNS)
    stats = pltpu.VMEM((N_PATTERNS, SEQ, LANES), f32)
    rows_f32 = pltpu.VMEM((SEQ, LANES), f32)
    return pl.pallas_call(
        functools.partial(_attn_kernel, exact_max=exact_max),
        grid=(BATCH, n_hp),
        in_specs=[blk(0), blk(n_hp), blk(2 * n_hp)],
        out_specs=pl.BlockSpec((None, SEQ, LANES), lambda b, h: (b, 0, h)),
        out_shape=jax.ShapeDtypeStruct((BATCH, SEQ, ATTN_WIDTH), f32),
        scratch_shapes=[
            pltpu.VMEM((4, Q_CHUNK, K_WIN), f32),
            rows_f32, rows_f32, rows_f32,
            pltpu.VMEM((SEQ, LANES), bf16), pltpu.VMEM((SEQ, LANES), bf16),
            pltpu.VMEM((pad_rows, LANES), bf16), pltpu.VMEM((pad_rows, LANES), bf16),
            pltpu.VMEM((pad_rows, LANES), bf16),
            stats, stats, stats,
        ],
        compiler_params=_params(("arbitrary", "arbitrary"), 48),
        name="attention_exact" if exact_max else "attention",
    )(qkv3, qkv3, qkv3)


def _store_token_major(ref, x):
    for j in range(SLAB_ROWS):
        ref[pl.ds(j, x.shape[0], stride=SLAB_ROWS), :] = x[:, j * LANES:(j + 1) * LANES]


def _dot_nt(a, b):
    return lax.dot_general(a, b, (((1,), (1,)), ((), ())), preferred_element_type=f32)


def _out_proj_kernel(yc_ref, edge_ref, edge_prev_ref, edge_next_ref, ya_ref, x_ref, w_ref,
                     gc_ref, ga_ref, gf_ref, wr_ref, x1_ref, h2_ref, lg_ref):
    tile = pl.program_id(0) % (SEQ // ROW_TILE)
    from_prev = (edge_ref[pl.ds(EDGE_B_FIRST, 1), :]
                 * edge_prev_ref[pl.ds(EDGE_U_LAST + SUBLANES - 1, 1), :])
    from_next = (edge_ref[pl.ds(EDGE_B_LAST + SUBLANES - 1, 1), :]
                 * edge_next_ref[pl.ds(EDGE_U_FIRST, 1), :])
    from_prev = jnp.where(tile == 0, 0.0, from_prev)
    from_next = jnp.where(tile == SEQ // ROW_TILE - 1, 0.0, from_next)
    row = lax.broadcasted_iota(jnp.int32, (ROW_TILE, CONV_WIDTH), 0)
    y_conv = (yc_ref[...] + jnp.where(row == 0, from_prev, 0.0)
              + jnp.where(row == ROW_TILE - 1, from_next, 0.0))
    yc = _rms(y_conv, gc_ref[...]).astype(bf16)
    ya = _rms(ya_ref[...], ga_ref[...]).astype(bf16)
    mix = (jnp.dot(yc, w_ref[pl.ds(0, CONV_WIDTH), :], preferred_element_type=f32)
           + jnp.dot(ya, w_ref[pl.ds(CONV_WIDTH, ATTN_WIDTH), :], preferred_element_type=f32))
    x1 = x_ref[...] + mix
    x1_ref[...] = x1
    h2 = _rms(x1, gf_ref[...])
    _store_token_major(h2_ref, h2)
    wr = wr_ref[...]
    wr_hi = wr.astype(bf16)
    wr_lo = (wr - wr_hi.astype(f32)).astype(bf16)
    h2_hi = h2.astype(bf16)
    h2_lo = (h2 - h2_hi.astype(f32)).astype(bf16)
    both = _dot_nt(jnp.concatenate([wr_hi, wr_lo], axis=0), h2_hi)
    lg_ref[...] = both[:N_EXPERTS] + both[N_EXPERTS:] + _dot_nt(wr_hi, h2_lo)


def _out_proj(yc, edge, ya, x2, w_out_bf, g_conv, g_attn, g_ffn, w_router_t):
    n = TOKENS // ROW_TILE
    slab = pl.BlockSpec((ROW_TILE * SLAB_ROWS, LANES), lambda i: (i, 0))
    slab_shape = jax.ShapeDtypeStruct((TOKENS * SLAB_ROWS, LANES), f32)
    return pl.pallas_call(
        _out_proj_kernel,
        grid=(n,),
        in_specs=[
            pl.BlockSpec((ROW_TILE, CONV_WIDTH), lambda i: (i, 0)),
            pl.BlockSpec((EDGE_ROWS, CONV_WIDTH), lambda i: (i, 0)),
            pl.BlockSpec((EDGE_ROWS, CONV_WIDTH), lambda i: (jnp.maximum(i - 1, 0), 0)),
            pl.BlockSpec((EDGE_ROWS, CONV_WIDTH), lambda i: (jnp.minimum(i + 1, n - 1), 0)),
            pl.BlockSpec((ROW_TILE, ATTN_WIDTH), lambda i: (i, 0)),
            pl.BlockSpec((ROW_TILE, D_MODEL), lambda i: (i, 0)),
            pl.BlockSpec((D_MODEL, D_MODEL), lambda i: (0, 0)),
            pl.BlockSpec((1, CONV_WIDTH), lambda i: (0, 0)),
            pl.BlockSpec((1, ATTN_WIDTH), lambda i: (0, 0)),
            pl.BlockSpec((1, D_MODEL), lambda i: (0, 0)),
            pl.BlockSpec((N_EXPERTS, D_MODEL), lambda i: (0, 0)),
        ],
        out_specs=[pl.BlockSpec((ROW_TILE, D_MODEL), lambda i: (i, 0)), slab,
                   pl.BlockSpec((N_EXPERTS, ROW_TILE), lambda i: (0, i))],
        out_shape=[jax.ShapeDtypeStruct((TOKENS, D_MODEL), f32), slab_shape,
                   jax.ShapeDtypeStruct((N_EXPERTS, TOKENS), f32)],
        compiler_params=_params(("arbitrary",), 48),
        name="out_proj",
    )(yc, edge, edge, edge, ya, x2, w_out_bf, g_conv, g_attn, g_ffn, w_router_t)


N_LANE_BLOCKS = SEQ // LANES


def _excl_cumsum_tokens(x, tri):
    n = x.shape[0]
    stacked = jnp.concatenate([x[:, j * LANES:(j + 1) * LANES] for j in range(N_LANE_BLOCKS)], axis=0)
    within = jnp.dot(stacked.astype(bf16), tri, preferred_element_type=f32)
    totals = jnp.sum(stacked, axis=1, keepdims=True)
    out, offset = [], jnp.zeros((n, 1), f32)
    for j in range(N_LANE_BLOCKS):
        rows = slice(j * n, (j + 1) * n)
        out.append(within[rows] + offset)
        offset = offset + totals[rows]
    return jnp.concatenate(out, axis=1)


def _select_kernel(lg_ref, rank_ref, aff_ref):
    affs = []
    for b in range(BATCH):
        x = lg_ref[:, b * SEQ:(b + 1) * SEQ]
        e = jnp.exp(x - jnp.max(x, axis=0, keepdims=True))
        affs.append(e / jnp.sum(e, axis=0, keepdims=True))
    aff = jnp.concatenate(affs, axis=0)
    aff_ref[...] = aff
    thr = jnp.zeros((BATCH * N_EXPERTS, 1), jnp.int32)
    for bit in range(30, -1, -1):
        cand = thr | (1 << bit)
        hit = aff >= pltpu.bitcast(cand, f32)
        cnt = jnp.sum(jnp.where(hit, 1.0, 0.0), axis=1, keepdims=True)
        thr = jnp.where(cnt >= float(CAPACITY), cand, thr)
    above = aff >= pltpu.bitcast(thr + 1, f32)
    tie = (aff >= pltpu.bitcast(thr, f32)) & jnp.logical_not(above)
    need = float(CAPACITY) - jnp.sum(jnp.where(above, 1.0, 0.0), axis=1, keepdims=True)
    tri = jnp.where(lax.broadcasted_iota(jnp.int32, (LANES, LANES), 0)
                    < lax.broadcasted_iota(jnp.int32, (LANES, LANES), 1), 1.0, 0.0).astype(bf16)
    tie_rank = _excl_cumsum_tokens(jnp.where(tie, 1.0, 0.0), tri)
    sel = above | (tie & (tie_rank < need))
    rank_ref[...] = jnp.where(sel, _excl_cumsum_tokens(jnp.where(sel, 1.0, 0.0), tri), -1.0)


def _select(logits_t):
    shape = jax.ShapeDtypeStruct((BATCH * N_EXPERTS, SEQ), f32)
    blk = pl.BlockSpec((BATCH * N_EXPERTS, SEQ), lambda i: (0, 0))
    return pl.pallas_call(
        _select_kernel,
        grid=(1,),
        in_specs=[pl.BlockSpec((N_EXPERTS, TOKENS), lambda i: (0, 0))],
        out_specs=[blk, blk],
        out_shape=[shape, shape],
        compiler_params=_params(("arbitrary",), 40),
        name="select",
    )(logits_t)


def _route_kernel(rank_ref, aff_ref, idx_ref, gate_ref):
    rank = rank_ref[...]
    aff = aff_ref[...]
    high = jnp.floor(rank * (1.0 / SLOT_SPLIT))
    low = rank - high * SLOT_SPLIT
    digit = lax.broadcasted_iota(jnp.int32, (SLOT_SPLIT, SEQ), 0).astype(f32)
    tok = lax.broadcasted_iota(jnp.int32, (1, SEQ), 1)
    g1 = aff.astype(bf16).astype(f32)
    g2 = (aff - g1).astype(bf16).astype(f32)
    g3 = aff - g1 - g2
    per_expert = lambda v: jnp.broadcast_to(v.astype(f32), (N_EXPERTS, SEQ))
    payloads = [per_expert(tok // LANES), per_expert(tok % LANES), g1, g2, g3]
    high_rows = jnp.concatenate(
        [jnp.where(high[e:e + 1] == digit, 1.0, 0.0) for e in range(N_EXPERTS)], axis=0).astype(bf16)
    low_hit = [low[e:e + 1] == digit for e in range(N_EXPERTS)]
    n_rows = N_EXPERTS * SLOT_SPLIT
    same_expert = (lax.broadcasted_iota(jnp.int32, (n_rows, n_rows), 0) // SLOT_SPLIT
                   == lax.broadcasted_iota(jnp.int32, (n_rows, n_rows), 1) // SLOT_SPLIT)
    fold = jnp.where(lax.broadcasted_iota(jnp.int32, (n_rows, SLOT_SPLIT), 0) % SLOT_SPLIT
                     == lax.broadcasted_iota(jnp.int32, (n_rows, SLOT_SPLIT), 1), 1.0, 0.0).astype(bf16)
    folded = []
    for val in payloads:
        rows = jnp.concatenate(
            [jnp.where(low_hit[e], val[e:e + 1], 0.0) for e in range(N_EXPERTS)], axis=0).astype(bf16)
        picked = jnp.where(same_expert, _dot_nt(high_rows, rows), 0.0).astype(bf16)
        folded.append(jnp.dot(picked, fold, preferred_element_type=f32))
    idx_ref[...] = ((folded[0] * float(LANES) + folded[1]) * float(SLAB_ROWS)).astype(jnp.int32)
    gate_ref[...] = folded[2] + folded[3] + folded[4]


def _route(rank, aff):
    n_rows = N_EXPERTS * SLOT_SPLIT
    blk = pl.BlockSpec((None, n_rows, CAPACITY // SLOT_SPLIT), lambda b: (b, 0, 0))
    shape = (BATCH, n_rows, CAPACITY // SLOT_SPLIT)
    per_seq = pl.BlockSpec((N_EXPERTS, SEQ), lambda b: (b, 0))
    return pl.pallas_call(
        _route_kernel,
        grid=(BATCH,),
        in_specs=[per_seq, per_seq],
        out_specs=[blk, blk],
        out_shape=[jax.ShapeDtypeStruct(shape, jnp.int32), jax.ShapeDtypeStruct(shape, f32)],
        compiler_params=_params(("arbitrary",), 40),
        name="route",
    )(rank, aff)


def _slot_list_spec():
    return pl.BlockSpec((None, 1, EXPERTS_PER_STEP * CAPACITY), lambda b, s: (b, 0, s),
                        memory_space=pltpu.SMEM)


def _gather_kernel(row_ref, h2_ref, xe_ref, tile_ref):
    def expert(e, c):
        base = e * CAPACITY
        for slot in range(CAPACITY):
            row = row_ref[0, base + slot]
            slab = h2_ref[pl.ds(pl.multiple_of(row, SLAB_ROWS), SLAB_ROWS), :]
            tile_ref[pl.ds(slot, SLAB_ROWS, stride=TILE_PITCH), :] = slab
        for j in range(SLAB_ROWS):
            xe_ref[e, :, j * LANES:(j + 1) * LANES] = (
                tile_ref[pl.ds(j * TILE_PITCH, CAPACITY), :].astype(bf16))
        return c
    lax.fori_loop(0, N_EXPERTS, expert, 0)


def _gather(rows, h2_slab):
    return pl.pallas_call(
        _gather_kernel,
        grid=(BATCH,),
        in_specs=[
            pl.BlockSpec((None, 1, N_EXPERTS * CAPACITY), lambda b: (b, 0, 0),
                         memory_space=pltpu.SMEM),
            pl.BlockSpec((SEQ * SLAB_ROWS, LANES), lambda b: (b, 0)),
        ],
        out_specs=pl.BlockSpec((N_EXPERTS, None, CAPACITY, D_MODEL), lambda b: (0, b, 0, 0)),
        out_shape=jax.ShapeDtypeStruct((N_EXPERTS, BATCH, CAPACITY, D_MODEL), bf16),
        scratch_shapes=[pltpu.VMEM((SLAB_ROWS * TILE_PITCH, LANES), f32)],
        compiler_params=_params(("arbitrary",), 48),
        name="gather",
    )(rows, h2_slab)


def _ffn_kernel(xe_ref, wg_ref, wu_ref, wd_ref, y_ref, acc_ref):
    f = pl.program_id(1)

    last = D_FF // FF_TILE - 1

    def partial_out():
        xe = xe_ref[...]
        out = None
        for c0 in range(0, FF_TILE, FF_GROUP):
            a = jnp.dot(xe, wg_ref[:, c0:c0 + FF_GROUP].astype(bf16), preferred_element_type=f32)
            u = jnp.dot(xe, wu_ref[:, c0:c0 + FF_GROUP].astype(bf16), preferred_element_type=f32)
            hidden = (a * (1.0 / (1.0 + jnp.exp(-a))) * u).astype(bf16)
            part = jnp.dot(hidden, wd_ref[pl.ds(c0, FF_GROUP), :].astype(bf16),
                           preferred_element_type=f32)
            out = part if out is None else out + part
        return out

    @pl.when(f == 0)
    def _():
        acc_ref[...] = partial_out()

    @pl.when((f > 0) & (f < last))
    def _():
        acc_ref[...] += partial_out()

    @pl.when(f == last)
    def _():
        y_ref[...] = (acc_ref[...] + partial_out()).astype(bf16)


def _ffn(xe, w_gate, w_up, w_down):
    rows = BATCH * CAPACITY
    return pl.pallas_call(
        _ffn_kernel,
        grid=(N_EXPERTS, D_FF // FF_TILE),
        in_specs=[
            pl.BlockSpec((None, rows, D_MODEL), lambda e, f: (e, 0, 0)),
            pl.BlockSpec((None, D_MODEL, FF_TILE), lambda e, f: (e, 0, f)),
            pl.BlockSpec((None, D_MODEL, FF_TILE), lambda e, f: (e, 0, f)),
            pl.BlockSpec((None, FF_TILE, D_MODEL), lambda e, f: (e, f, 0)),
        ],
        out_specs=pl.BlockSpec((None, rows, D_MODEL), lambda e, f: (e, 0, 0)),
        out_shape=jax.ShapeDtypeStruct((N_EXPERTS, rows, D_MODEL), bf16),
        scratch_shapes=[pltpu.VMEM((rows, D_MODEL), f32)],
        compiler_params=_params(("arbitrary", "arbitrary"), 56),
        name="ffn",
    )(xe, w_gate, w_up, w_down)


SCATTER_UNROLL = 8
COPY_ROWS = 2048
NORM_TOKENS = 256


def _combine_kernel(row_ref, gate_ref, y_ref, x1_ref, g_ref, o_ref, acc_ref, tile_ref):
    step = pl.program_id(1)

    @pl.when(step == 0)
    def _():
        def clear(i, c):
            rows = pl.ds(pl.multiple_of(i * COPY_ROWS, COPY_ROWS), COPY_ROWS)
            acc_ref[rows, :] = jnp.zeros((COPY_ROWS, LANES), f32)
            return c
        lax.fori_loop(0, SEQ * SLAB_ROWS // COPY_ROWS, clear, 0)

    for k in range(EXPERTS_PER_STEP):
        for j in range(SLAB_ROWS):
            tile_ref[k, pl.ds(j * TILE_PITCH, CAPACITY), :] = (
                y_ref[k, :, j * LANES:(j + 1) * LANES].astype(f32))
        for first in range(0, CAPACITY, SCATTER_UNROLL):
            updates = []
            for slot in range(first, first + SCATTER_UNROLL):
                row = row_ref[0, k * CAPACITY + slot]
                gate = gate_ref[0, k * CAPACITY + slot]
                rows = pl.ds(pl.multiple_of(row, SLAB_ROWS), SLAB_ROWS)
                contrib = tile_ref[k, pl.ds(slot, SLAB_ROWS, stride=TILE_PITCH), :]
                updates.append((rows, acc_ref[rows, :] + gate * contrib))
            for rows, value in updates:
                acc_ref[rows, :] = value

    @pl.when(step == N_EXPERTS // EXPERTS_PER_STEP - 1)
    def _():
        def norm(i, c):
            first = i * (NORM_TOKENS * SLAB_ROWS)
            rows = pl.ds(pl.multiple_of(i * NORM_TOKENS, NORM_TOKENS), NORM_TOKENS)
            chunks = [x1_ref[rows, j * LANES:(j + 1) * LANES]
                      + acc_ref[pl.ds(first + j, NORM_TOKENS, stride=SLAB_ROWS), :]
                      for j in range(SLAB_ROWS)]
            squares = functools.reduce(lambda a, b: a + b, [ch * ch for ch in chunks])
            inv = lax.rsqrt(jnp.sum(squares, axis=1, keepdims=True) * (1.0 / D_MODEL) + RMS_EPS)
            for j in range(SLAB_ROWS):
                cols = slice(j * LANES, (j + 1) * LANES)
                o_ref[rows, cols] = chunks[j] * inv * g_ref[:, cols]
            return c
        lax.fori_loop(0, SEQ // NORM_TOKENS, norm, 0)


def _combine(rows, gate, y, x1, g_final):
    return pl.pallas_call(
        _combine_kernel,
        grid=(BATCH, N_EXPERTS // EXPERTS_PER_STEP),
        in_specs=[
            _slot_list_spec(), _slot_list_spec(),
            pl.BlockSpec((EXPERTS_PER_STEP, None, CAPACITY, D_MODEL), lambda b, s: (s, b, 0, 0)),
            pl.BlockSpec((SEQ, D_MODEL), lambda b, s: (b, 0)),
            pl.BlockSpec((1, D_MODEL), lambda b, s: (0, 0)),
        ],
        out_specs=pl.BlockSpec((SEQ, D_MODEL), lambda b, s: (b, 0)),
        out_shape=jax.ShapeDtypeStruct((TOKENS, D_MODEL), f32),
        scratch_shapes=[pltpu.VMEM((SEQ * SLAB_ROWS, LANES), f32),
                        pltpu.VMEM((EXPERTS_PER_STEP, SLAB_ROWS * TILE_PITCH, LANES), f32)],
        compiler_params=_params(("arbitrary", "arbitrary"), 56),
        name="combine",
    )(rows, gate, y, x1, g_final)


def _rope_freqs():
    inv_freq = ROPE_THETA ** (-jnp.arange(ROT_HALF, dtype=f32) / ROT_HALF)
    return inv_freq.reshape(ROT_HALF, 1)


def kernel(x, positions, g_mix, w_in, conv_w, g_conv_out, g_attn_out, w_out, g_ffn, w_router,
           w_gate, w_up, w_down, g_final):
    x2 = x.reshape(TOKENS, D_MODEL)
    rope = _rope_tables(positions.reshape(BATCH, 1, SEQ), _rope_freqs())
    yc, edge, qkv, norms = _in_proj(x2, g_mix[0].reshape(1, D_MODEL), w_in[0].astype(bf16),
                                    conv_w[0], rope)
    norms = norms.reshape(BATCH, SEQ // ROW_TILE, SUBLANES, LANES)[:, :, 0, :2 * N_HEADS]
    norms = norms.max(axis=1).reshape(BATCH, 2, N_HEADS)
    scores_bounded = jnp.all(norms[:, 0] * norms[:, 1] * NORM_SLACK <= SAFE_SCORE ** 2)
    qkv3 = qkv.reshape(BATCH, SEQ, 3 * ATTN_WIDTH)
    ya = lax.cond(scores_bounded, functools.partial(_attention, exact_max=False),
                  functools.partial(_attention, exact_max=True), qkv3)
    x1, h2, logits_t = _out_proj(yc, edge, ya.reshape(TOKENS, ATTN_WIDTH), x2, w_out[0].astype(bf16),
                                 g_conv_out[0].reshape(1, CONV_WIDTH),
                                 g_attn_out[0].reshape(1, ATTN_WIDTH), g_ffn[0].reshape(1, D_MODEL),
                                 w_router[0].T)
    idx, gate = _route(*_select(logits_t))
    idx = idx.reshape(BATCH, 1, N_EXPERTS * CAPACITY)
    gate = gate.reshape(BATCH, 1, N_EXPERTS * CAPACITY)
    xe = _gather(idx, h2)
    y = _ffn(xe.reshape(N_EXPERTS, BATCH * CAPACITY, D_MODEL), w_gate[0], w_up[0], w_down[0])
    out = _combine(idx, gate, y.reshape(N_EXPERTS, BATCH, CAPACITY, D_MODEL), x1,
                   g_final.reshape(1, D_MODEL))
    return out.reshape(BATCH, SEQ, D_MODEL)
```
